```python
import jax, jax.numpy as jnp
from jax import lax
import numpy as np

D_MODEL = 1024
BATCH = 4
SEQ = 8192
DEPTH = 2

RMS_EPS = 1e-6
LN_EPS = 1e-5
N_EVEN = (DEPTH + 1) // 2
N_ODD = DEPTH // 2

GLA_WIDTH = D_MODEL // 2
GLA_HEADS = 4
GLA_DK = GLA_WIDTH // 2 // GLA_HEADS
GLA_DV = GLA_WIDTH // GLA_HEADS
GLA_KEY = GLA_HEADS * GLA_DK
GLA_GATE_RANK = 16
GLA_TAU = 16.0
GLA_CHUNK = 64
SGU_WIDTH = D_MODEL - GLA_WIDTH
SGU_GROUPS = 4
SGU_GROUP_DIM = SGU_WIDTH // SGU_GROUPS
SGU_CHUNK = 128
EVEN_SPLITS = (GLA_KEY, GLA_KEY, GLA_WIDTH, GLA_WIDTH, GLA_GATE_RANK, SGU_WIDTH, SGU_WIDTH)
EVEN_IN = sum(EVEN_SPLITS)

RWKV_WIDTH = D_MODEL // 2
RWKV_HEAD = 64
RWKV_HEADS = RWKV_WIDTH // RWKV_HEAD
RWKV_DECAY_RANK = 32
RWKV_AAA_RANK = 32
RWKV_GATE_RANK = 96
RWKV_GN_EPS = 64e-5
RWKV_SPLITS = (RWKV_WIDTH, RWKV_WIDTH, RWKV_WIDTH, RWKV_DECAY_RANK, RWKV_AAA_RANK, RWKV_GATE_RANK)
RWKV_IN = sum(RWKV_SPLITS)
CONV_WIDTH = D_MODEL - RWKV_WIDTH
CONV_KERNEL = 31
ODD_IN = RWKV_IN + 2 * CONV_WIDTH

D_FF = 4 * D_MODEL

kernel_name = "hybrid_gla_sgu_rwkv7_conformer_trunk"


def _split(h, sizes):
    idx = [int(i) for i in np.cumsum(sizes)[:-1]]
    return jnp.split(h, idx, axis=-1)


def _rmsnorm(x, g):
    xf = x.astype(jnp.float32)
    y = xf * lax.rsqrt(jnp.mean(xf * xf, axis=-1, keepdims=True) + RMS_EPS)
    return (y * g.astype(jnp.float32)).astype(x.dtype)


def _layernorm(x, g, b, eps):
    mu = jnp.mean(x, axis=-1, keepdims=True)
    var = jnp.mean(jnp.square(x - mu), axis=-1, keepdims=True)
    return (x - mu) * lax.rsqrt(var + eps) * g.astype(jnp.float32) + b.astype(jnp.float32)


def _token_shift(x):
    return jnp.pad(x[:, :-1], ((0, 0), (1, 0), (0, 0)))


def _gla(q, k, v, g, alpha_lr, w_alpha2, b_alpha, gn_gain):
    B, T, _ = q.shape
    N = T // GLA_CHUNK
    log_a = jax.nn.log_sigmoid(alpha_lr @ w_alpha2.astype(jnp.float32) + b_alpha.astype(jnp.float32)) / GLA_TAU

    def heads(t, d):
        return t.reshape(B, N, GLA_CHUNK, GLA_HEADS, d).transpose(0, 3, 1, 2, 4)

    qh = heads(q, GLA_DK) * (GLA_DK ** -0.5)
    kh = heads(k, GLA_DK)
    vh = heads(v, GLA_DV)
    b = jnp.cumsum(heads(log_a, GLA_DK), axis=3)
    b_last = b[:, :, :, -1:, :]
    q_dec = qh * jnp.exp(b)
    k_dec = kh * jnp.exp(-b)
    k_state = kh * jnp.exp(b_last - b)
    mask = jnp.tril(jnp.ones((GLA_CHUNK, GLA_CHUNK), dtype=bool))
    att = jnp.where(mask, jnp.einsum('bhncd,bhnsd->bhncs', q_dec, k_dec), 0.0)
    o_intra = jnp.einsum('bhncs,bhnsv->bhncv', att, vh)

    def step(S, inp):
        qd, ks, vv, dl = inp
        o = jnp.einsum('bhcd,bhdv->bhcv', qd, S)
        S = S * dl[..., None] + jnp.einsum('bhcd,bhcv->bhdv', ks, vv)
        return S, o

    xs = (jnp.moveaxis(q_dec, 2, 0), jnp.moveaxis(k_state, 2, 0), jnp.moveaxis(vh, 2, 0),
          jnp.moveaxis(jnp.exp(b_last[:, :, :, 0, :]), 2, 0))
    S0 = jnp.zeros((B, GLA_HEADS, GLA_DK, GLA_DV), jnp.float32)
    _, o_inter = lax.scan(step, S0, xs)
    o = o_intra + jnp.moveaxis(o_inter, 0, 2)
    o = o * lax.rsqrt(jnp.mean(o * o, axis=-1, keepdims=True) + RMS_EPS)
    o = o.transpose(0, 2, 3, 1, 4).reshape(B, T, GLA_WIDTH) * gn_gain.astype(jnp.float32)
    return o * jax.nn.silu(g)


def _sgu(u, sv, ln_g, ln_b, w_s, b_s):
    B, T, _ = u.shape
    N = T // SGU_CHUNK
    u = jax.nn.gelu(u)
    sv = _layernorm(jax.nn.gelu(sv), ln_g, ln_b, LN_EPS)
    vh = sv.reshape(B, N, SGU_CHUNK, SGU_GROUPS, SGU_GROUP_DIM)
    mask = jnp.tril(jnp.ones((SGU_CHUNK, SGU_CHUNK), dtype=bool))
    w = jnp.where(mask[None], w_s.astype(jnp.float32), 0.0)
    s = jnp.einsum('gts,bnsgc->bntgc', w, vh) + b_s.astype(jnp.float32).T[:, :, None]
    return u * s.reshape(B, T, SGU_WIDTH)


def _rwkv7_scan(r, w, k, v, kk, a):
    B, T, H, N = r.shape

    def step(S, inp):
        r_t, w_t, k_t, v_t, kk_t, a_t = inp
        sa = -jnp.einsum('bhvk,bhk->bhv', S, kk_t)
        S = (S * w_t[:, :, None, :] + sa[..., None] * (kk_t * a_t)[:, :, None, :]
             + v_t[..., None] * k_t[:, :, None, :])
        return S, jnp.einsum('bhvk,bhk->bhv', S, r_t)

    xs = tuple(jnp.moveaxis(t, 1, 0) for t in (r, w, k, v, kk, a))
    _, y = lax.scan(step, jnp.zeros((B, H, N, N), jnp.float32), xs)
    return jnp.moveaxis(y, 0, 1)


def _rwkv7(h, mu, w0, w2, a0, a2, g2, k_k, k_a, r_k, gn_g, gn_b):
    B, T, _ = h.shape
    f32 = jnp.float32
    h = h + (_token_shift(h) - h) * mu.astype(f32)
    r, k, v, xw, xa, xg = _split(h, RWKV_SPLITS)
    wlog = -jax.nn.softplus(-(w0.astype(f32) + jnp.tanh(xw) @ w2.astype(f32))) - 0.5
    decay = jnp.exp(-jnp.exp(wlog))
    a = jax.nn.sigmoid(a0.astype(f32) + xa @ a2.astype(f32))
    g = jax.nn.sigmoid(xg) @ g2.astype(f32)
    hd = lambda t: t.reshape(B, T, RWKV_HEADS, RWKV_HEAD)
    kk = hd(k * k_k.astype(f32))
    kk = kk / jnp.maximum(jnp.sqrt(jnp.sum(kk * kk, axis=-1, keepdims=True)), 1e-12)
    k = k * (1.0 + (a - 1.0) * k_a.astype(f32))
    rh, kh, vh = hd(r), hd(k), hd(v)
    y = _rwkv7_scan(rh, hd(decay), kh, vh, kk, hd(a))
    mu_y = jnp.mean(y, axis=-1, keepdims=True)
    var_y = jnp.mean(jnp.square(y - mu_y), axis=-1, keepdims=True)
    y = ((y - mu_y) * lax.rsqrt(var_y + RWKV_GN_EPS)).reshape(B, T, RWKV_WIDTH)
    y = y * gn_g.astype(f32) + gn_b.astype(f32)
    bonus = jnp.sum(rh * kh * r_k.astype(f32), axis=-1, keepdims=True) * vh
    y = y + bonus.reshape(B, T, RWKV_WIDTH)
    return y * g


def _conformer_conv(h, conv_w, conv_b, ln_g, ln_b):
    a, gate = _split(h, (CONV_WIDTH, CONV_WIDTH))
    z = a * jax.nn.sigmoid(gate)
    z = jnp.pad(z, ((0, 0), (CONV_KERNEL - 1, 0), (0, 0)))
    y = lax.conv_general_dilated(z, conv_w.astype(jnp.float32)[:, None, :], window_strides=(1,),
                                 padding='VALID', dimension_numbers=('NWC', 'WIO', 'NWC'),
                                 feature_group_count=CONV_WIDTH)
    y = y + conv_b.astype(jnp.float32)
    return jax.nn.silu(_layernorm(y, ln_g, ln_b, LN_EPS))


def setup_inputs(seed: int = 0) -> dict:
    key = jax.random.key(seed)
    keys = jax.random.split(key, 40)
    f32 = jnp.float32
    D = D_MODEL

    def nrm(i, shape, scale):
        return scale * jax.random.normal(keys[i], shape, f32)

    def gain(i, shape):
        return 1.0 + nrm(i, shape, 0.05)

    w0_ramp = jnp.linspace(-6.5, -1.5, RWKV_WIDTH, dtype=f32)
    return {
        "x": nrm(0, (BATCH, SEQ, D), 1.0),
        "norm_mix": gain(1, (DEPTH, D)),
        "norm_ffn": gain(2, (DEPTH, D)),
        "w_up": nrm(3, (DEPTH, D, D_FF), D ** -0.5),
        "w_down": nrm(4, (DEPTH, D_FF, D), D_FF ** -0.5),
        "norm_final": gain(5, (D,)),
        "even_w_in": nrm(6, (N_EVEN, D, EVEN_IN), D ** -0.5),
        "even_w_out": nrm(7, (N_EVEN, D, D), D ** -0.5),
        "gla_w_alpha2": nrm(8, (N_EVEN, GLA_GATE_RANK, GLA_KEY), GLA_GATE_RANK ** -0.5),
        "gla_b_alpha": nrm(9, (N_EVEN, GLA_KEY), 0.5),
        "gla_norm": gain(10, (N_EVEN, GLA_WIDTH)),
        "sgu_ln_g": gain(11, (N_EVEN, SGU_WIDTH)),
        "sgu_ln_b": nrm(12, (N_EVEN, SGU_WIDTH), 0.02),
        "sgu_w": nrm(13, (N_EVEN, SGU_GROUPS, SGU_CHUNK, SGU_CHUNK), SGU_CHUNK ** -0.5),
        "sgu_b": gain(14, (N_EVEN, SGU_GROUPS, SGU_CHUNK)),
        "odd_w_in": nrm(15, (N_ODD, D, ODD_IN), D ** -0.5),
        "odd_w_out": nrm(16, (N_ODD, D, D), D ** -0.5),
        "rwkv_mu": jax.random.uniform(keys[17], (N_ODD, RWKV_IN), f32),
        "rwkv_w0": w0_ramp[None, :] + nrm(18, (N_ODD, RWKV_WIDTH), 0.1),
        "rwkv_w2": nrm(19, (N_ODD, RWKV_DECAY_RANK, RWKV_WIDTH), 0.5 * RWKV_DECAY_RANK ** -0.5),
        "rwkv_a0": nrm(20, (N_ODD, RWKV_WIDTH), 0.1),
        "rwkv_a2": nrm(21, (N_ODD, RWKV_AAA_RANK, RWKV_WIDTH), RWKV_AAA_RANK ** -0.5),
        "rwkv_g2": nrm(22, (N_ODD, RWKV_GATE_RANK, RWKV_WIDTH), RWKV_GATE_RANK ** -0.5),
        "rwkv_k_k": 0.85 + nrm(23, (N_ODD, RWKV_WIDTH), 0.05),
        "rwkv_k_a": gain(24, (N_ODD, RWKV_WIDTH)),
        "rwkv_r_k": nrm(25, (N_ODD, RWKV_HEADS, RWKV_HEAD), 0.1),
        "rwkv_gn_g": gain(26, (N_ODD, RWKV_WIDTH)),
        "rwkv_gn_b": nrm(27, (N_ODD, RWKV_WIDTH), 0.02),
        "conv_w": nrm(28, (N_ODD, CONV_KERNEL, CONV_WIDTH), CONV_KERNEL ** -0.5),
        "conv_b": nrm(29, (N_ODD, CONV_WIDTH), 0.02),
        "conv_ln_g": gain(30, (N_ODD, CONV_WIDTH)),
        "conv_ln_b": nrm(31, (N_ODD, CONV_WIDTH), 0.02),
    }


def reference(x, norm_mix, norm_ffn, w_up, w_down, norm_final, even_w_in, even_w_out,
              gla_w_alpha2, gla_b_alpha, gla_norm, sgu_ln_g, sgu_ln_b, sgu_w, sgu_b,
              odd_w_in, odd_w_out, rwkv_mu, rwkv_w0, rwkv_w2, rwkv_a0, rwkv_a2, rwkv_g2,
              rwkv_k_k, rwkv_k_a, rwkv_r_k, rwkv_gn_g, rwkv_gn_b, conv_w, conv_b,
              conv_ln_g, conv_ln_b):
    for layer in range(DEPTH):
        j = layer // 2
        h = _rmsnorm(x, norm_mix[layer])
        if layer % 2 == 0:
            p = (h @ even_w_in[j]).astype(jnp.float32)
            q, k, v, g, alr, u, sv = _split(p, EVEN_SPLITS)
            o_a = _gla(q, k, v, g, alr, gla_w_alpha2[j], gla_b_alpha[j], gla_norm[j])
            o_b = _sgu(u, sv, sgu_ln_g[j], sgu_ln_b[j], sgu_w[j], sgu_b[j])
            m = jnp.concatenate([o_a, o_b], axis=-1).astype(x.dtype)
            x = x + m @ even_w_out[j]
        else:
            p = (h @ odd_w_in[j]).astype(jnp.float32)
            o_c = _rwkv7(p[..., :RWKV_IN], rwkv_mu[j], rwkv_w0[j], rwkv_w2[j], rwkv_a0[j],
                         rwkv_a2[j], rwkv_g2[j], rwkv_k_k[j], rwkv_k_a[j], rwkv_r_k[j],
                         rwkv_gn_g[j], rwkv_gn_b[j])
            o_d = _conformer_conv(p[..., RWKV_IN:], conv_w[j], conv_b[j], conv_ln_g[j], conv_ln_b[j])
            m = jnp.concatenate([o_c, o_d], axis=-1).astype(x.dtype)
            x = x + m @ odd_w_out[j]
        f = jnp.square(jax.nn.relu(_rmsnorm(x, norm_ffn[layer]) @ w_up[layer]))
        x = x + f @ w_down[layer]
    return _rmsnorm(x, norm_final)
```

```python
import functools

import jax
import jax.numpy as jnp
from jax import lax
from jax.experimental import pallas as pl
from jax.experimental.pallas import tpu as pltpu

F32 = jnp.float32
BF16 = jnp.bfloat16

RMS_EPS = 1e-6
LN_EPS = 1e-5

GLA_HEADS = 4
GLA_DK = 64
GLA_DV = 128
GLA_KEY = GLA_HEADS * GLA_DK
GLA_WIDTH = GLA_HEADS * GLA_DV
GLA_GATE_RANK = 16
GLA_TAU = 16.0
GLA_CHUNK = 64
SGU_WIDTH = 512
SGU_GROUPS = 4
SGU_CHUNK = 128
EVEN_COLS = 2688

RWKV_WIDTH = 512
RWKV_HEAD = 64
RWKV_HEADS = 8
RWKV_DECAY_RANK = 32
RWKV_AAA_RANK = 32
RWKV_GATE_RANK = 96
RWKV_GN_EPS = 64e-5
RWKV_IN = 1696
RWKV_COLS = 1792
RWKV_CHUNK = 64
RWKV_GROUP = 4
CONV_WIDTH = 512
CONV_KERNEL = 31
CONV_HALO = 32
ODD_COLS = RWKV_COLS + 2 * CONV_WIDTH

LANE = 128
VMEM_LIMIT = 56 * 1024 * 1024


def _iota(shape, dim):
    return lax.broadcasted_iota(jnp.int32, shape, dim)


def _dot(a, b):
    return jnp.dot(a, b, preferred_element_type=F32)


def _dot_nt(a, b):
    return lax.dot_general(a, b, (((1,), (1,)), ((), ())), preferred_element_type=F32)


def _dot_tn(a, b):
    return lax.dot_general(a, b, (((0,), (0,)), ((), ())), preferred_element_type=F32)


def _split2(x):
    hi = x.astype(BF16)
    lo = (x - hi.astype(F32)).astype(BF16)
    return hi, lo


def _dot_exact_lhs(m_bf16, x):
    hi, lo = _split2(x)
    return _dot(m_bf16, hi) + _dot(m_bf16, lo)


def _dot_exact_rhs(x, m_bf16):
    hi, lo = _split2(x)
    return _dot(hi, m_bf16) + _dot(lo, m_bf16)


def _rms(x, g, eps):
    return x * lax.rsqrt(jnp.mean(x * x, axis=-1, keepdims=True) + eps) * g


def _layernorm(x, g, b, eps):
    mu = jnp.mean(x, axis=-1, keepdims=True)
    xc = x - mu
    var = jnp.mean(xc * xc, axis=-1, keepdims=True)
    return xc * lax.rsqrt(var + eps) * g + b


def _sigmoid(x):
    return 1.0 / (1.0 + jnp.exp(-x))


def _silu(x):
    return x * _sigmoid(x)


def _softplus(x):
    return jnp.maximum(x, 0.0) + jnp.log(1.0 + jnp.exp(-jnp.abs(x)))


def _gelu_tanh(x):
    c = 0.7978845608028654
    return 0.5 * x * (1.0 + jnp.tanh(c * (x + 0.044715 * (x * x * x))))


def _tril_ones(n, dtype):
    return jnp.where(_iota((n, n), 0) >= _iota((n, n), 1), 1.0, 0.0).astype(dtype)


def _even_kernel(x_ref, gn_ref, win_ref, wal_ref, bal_ref, glan_ref, lng_ref, lnb_ref,
                 sguw_ref, sgub_ref, wout_ref, o_ref,
                 p_ref, la_ref, oa_ref, ob_ref, st_ref, *, tb):
    t_idx = pl.program_id(1)

    @pl.when(t_idx == 0)
    def _():
        st_ref[...] = jnp.zeros_like(st_ref)

    x = x_ref[...]
    h = _rms(x, gn_ref[...], RMS_EPS).astype(BF16)
    p_ref[...] = _dot(h, win_ref[...])

    alr = p_ref[:, 2560:2688]
    a_hi, a_lo = _split2(alr)
    w_hi, w_lo = _split2(wal_ref[...])
    z = _dot(a_hi, w_hi) + _dot(a_hi, w_lo) + _dot(a_lo, w_hi) + bal_ref[...]
    la_ref[...] = -_softplus(-z) * (1.0 / GLA_TAU)

    tril = _tril_ones(GLA_CHUNK, BF16)
    rk = _iota((GLA_KEY, GLA_KEY), 0) >> 6
    ck = _iota((GLA_KEY, GLA_KEY), 1) >> 6
    k4_mask = rk == ck
    rv = _iota((GLA_KEY, GLA_WIDTH), 0) >> 6
    cv = _iota((GLA_KEY, GLA_WIDTH), 1) >> 7
    vbd_mask = rv == cv
    causal = _iota((GLA_CHUNK, GLA_KEY), 0) >= (_iota((GLA_CHUNK, GLA_KEY), 1) & 63)
    rs = _iota((GLA_WIDTH, GLA_KEY), 0) >> 7
    cs = _iota((GLA_WIDTH, GLA_KEY), 1) >> 6
    st_mask = rs == cs

    def gla_chunk(c, carry):
        rows = pl.ds(pl.multiple_of(c * GLA_CHUNK, GLA_CHUNK), GLA_CHUNK)
        q = p_ref[rows, 0:256] * (GLA_DK ** -0.5)
        k = p_ref[rows, 256:512]
        v = p_ref[rows, 512:1024]
        b = _dot_exact_lhs(tril, la_ref[rows, :])
        b_last = b[GLA_CHUNK - 1:GLA_CHUNK, :]
        q_dec = q * jnp.exp(b)
        k_dec = k * jnp.exp(-b)
        k_state = k * jnp.exp(b_last - b)
        st = st_ref[...]
        k4 = jnp.where(k4_mask, jnp.concatenate([k_dec] * GLA_HEADS, axis=0), 0.0)
        att = jnp.where(causal, _dot_nt(q_dec, k4), 0.0)
        v_bd = jnp.where(vbd_mask, jnp.concatenate([v] * GLA_HEADS, axis=0), 0.0)
        o = _dot(att, v_bd) + _dot_nt(q_dec, st)
        st_ref[...] = st * jnp.exp(b_last) + jnp.where(st_mask, _dot_tn(v, k_state), 0.0)
        for hh in range(GLA_HEADS):
            oh = o[:, hh * GLA_DV:(hh + 1) * GLA_DV]
            oh = oh * lax.rsqrt(jnp.mean(oh * oh, axis=-1, keepdims=True) + RMS_EPS)
            oa_ref[rows, hh * GLA_DV:(hh + 1) * GLA_DV] = oh
        return carry

    lax.fori_loop(0, tb // GLA_CHUNK, gla_chunk, 0)

    g = p_ref[:, 1024:1536]
    o_a = oa_ref[...] * glan_ref[...] * _silu(g)

    sv = _layernorm(_gelu_tanh(p_ref[:, 2048:2560]), lng_ref[...], lnb_ref[...], LN_EPS)
    sv = sv.astype(BF16)
    tri128 = _iota((SGU_CHUNK, SGU_CHUNK), 0) >= _iota((SGU_CHUNK, SGU_CHUNK), 1)
    for gg in range(SGU_GROUPS):
        wg = jnp.where(tri128, sguw_ref[gg], 0.0).astype(BF16)
        for c in range(tb // SGU_CHUNK):
            r0 = c * SGU_CHUNK
            s = _dot(wg, sv[r0:r0 + SGU_CHUNK, gg * LANE:(gg + 1) * LANE])
            s = s + sgub_ref[:, gg * LANE:(gg + 1) * LANE]
            u = _gelu_tanh(p_ref[r0:r0 + SGU_CHUNK, 1536 + gg * LANE:1536 + (gg + 1) * LANE])
            ob_ref[r0:r0 + SGU_CHUNK, gg * LANE:(gg + 1) * LANE] = u * s

    m = _dot(o_a.astype(BF16), wout_ref[0:512, :]) + _dot(ob_ref[...].astype(BF16), wout_ref[512:1024, :])
    o_ref[...] = x + m


def _even_layer(x, norm_g, w_in, w_out, w_alpha2, b_alpha, gla_norm, ln_g, ln_b, sgu_w, sgu_b, *, tb):
    B, T, D = x.shape
    w_in_r = jnp.concatenate(
        [w_in[:, :1536], w_in[:, 1552:2576], w_in[:, 1536:1552],
         jnp.zeros((D, EVEN_COLS - 2576), w_in.dtype)], axis=1).astype(BF16)
    wal = jnp.zeros((LANE, GLA_KEY), F32).at[:GLA_GATE_RANK].set(w_alpha2.astype(F32))
    sgub = jnp.repeat(sgu_b.astype(F32).T, LANE, axis=1)
    row = lambda a: a.astype(F32).reshape(1, -1)
    const = lambda shape: pl.BlockSpec(shape, lambda b, t: (0,) * len(shape))
    kern = functools.partial(_even_kernel, tb=tb)
    return pl.pallas_call(
        kern,
        grid=(B, T // tb),
        in_specs=[
            pl.BlockSpec((None, tb, D), lambda b, t: (b, t, 0)),
            const((1, D)), const((D, EVEN_COLS)), const((LANE, GLA_KEY)), const((1, GLA_KEY)),
            const((1, GLA_WIDTH)), const((1, SGU_WIDTH)), const((1, SGU_WIDTH)),
            const((SGU_GROUPS, SGU_CHUNK, SGU_CHUNK)), const((SGU_CHUNK, SGU_WIDTH)),
            const((D, D)),
        ],
        out_specs=pl.BlockSpec((None, tb, D), lambda b, t: (b, t, 0)),
        out_shape=jax.ShapeDtypeStruct((B, T, D), F32),
        scratch_shapes=[
            pltpu.VMEM((tb, EVEN_COLS), F32),
            pltpu.VMEM((tb, GLA_KEY), F32),
            pltpu.VMEM((tb, GLA_WIDTH), F32),
            pltpu.VMEM((tb, SGU_WIDTH), F32),
            pltpu.VMEM((GLA_WIDTH, GLA_KEY), F32),
        ],
        compiler_params=pltpu.CompilerParams(
            dimension_semantics=("arbitrary", "arbitrary"), vmem_limit_bytes=VMEM_LIMIT),
        name="even_mixer",
    )(x, row(norm_g), w_in_r, wal, row(b_alpha), row(gla_norm), row(ln_g), row(ln_b),
      sgu_w.astype(F32), sgub, w_out.astype(BF16))


def _odd_kernel(x_ref, gn_ref, win_ref, mu_ref, wlr_ref, w0_ref, a0_ref, kk_ref, ka_ref, rk_ref,
                gng_ref, gnb_ref, cw_ref, cb_ref, clg_ref, clb_ref, wout_ref, o_ref,
                p_ref, prev_ref, zc_ref, r_s, kk_s, k2_s, qa_s, v_s, lw_s, y_s, z_s, *, tb):
    t_idx = pl.program_id(1)
    gw = RWKV_GROUP * RWKV_HEAD
    n_groups = RWKV_WIDTH // gw

    @pl.when(t_idx == 0)
    def _():
        prev_ref[...] = jnp.zeros_like(prev_ref)
        zc_ref[0:CONV_HALO, :] = jnp.zeros((CONV_HALO, CONV_WIDTH), F32)
        z_s[...] = jnp.zeros_like(z_s)

    @pl.when(t_idx != 0)
    def _():
        zc_ref[0:CONV_HALO, :] = zc_ref[tb:tb + CONV_HALO, :]

    x = x_ref[...]
    h = _rms(x, gn_ref[...], RMS_EPS).astype(BF16)
    p_ref[...] = _dot(h, win_ref[...])

    zc_ref[CONV_HALO:CONV_HALO + tb, :] = p_ref[:, RWKV_COLS:RWKV_COLS + CONV_WIDTH] * _sigmoid(
        p_ref[:, RWKV_COLS + CONV_WIDTH:ODD_COLS])
    acc = jnp.zeros((tb, CONV_WIDTH), F32) + cb_ref[...]
    for j in range(CONV_KERNEL):
        off = CONV_HALO - (CONV_KERNEL - 1) + j
        acc = acc + zc_ref[off:off + tb, :] * cw_ref[j:j + 1, :]
    o_d = _silu(_layernorm(acc, clg_ref[...], clb_ref[...], LN_EPS))

    pr = p_ref[:, 0:RWKV_COLS]
    shifted = pltpu.roll(pr, 1, 0)
    shifted = jnp.where(_iota((tb, RWKV_COLS), 0) == 0, prev_ref[7:8, :], shifted)
    prev_ref[...] = p_ref[tb - 8:tb, 0:RWKV_COLS]
    hs = pr + (shifted - pr) * mu_ref[...]
    r = hs[:, 0:512]
    k = hs[:, 512:1024]
    v = hs[:, 1024:1536]
    lr = hs[:, 1536:1792]
    lane = _iota((tb, 256), 1)
    f = jnp.where(lane < RWKV_DECAY_RANK, jnp.tanh(lr),
                  jnp.where(lane < RWKV_DECAY_RANK + RWKV_AAA_RANK, lr, _sigmoid(lr)))
    lo3 = _dot(f.astype(BF16), wlr_ref[...])
    wlog = -_softplus(-(w0_ref[...] + lo3[:, 0:512])) - 0.5
    lw_s[...] = -jnp.exp(wlog)
    a = _sigmoid(a0_ref[...] + lo3[:, 512:1024])
    gate = lo3[:, 1024:1536]

    hm = jnp.where((_iota((RWKV_WIDTH, RWKV_WIDTH), 0) >> 6) == (_iota((RWKV_WIDTH, RWKV_WIDTH), 1) >> 6),
                   1.0, 0.0).astype(BF16)
    kk = k * kk_ref[...]
    kk = kk / jnp.maximum(jnp.sqrt(_dot_exact_rhs(kk * kk, hm)), 1e-12)
    k2 = k * (1.0 + (a - 1.0) * ka_ref[...])
    bonus = _dot_exact_rhs(r * k2 * rk_ref[...], hm) * v
    r_s[...] = r
    kk_s[...] = kk
    k2_s[...] = k2
    qa_s[...] = kk * a
    v_s[...] = v

    tril = _tril_ones(RWKV_CHUNK, BF16)
    ri = _iota((gw, gw), 0)
    ci = _iota((gw, gw), 1)
    same = (ri >> 6) == (ci >> 6)
    strict = same & (ri > ci)
    incl = same & (ri >= ci)
    eye = jnp.where(ri == ci, 1.0, 0.0)

    def bd(xs):
        return jnp.where(same, jnp.concatenate([xs] * RWKV_GROUP, axis=0), 0.0)

    def rwkv_chunk(c, carry):
        rows = pl.ds(pl.multiple_of(c * RWKV_CHUNK, RWKV_CHUNK), RWKV_CHUNK)
        lw = lw_s[rows, :]
        cum = _dot_exact_lhs(tril, lw)
        last = cum[RWKV_CHUNK - 1:RWKV_CHUNK, :]
        e_pos = jnp.exp(cum)
        e_neg = jnp.exp(-cum)
        e_prev = jnp.exp(cum - lw)
        e_hat = jnp.exp(last - cum)
        e_last = jnp.exp(last)
        kkc = kk_s[rows, :]
        qac = qa_s[rows, :]
        k2c = k2_s[rows, :]
        r_t = r_s[rows, :] * e_pos
        p_t = -kkc * e_prev
        q_t = qac * e_neg
        k_t = k2c * e_neg
        q_h = qac * e_hat
        k_h = k2c * e_hat
        vc = v_s[rows, :]
        for j in range(n_groups):
            sl = slice(j * gw, (j + 1) * gw)
            P2, R2, Q2, K2 = bd(p_t[:, sl]), bd(r_t[:, sl]), bd(q_t[:, sl]), bd(k_t[:, sl])
            Qh2, Kh2, V2 = bd(q_h[:, sl]), bd(k_h[:, sl]), bd(vc[:, sl])
            Z = z_s[j]
            PR = jnp.concatenate([P2, R2], axis=0)
            sc = _dot_nt(PR, jnp.concatenate([Q2, K2], axis=0))
            A_pq = jnp.where(strict, sc[0:gw, 0:gw], 0.0)
            A_pk = jnp.where(strict, sc[0:gw, gw:2 * gw], 0.0)
            A_rq = jnp.where(incl, sc[gw:2 * gw, 0:gw], 0.0)
            A_rk = jnp.where(incl, sc[gw:2 * gw, gw:2 * gw], 0.0)
            Xp = _dot(A_pq, A_pq)
            Tm = eye + A_pq
            for _ in range(4):
                XT = _dot(Xp, jnp.concatenate([Xp, Tm], axis=1))
                Xp = XT[:, 0:gw]
                Tm = Tm + XT[:, gw:2 * gw]
            Tm = Tm + _dot(Xp, Tm)
            hz = _dot_nt(PR, Z)
            U2 = _dot(Tm, _dot(A_pk, V2) + hz[0:gw])
            UV = jnp.concatenate([U2, V2], axis=0)
            Y2 = hz[gw:2 * gw] + _dot(jnp.concatenate([A_rq, A_rk], axis=1), UV)
            z_s[j] = Z * e_last[:, sl] + _dot_tn(UV, jnp.concatenate([Qh2, Kh2], axis=0))
            y = Y2[0:RWKV_CHUNK]
            for e in range(1, RWKV_GROUP):
                y = y + Y2[e * RWKV_CHUNK:(e + 1) * RWKV_CHUNK]
            y_s[rows, sl] = y
        return carry

    lax.fori_loop(0, tb // RWKV_CHUNK, rwkv_chunk, 0)

    y = y_s[...]
    inv_n = 1.0 / RWKV_HEAD
    mu_y = _dot_exact_rhs(y, hm) * inv_n
    yc = y - mu_y
    var_y = _dot_exact_rhs(yc * yc, hm) * inv_n
    y = yc * lax.rsqrt(var_y + RWKV_GN_EPS) * gng_ref[...] + gnb_ref[...] + bonus
    o_c = y * gate

    m = _dot(o_c.astype(BF16), wout_ref[0:512, :]) + _dot(o_d.astype(BF16), wout_ref[512:1024, :])
    o_ref[...] = x + m


def _odd_layer(x, norm_g, w_in, w_out, mu, w0, w2, a0, a2, g2, k_k, k_a, r_k, gn_g, gn_b,
               conv_w, conv_b, cln_g, cln_b, *, tb):
    B, T, D = x.shape
    pad_lr = 256 - (RWKV_DECAY_RANK + RWKV_AAA_RANK + RWKV_GATE_RANK)
    w_in_r = jnp.concatenate(
        [w_in[:, :RWKV_IN], jnp.zeros((D, pad_lr), w_in.dtype), w_in[:, RWKV_IN:]], axis=1).astype(BF16)
    mu_r = jnp.concatenate([mu.astype(F32), jnp.zeros((pad_lr,), F32)]).reshape(1, RWKV_COLS)
    wlr = jnp.zeros((256, 3 * RWKV_WIDTH), F32)
    wlr = wlr.at[0:32, 0:512].set(w2.astype(F32))
    wlr = wlr.at[32:64, 512:1024].set(a2.astype(F32))
    wlr = wlr.at[64:160, 1024:1536].set(g2.astype(F32))
    row = lambda a: a.astype(F32).reshape(1, -1)
    const = lambda shape: pl.BlockSpec(shape, lambda b, t: (0,) * len(shape))
    gw = RWKV_GROUP * RWKV_HEAD
    kern = functools.partial(_odd_kernel, tb=tb)
    return pl.pallas_call(
        kern,
        grid=(B, T // tb),
        in_specs=[
            pl.BlockSpec((None, tb, D), lambda b, t: (b, t, 0)),
            const((1, D)), const((D, ODD_COLS)), const((1, RWKV_COLS)), const((256, 3 * RWKV_WIDTH)),
            const((1, 512)), const((1, 512)), const((1, 512)), const((1, 512)), const((1, 512)),
            const((1, 512)), const((1, 512)),
            const((CONV_KERNEL, CONV_WIDTH)), const((1, 512)), const((1, 512)), const((1, 512)),
            const((D, D)),
        ],
        out_specs=pl.BlockSpec((None, tb, D), lambda b, t: (b, t, 0)),
        out_shape=jax.ShapeDtypeStruct((B, T, D), F32),
        scratch_shapes=[
            pltpu.VMEM((tb, ODD_COLS), F32),
            pltpu.VMEM((8, RWKV_COLS), F32),
            pltpu.VMEM((tb + CONV_HALO, CONV_WIDTH), F32),
            pltpu.VMEM((tb, 512), F32), pltpu.VMEM((tb, 512), F32), pltpu.VMEM((tb, 512), F32),
            pltpu.VMEM((tb, 512), F32), pltpu.VMEM((tb, 512), F32), pltpu.VMEM((tb, 512), F32),
            pltpu.VMEM((tb, 512), F32),
            pltpu.VMEM((RWKV_WIDTH // gw, gw, gw), F32),
        ],
        compiler_params=pltpu.CompilerParams(
            dimension_semantics=("arbitrary", "arbitrary"), vmem_limit_bytes=VMEM_LIMIT),
        name="odd_mixer",
    )(x, row(norm_g), w_in_r, mu_r, wlr.astype(BF16), row(w0), row(a0), row(k_k), row(k_a), row(r_k),
      row(gn_g), row(gn_b), conv_w.astype(F32), row(conv_b), row(cln_g), row(cln_b), w_out.astype(BF16))


def _mlp_kernel(x_ref, g_ref, wup_ref, wdn_ref, gf_ref, o_ref, *, ff_chunk, final_norm):
    x = x_ref[...]
    xb = _rms(x, g_ref[...], RMS_EPS).astype(BF16)
    acc = x
    d_ff = wup_ref.shape[1]
    for f0 in range(0, d_ff, ff_chunk):
        hcol = _dot(xb, wup_ref[:, f0:f0 + ff_chunk])
        hcol = jnp.square(jnp.maximum(hcol, 0.0))
        acc = acc + _dot(hcol.astype(BF16), wdn_ref[f0:f0 + ff_chunk, :])
    if final_norm:
        acc = _rms(acc, gf_ref[...], RMS_EPS)
    o_ref[...] = acc


def _mlp_layer(x2, norm_g, w_up, w_down, final_g, *, tm, final_norm):
    M, D = x2.shape
    d_ff = w_up.shape[1]
    kern = functools.partial(_mlp_kernel, ff_chunk=1024, final_norm=final_norm)
    return pl.pallas_call(
        kern,
        grid=(M // tm,),
        in_specs=[
            pl.BlockSpec((tm, D), lambda i: (i, 0)),
            pl.BlockSpec((1, D), lambda i: (0, 0)),
            pl.BlockSpec((D, d_ff), lambda i: (0, 0)),
            pl.BlockSpec((d_ff, D), lambda i: (0, 0)),
            pl.BlockSpec((1, D), lambda i: (0, 0)),
        ],
        out_specs=pl.BlockSpec((tm, D), lambda i: (i, 0)),
        out_shape=jax.ShapeDtypeStruct((M, D), F32),
        compiler_params=pltpu.CompilerParams(
            dimension_semantics=("arbitrary",), vmem_limit_bytes=VMEM_LIMIT),
        name="mlp_final" if final_norm else "mlp",
    )(x2, norm_g.astype(F32).reshape(1, D), w_up.astype(BF16), w_down.astype(BF16),
      final_g.astype(F32).reshape(1, D))


def _pick_block(n, target):
    b = min(n, target)
    while n % b:
        b //= 2
    return b


def kernel(x, norm_mix, norm_ffn, w_up, w_down, norm_final, even_w_in, even_w_out, gla_w_alpha2, gla_b_alpha, gla_norm, sgu_ln_g, sgu_ln_b, sgu_w, sgu_b, odd_w_in, odd_w_out, rwkv_mu, rwkv_w0, rwkv_w2, rwkv_a0, rwkv_a2, rwkv_g2, rwkv_k_k, rwkv_k_a, rwkv_r_k, rwkv_gn_g, rwkv_gn_b, conv_w, conv_b, conv_ln_g, conv_ln_b):
    B, T, D = x.shape
    depth = norm_mix.shape[0]
    tb = _pick_block(T, 512)
    tm = _pick_block(B * T, 512)
    assert tb % SGU_CHUNK == 0 and D == 1024
    for layer in range(depth):
        j = layer // 2
        if layer % 2 == 0:
            x = _even_layer(x, norm_mix[layer], even_w_in[j], even_w_out[j], gla_w_alpha2[j],
                            gla_b_alpha[j], gla_norm[j], sgu_ln_g[j], sgu_ln_b[j], sgu_w[j], sgu_b[j],
                            tb=tb)
        else:
            x = _odd_layer(x, norm_mix[layer], odd_w_in[j], odd_w_out[j], rwkv_mu[j], rwkv_w0[j],
                           rwkv_w2[j], rwkv_a0[j], rwkv_a2[j], rwkv_g2[j], rwkv_k_k[j], rwkv_k_a[j],
                           rwkv_r_k[j].reshape(-1), rwkv_gn_g[j], rwkv_gn_b[j], conv_w[j], conv_b[j],
                           conv_ln_g[j], conv_ln_b[j], tb=tb)
        last = layer == depth - 1
        x = _mlp_layer(x.reshape(B * T, D), norm_ffn[layer], w_up[layer], w_down[layer], norm_final,
                       tm=tm, final_norm=last).reshape(B, T, D)
    return x
```

```python
import functools

import jax
import jax.numpy as jnp
from jax import lax
from jax.experimental import pallas as pl
from jax.experimental.pallas import tpu as pltpu

F32 = jnp.float32
BF16 = jnp.bfloat16

RMS_EPS = 1e-6
LN_EPS = 1e-5

GLA_HEADS = 4
GLA_DK = 64
GLA_DV = 128
GLA_KEY = GLA_HEADS * GLA_DK
GLA_WIDTH = GLA_HEADS * GLA_DV
GLA_GATE_RANK = 16
GLA_TAU = 16.0
GLA_CHUNK = 64
SGU_WIDTH = 512
SGU_GROUPS = 4
SGU_CHUNK = 128
EVEN_COLS = 2688

RWKV_WIDTH = 512
RWKV_HEAD = 64
RWKV_HEADS = 8
RWKV_DECAY_RANK = 32
RWKV_AAA_RANK = 32
RWKV_GATE_RANK = 96
RWKV_GN_EPS = 64e-5
RWKV_IN = 1696
RWKV_COLS = 1792
RWKV_CHUNK = 64
RWKV_GROUP = 2
CONV_WIDTH = 512
CONV_KERNEL = 31
CONV_HALO = 32
ODD_COLS = RWKV_COLS + 2 * CONV_WIDTH

LANE = 128
VMEM_LIMIT = 56 * 1024 * 1024


def _iota(shape, dim):
    return lax.broadcasted_iota(jnp.int32, shape, dim)


def _dot(a, b):
    return jnp.dot(a, b, preferred_element_type=F32)


def _dot_nt(a, b):
    return lax.dot_general(a, b, (((1,), (1,)), ((), ())), preferred_element_type=F32)


def _dot_tn(a, b):
    return lax.dot_general(a, b, (((0,), (0,)), ((), ())), preferred_element_type=F32)


def _split2(x):
    hi = x.astype(BF16)
    lo = (x - hi.astype(F32)).astype(BF16)
    return hi, lo


def _dot_exact_lhs(m_bf16, x):
    hi, lo = _split2(x)
    return _dot(m_bf16, hi) + _dot(m_bf16, lo)


def _dot_exact_rhs(x, m_bf16):
    hi, lo = _split2(x)
    return _dot(hi, m_bf16) + _dot(lo, m_bf16)


def _rms(x, g, eps):
    return x * lax.rsqrt(jnp.mean(x * x, axis=-1, keepdims=True) + eps) * g


def _layernorm(x, g, b, eps):
    mu = jnp.mean(x, axis=-1, keepdims=True)
    xc = x - mu
    var = jnp.mean(xc * xc, axis=-1, keepdims=True)
    return xc * lax.rsqrt(var + eps) * g + b


def _sigmoid(x):
    return 1.0 / (1.0 + jnp.exp(-x))


def _silu(x):
    return x * _sigmoid(x)


def _softplus(x):
    return jnp.maximum(x, 0.0) + jnp.log(1.0 + jnp.exp(-jnp.abs(x)))


def _gelu_tanh(x):
    c = 0.7978845608028654
    return 0.5 * x * (1.0 + jnp.tanh(c * (x + 0.044715 * (x * x * x))))


def _tril_ones(n, dtype):
    return jnp.where(_iota((n, n), 0) >= _iota((n, n), 1), 1.0, 0.0).astype(dtype)


def _even_kernel(x_ref, gn_ref, win_ref, wal_ref, bal_ref, glan_ref, lng_ref, lnb_ref,
                 sguw_ref, sgub_ref, wout_ref, o_ref,
                 p_ref, la_ref, oa_ref, ob_ref, st_ref, *, tb):
    t_idx = pl.program_id(1)

    @pl.when(t_idx == 0)
    def _():
        st_ref[...] = jnp.zeros_like(st_ref)

    x = x_ref[...]
    h = _rms(x, gn_ref[...], RMS_EPS).astype(BF16)
    p_ref[...] = _dot(h, win_ref[...])

    alr = p_ref[:, 2560:2688]
    a_hi, a_lo = _split2(alr)
    w_hi, w_lo = _split2(wal_ref[...])
    z = _dot(a_hi, w_hi) + _dot(a_hi, w_lo) + _dot(a_lo, w_hi) + bal_ref[...]
    la_ref[...] = -_softplus(-z) * (1.0 / GLA_TAU)

    tril = _tril_ones(GLA_CHUNK, BF16)
    rk = _iota((GLA_KEY, GLA_KEY), 0) >> 6
    ck = _iota((GLA_KEY, GLA_KEY), 1) >> 6
    k4_mask = rk == ck
    rv = _iota((GLA_KEY, GLA_WIDTH), 0) >> 6
    cv = _iota((GLA_KEY, GLA_WIDTH), 1) >> 7
    vbd_mask = rv == cv
    causal = _iota((GLA_CHUNK, GLA_KEY), 0) >= (_iota((GLA_CHUNK, GLA_KEY), 1) & 63)
    rs = _iota((GLA_WIDTH, GLA_KEY), 0) >> 7
    cs = _iota((GLA_WIDTH, GLA_KEY), 1) >> 6
    st_mask = rs == cs

    def gla_chunk(c, carry):
        rows = pl.ds(pl.multiple_of(c * GLA_CHUNK, GLA_CHUNK), GLA_CHUNK)
        q = p_ref[rows, 0:256] * (GLA_DK ** -0.5)
        k = p_ref[rows, 256:512]
        v = p_ref[rows, 512:1024]
        b = _dot_exact_lhs(tril, la_ref[rows, :])
        b_last = b[GLA_CHUNK - 1:GLA_CHUNK, :]
        q_dec = q * jnp.exp(b)
        k_dec = k * jnp.exp(-b)
        k_state = k * jnp.exp(b_last - b)
        st = st_ref[...]
        k4 = jnp.where(k4_mask, jnp.concatenate([k_dec] * GLA_HEADS, axis=0), 0.0)
        att = jnp.where(causal, _dot_nt(q_dec, k4), 0.0)
        v_bd = jnp.where(vbd_mask, jnp.concatenate([v] * GLA_HEADS, axis=0), 0.0)
        o = _dot(att, v_bd) + _dot_nt(q_dec, st)
        st_ref[...] = st * jnp.exp(b_last) + jnp.where(st_mask, _dot_tn(v, k_state), 0.0)
        for hh in range(GLA_HEADS):
            oh = o[:, hh * GLA_DV:(hh + 1) * GLA_DV]
            oh = oh * lax.rsqrt(jnp.mean(oh * oh, axis=-1, keepdims=True) + RMS_EPS)
            oa_ref[rows, hh * GLA_DV:(hh + 1) * GLA_DV] = oh
        return carry

    lax.fori_loop(0, tb // GLA_CHUNK, gla_chunk, 0)

    g = p_ref[:, 1024:1536]
    o_a = oa_ref[...] * glan_ref[...] * _silu(g)

    sv = _layernorm(_gelu_tanh(p_ref[:, 2048:2560]), lng_ref[...], lnb_ref[...], LN_EPS)
    sv = sv.astype(BF16)
    tri128 = _iota((SGU_CHUNK, SGU_CHUNK), 0) >= _iota((SGU_CHUNK, SGU_CHUNK), 1)
    for gg in range(SGU_GROUPS):
        wg = jnp.where(tri128, sguw_ref[gg], 0.0).astype(BF16)
        for c in range(tb // SGU_CHUNK):
            r0 = c * SGU_CHUNK
            s = _dot(wg, sv[r0:r0 + SGU_CHUNK, gg * LANE:(gg + 1) * LANE])
            s = s + sgub_ref[:, gg * LANE:(gg + 1) * LANE]
            u = _gelu_tanh(p_ref[r0:r0 + SGU_CHUNK, 1536 + gg * LANE:1536 + (gg + 1) * LANE])
            ob_ref[r0:r0 + SGU_CHUNK, gg * LANE:(gg + 1) * LANE] = u * s

    m = _dot(o_a.astype(BF16), wout_ref[0:512, :]) + _dot(ob_ref[...].astype(BF16), wout_ref[512:1024, :])
    o_ref[...] = x + m


def _even_layer(x, norm_g, w_in, w_out, w_alpha2, b_alpha, gla_norm, ln_g, ln_b, sgu_w, sgu_b, *, tb):
    B, T, D = x.shape
    w_in_r = jnp.concatenate(
        [w_in[:, :1536], w_in[:, 1552:2576], w_in[:, 1536:1552],
         jnp.zeros((D, EVEN_COLS - 2576), w_in.dtype)], axis=1).astype(BF16)
    wal = jnp.zeros((LANE, GLA_KEY), F32).at[:GLA_GATE_RANK].set(w_alpha2.astype(F32))
    sgub = jnp.repeat(sgu_b.astype(F32).T, LANE, axis=1)
    row = lambda a: a.astype(F32).reshape(1, -1)
    const = lambda shape: pl.BlockSpec(shape, lambda b, t: (0,) * len(shape))
    kern = functools.partial(_even_kernel, tb=tb)
    return pl.pallas_call(
        kern,
        grid=(B, T // tb),
        in_specs=[
            pl.BlockSpec((None, tb, D), lambda b, t: (b, t, 0)),
            const((1, D)), const((D, EVEN_COLS)), const((LANE, GLA_KEY)), const((1, GLA_KEY)),
            const((1, GLA_WIDTH)), const((1, SGU_WIDTH)), const((1, SGU_WIDTH)),
            const((SGU_GROUPS, SGU_CHUNK, SGU_CHUNK)), const((SGU_CHUNK, SGU_WIDTH)),
            const((D, D)),
        ],
        out_specs=pl.BlockSpec((None, tb, D), lambda b, t: (b, t, 0)),
        out_shape=jax.ShapeDtypeStruct((B, T, D), F32),
        scratch_shapes=[
            pltpu.VMEM((tb, EVEN_COLS), F32),
            pltpu.VMEM((tb, GLA_KEY), F32),
            pltpu.VMEM((tb, GLA_WIDTH), F32),
            pltpu.VMEM((tb, SGU_WIDTH), F32),
            pltpu.VMEM((GLA_WIDTH, GLA_KEY), F32),
        ],
        compiler_params=pltpu.CompilerParams(
            dimension_semantics=("arbitrary", "arbitrary"), vmem_limit_bytes=VMEM_LIMIT),
        name="even_mixer",
    )(x, row(norm_g), w_in_r, wal, row(b_alpha), row(gla_norm), row(ln_g), row(ln_b),
      sgu_w.astype(F32), sgub, w_out.astype(BF16))


def _odd_kernel(x_ref, gn_ref, win_ref, mu_ref, wlr_ref, w0_ref, a0_ref, kk_ref, ka_ref, rk_ref,
                gng_ref, gnb_ref, cw_ref, cb_ref, clg_ref, clb_ref, wout_ref, o_ref,
                p_ref, prev_ref, zc_ref, r_s, kk_s, k2_s, qa_s, v_s, lw_s, y_s, z_s, gl_s, ark_s, uw_s,
                *, tb):
    t_idx = pl.program_id(1)
    gw = RWKV_GROUP * RWKV_HEAD
    n_groups = RWKV_WIDTH // gw

    @pl.when(t_idx == 0)
    def _():
        prev_ref[...] = jnp.zeros_like(prev_ref)
        zc_ref[0:CONV_HALO, :] = jnp.zeros((CONV_HALO, CONV_WIDTH), F32)
        z_s[...] = jnp.zeros_like(z_s)

    @pl.when(t_idx != 0)
    def _():
        zc_ref[0:CONV_HALO, :] = zc_ref[tb:tb + CONV_HALO, :]

    x = x_ref[...]
    h = _rms(x, gn_ref[...], RMS_EPS).astype(BF16)
    p_ref[...] = _dot(h, win_ref[...])

    zc_ref[CONV_HALO:CONV_HALO + tb, :] = p_ref[:, RWKV_COLS:RWKV_COLS + CONV_WIDTH] * _sigmoid(
        p_ref[:, RWKV_COLS + CONV_WIDTH:ODD_COLS])
    acc = jnp.zeros((tb, CONV_WIDTH), F32) + cb_ref[...]
    for j in range(CONV_KERNEL):
        off = CONV_HALO - (CONV_KERNEL - 1) + j
        acc = acc + zc_ref[off:off + tb, :] * cw_ref[j:j + 1, :]
    o_d = _silu(_layernorm(acc, clg_ref[...], clb_ref[...], LN_EPS))

    pr = p_ref[:, 0:RWKV_COLS]
    shifted = pltpu.roll(pr, 1, 0)
    shifted = jnp.where(_iota((tb, RWKV_COLS), 0) == 0, prev_ref[7:8, :], shifted)
    prev_ref[...] = p_ref[tb - 8:tb, 0:RWKV_COLS]
    hs = pr + (shifted - pr) * mu_ref[...]
    r = hs[:, 0:512]
    k = hs[:, 512:1024]
    v = hs[:, 1024:1536]
    lr = hs[:, 1536:1792]
    lane = _iota((tb, 256), 1)
    f = jnp.where(lane < RWKV_DECAY_RANK, jnp.tanh(lr),
                  jnp.where(lane < RWKV_DECAY_RANK + RWKV_AAA_RANK, lr, _sigmoid(lr)))
    lo3 = _dot(f.astype(BF16), wlr_ref[...])
    wlog = -_softplus(-(w0_ref[...] + lo3[:, 0:512])) - 0.5
    lw_s[...] = -jnp.exp(wlog)
    a = _sigmoid(a0_ref[...] + lo3[:, 512:1024])
    gate = lo3[:, 1024:1536]

    hm = jnp.where((_iota((RWKV_WIDTH, RWKV_WIDTH), 0) >> 6) == (_iota((RWKV_WIDTH, RWKV_WIDTH), 1) >> 6),
                   1.0, 0.0).astype(BF16)
    kk = k * kk_ref[...]
    kk = kk / jnp.maximum(jnp.sqrt(_dot_exact_rhs(kk * kk, hm)), 1e-12)
    k2 = k * (1.0 + (a - 1.0) * ka_ref[...])
    bonus = _dot_exact_rhs(r * k2 * rk_ref[...], hm) * v
    r_s[...] = r
    kk_s[...] = kk
    k2_s[...] = k2
    qa_s[...] = kk * a
    v_s[...] = v

    tril = _tril_ones(RWKV_CHUNK, BF16)
    ri = _iota((gw, gw), 0)
    ci = _iota((gw, gw), 1)
    same = (ri >> 6) == (ci >> 6)
    strict = same & (ri > ci)
    ri2 = _iota((gw, 2 * gw), 0)
    ci2 = _iota((gw, 2 * gw), 1) & (gw - 1)
    incl2 = ((ri2 >> 6) == (ci2 >> 6)) & (ri2 >= ci2)
    eye = jnp.where(ri == ci, 1.0, 0.0)

    def bd(xs):
        return jnp.where(same, jnp.concatenate([xs] * RWKV_GROUP, axis=0), 0.0)

    def chunk_local(c, carry):
        rows = pl.ds(pl.multiple_of(c * RWKV_CHUNK, RWKV_CHUNK), RWKV_CHUNK)
        lw = lw_s[rows, :]
        cum = _dot_exact_lhs(tril, lw)
        last = cum[RWKV_CHUNK - 1:RWKV_CHUNK, :]
        e_neg = jnp.exp(-cum)
        e_hat = jnp.exp(last - cum)
        gl_s[c] = jnp.broadcast_to(jnp.exp(last), (8, RWKV_WIDTH))
        kkc = kk_s[rows, :]
        qac = qa_s[rows, :]
        k2c = k2_s[rows, :]
        r_t = r_s[rows, :] * jnp.exp(cum)
        p_t = -kkc * jnp.exp(cum - lw)
        q_t = qac * e_neg
        k_t = k2c * e_neg
        vc = v_s[rows, :]
        r_s[rows, :] = r_t
        qa_s[rows, :] = qac * e_hat
        k2_s[rows, :] = k2c * e_hat
        groups = range(n_groups)
        sls = [slice(j * gw, (j + 1) * gw) for j in groups]
        P2 = [bd(p_t[:, s]).astype(BF16) for s in sls]
        R2 = [bd(r_t[:, s]).astype(BF16) for s in sls]
        Q2 = [bd(q_t[:, s]).astype(BF16) for s in sls]
        K2 = [bd(k_t[:, s]).astype(BF16) for s in sls]
        V2 = [bd(vc[:, s]).astype(BF16) for s in sls]
        sc = [_dot_nt(jnp.concatenate([P2[j], R2[j]], axis=0), jnp.concatenate([Q2[j], K2[j]], axis=0))
              for j in groups]
        A_pq = [jnp.where(strict, sc[j][0:gw, 0:gw], 0.0) for j in groups]
        A_pk = [jnp.where(strict, sc[j][0:gw, gw:2 * gw], 0.0).astype(BF16) for j in groups]
        for j in groups:
            ark_s[c * n_groups + j] = jnp.where(incl2, sc[j][gw:2 * gw, :], 0.0).astype(BF16)
        Xp = [_dot(A_pq[j], A_pq[j]) for j in groups]
        av = [_dot(A_pk[j], V2[j]).astype(BF16) for j in groups]
        Tm = [eye + A_pq[j] for j in groups]
        for _ in range(4):
            XT = [_dot(Xp[j], jnp.concatenate([Xp[j], Tm[j]], axis=1)) for j in groups]
            Xp = [XT[j][:, 0:gw] for j in groups]
            Tm = [Tm[j] + XT[j][:, gw:2 * gw] for j in groups]
        XT = [_dot(Xp[j], Tm[j]) for j in groups]
        Tm = [(Tm[j] + XT[j]).astype(BF16) for j in groups]
        uw = [_dot(Tm[j], jnp.concatenate([av[j], P2[j]], axis=1)) for j in groups]
        for j in groups:
            uw_s[c * n_groups + j] = uw[j].astype(BF16)
        return carry

    def chunk_state(c, carry):
        rows = pl.ds(pl.multiple_of(c * RWKV_CHUNK, RWKV_CHUNK), RWKV_CHUNK)
        e_last = gl_s[c][0:1, :]
        groups = range(n_groups)
        sls = [slice(j * gw, (j + 1) * gw) for j in groups]
        R2 = [bd(r_s[rows, s]).astype(BF16) for s in sls]
        QK = [jnp.concatenate([bd(qa_s[rows, s]), bd(k2_s[rows, s])], axis=0).astype(BF16) for s in sls]
        V2 = [bd(v_s[rows, s]).astype(BF16) for s in sls]
        uw = [uw_s[c * n_groups + j] for j in groups]
        Z = [z_s[j] for j in groups]
        hw = [_dot_nt(jnp.concatenate([uw[j][:, gw:2 * gw], R2[j]], axis=0), Z[j].astype(BF16))
              for j in groups]
        UV = [jnp.concatenate([(uw[j][:, 0:gw].astype(F32) + hw[j][0:gw]).astype(BF16), V2[j]], axis=0)
              for j in groups]
        zn = [_dot_tn(UV[j], QK[j]) for j in groups]
        Y2 = [hw[j][gw:2 * gw] + _dot(ark_s[c * n_groups + j], UV[j]) for j in groups]
        for j in groups:
            z_s[j] = Z[j] * e_last[:, sls[j]] + zn[j]
            y = Y2[j][0:RWKV_CHUNK]
            for e in range(1, RWKV_GROUP):
                y = y + Y2[j][e * RWKV_CHUNK:(e + 1) * RWKV_CHUNK]
            y_s[rows, sls[j]] = y
        return carry

    lax.fori_loop(0, tb // RWKV_CHUNK, chunk_local, 0)
    lax.fori_loop(0, tb // RWKV_CHUNK, chunk_state, 0)

    y = y_s[...]
    inv_n = 1.0 / RWKV_HEAD
    mu_y = _dot_exact_rhs(y, hm) * inv_n
    yc = y - mu_y
    var_y = _dot_exact_rhs(yc * yc, hm) * inv_n
    y = yc * lax.rsqrt(var_y + RWKV_GN_EPS) * gng_ref[...] + gnb_ref[...] + bonus
    o_c = y * gate

    m = _dot(o_c.astype(BF16), wout_ref[0:512, :]) + _dot(o_d.astype(BF16), wout_ref[512:1024, :])
    o_ref[...] = x + m


def _odd_layer(x, norm_g, w_in, w_out, mu, w0, w2, a0, a2, g2, k_k, k_a, r_k, gn_g, gn_b,
               conv_w, conv_b, cln_g, cln_b, *, tb):
    B, T, D = x.shape
    pad_lr = 256 - (RWKV_DECAY_RANK + RWKV_AAA_RANK + RWKV_GATE_RANK)
    w_in_r = jnp.concatenate(
        [w_in[:, :RWKV_IN], jnp.zeros((D, pad_lr), w_in.dtype), w_in[:, RWKV_IN:]], axis=1).astype(BF16)
    mu_r = jnp.concatenate([mu.astype(F32), jnp.zeros((pad_lr,), F32)]).reshape(1, RWKV_COLS)
    wlr = jnp.zeros((256, 3 * RWKV_WIDTH), F32)
    wlr = wlr.at[0:32, 0:512].set(w2.astype(F32))
    wlr = wlr.at[32:64, 512:1024].set(a2.astype(F32))
    wlr = wlr.at[64:160, 1024:1536].set(g2.astype(F32))
    row = lambda a: a.astype(F32).reshape(1, -1)
    const = lambda shape: pl.BlockSpec(shape, lambda b, t: (0,) * len(shape))
    gw = RWKV_GROUP * RWKV_HEAD
    kern = functools.partial(_odd_kernel, tb=tb)
    return pl.pallas_call(
        kern,
        grid=(B, T // tb),
        in_specs=[
            pl.BlockSpec((None, tb, D), lambda b, t: (b, t, 0)),
            const((1, D)), const((D, ODD_COLS)), const((1, RWKV_COLS)), const((256, 3 * RWKV_WIDTH)),
            const((1, 512)), const((1, 512)), const((1, 512)), const((1, 512)), const((1, 512)),
            const((1, 512)), const((1, 512)),
            const((CONV_KERNEL, CONV_WIDTH)), const((1, 512)), const((1, 512)), const((1, 512)),
            const((D, D)),
        ],
        out_specs=pl.BlockSpec((None, tb, D), lambda b, t: (b, t, 0)),
        out_shape=jax.ShapeDtypeStruct((B, T, D), F32),
        scratch_shapes=[
            pltpu.VMEM((tb, ODD_COLS), F32),
            pltpu.VMEM((8, RWKV_COLS), F32),
            pltpu.VMEM((tb + CONV_HALO, CONV_WIDTH), F32),
            pltpu.VMEM((tb, 512), F32), pltpu.VMEM((tb, 512), F32), pltpu.VMEM((tb, 512), F32),
            pltpu.VMEM((tb, 512), F32), pltpu.VMEM((tb, 512), F32), pltpu.VMEM((tb, 512), F32),
            pltpu.VMEM((tb, 512), F32),
            pltpu.VMEM((RWKV_WIDTH // gw, gw, gw), F32),
            pltpu.VMEM((tb // RWKV_CHUNK, 8, RWKV_WIDTH), F32),
            pltpu.VMEM((tb // RWKV_CHUNK * (RWKV_WIDTH // gw), gw, 2 * gw), BF16),
            pltpu.VMEM((tb // RWKV_CHUNK * (RWKV_WIDTH // gw), gw, 2 * gw), BF16),
        ],
        compiler_params=pltpu.CompilerParams(
            dimension_semantics=("arbitrary", "arbitrary"), vmem_limit_bytes=VMEM_LIMIT),
        name="odd_mixer",
    )(x, row(norm_g), w_in_r, mu_r, wlr.astype(BF16), row(w0), row(a0), row(k_k), row(k_a), row(r_k),
      row(gn_g), row(gn_b), conv_w.astype(F32), row(conv_b), row(cln_g), row(cln_b), w_out.astype(BF16))


def _mlp_kernel(x_ref, g_ref, wup_ref, wdn_ref, gf_ref, o_ref, *, ff_chunk, final_norm):
    x = x_ref[...]
    xb = _rms(x, g_ref[...], RMS_EPS).astype(BF16)
    acc = x
    d_ff = wup_ref.shape[1]
    for f0 in range(0, d_ff, ff_chunk):
        hcol = _dot(xb, wup_ref[:, f0:f0 + ff_chunk])
        hcol = jnp.square(jnp.maximum(hcol, 0.0))
        acc = acc + _dot(hcol.astype(BF16), wdn_ref[f0:f0 + ff_chunk, :])
    if final_norm:
        acc = _rms(acc, gf_ref[...], RMS_EPS)
    o_ref[...] = acc


def _mlp_layer(x2, norm_g, w_up, w_down, final_g, *, tm, final_norm):
    M, D = x2.shape
    d_ff = w_up.shape[1]
    kern = functools.partial(_mlp_kernel, ff_chunk=1024, final_norm=final_norm)
    return pl.pallas_call(
        kern,
        grid=(M // tm,),
        in_specs=[
            pl.BlockSpec((tm, D), lambda i: (i, 0)),
            pl.BlockSpec((1, D), lambda i: (0, 0)),
            pl.BlockSpec((D, d_ff), lambda i: (0, 0)),
            pl.BlockSpec((d_ff, D), lambda i: (0, 0)),
            pl.BlockSpec((1, D), lambda i: (0, 0)),
        ],
        out_specs=pl.BlockSpec((tm, D), lambda i: (i, 0)),
        out_shape=jax.ShapeDtypeStruct((M, D), F32),
        compiler_params=pltpu.CompilerParams(
            dimension_semantics=("arbitrary",), vmem_limit_bytes=VMEM_LIMIT),
        name="mlp_final" if final_norm else "mlp",
    )(x2, norm_g.astype(F32).reshape(1, D), w_up.astype(BF16), w_down.astype(BF16),
      final_g.astype(F32).reshape(1, D))


def _pick_block(n, target):
    b = min(n, target)
    while n % b:
        b //= 2
    return b


def kernel(x, norm_mix, norm_ffn, w_up, w_down, norm_final, even_w_in, even_w_out, gla_w_alpha2, gla_b_alpha, gla_norm, sgu_ln_g, sgu_ln_b, sgu_w, sgu_b, odd_w_in, odd_w_out, rwkv_mu, rwkv_w0, rwkv_w2, rwkv_a0, rwkv_a2, rwkv_g2, rwkv_k_k, rwkv_k_a, rwkv_r_k, rwkv_gn_g, rwkv_gn_b, conv_w, conv_b, conv_ln_g, conv_ln_b):
    B, T, D = x.shape
    depth = norm_mix.shape[0]
    tb = _pick_block(T, 512)
    tm = _pick_block(B * T, 512)
    assert tb % SGU_CHUNK == 0 and D == 1024
    for layer in range(depth):
        j = layer // 2
        if layer % 2 == 0:
            x = _even_layer(x, norm_mix[layer], even_w_in[j], even_w_out[j], gla_w_alpha2[j],
                            gla_b_alpha[j], gla_norm[j], sgu_ln_g[j], sgu_ln_b[j], sgu_w[j], sgu_b[j],
                            tb=tb)
        else:
            x = _odd_layer(x, norm_mix[layer], odd_w_in[j], odd_w_out[j], rwkv_mu[j], rwkv_w0[j],
                           rwkv_w2[j], rwkv_a0[j], rwkv_a2[j], rwkv_g2[j], rwkv_k_k[j], rwkv_k_a[j],
                           rwkv_r_k[j].reshape(-1), rwkv_gn_g[j], rwkv_gn_b[j], conv_w[j], conv_b[j],
                           conv_ln_g[j], conv_ln_b[j], tb=tb)
        last = layer == depth - 1
        x = _mlp_layer(x.reshape(B * T, D), norm_ffn[layer], w_up[layer], w_down[layer], norm_final,
                       tm=tm, final_norm=last).reshape(B, T, D)
    return x
```

```python
import functools

import jax
import jax.numpy as jnp
from jax import lax
from jax.experimental import pallas as pl
from jax.experimental.pallas import tpu as pltpu

F32 = jnp.float32
BF16 = jnp.bfloat16

RMS_EPS = 1e-6
LN_EPS = 1e-5

GLA_HEADS = 4
GLA_DK = 64
GLA_DV = 128
GLA_KEY = GLA_HEADS * GLA_DK
GLA_WIDTH = GLA_HEADS * GLA_DV
GLA_GATE_RANK = 16
GLA_TAU = 16.0
GLA_CHUNK = 64
SGU_WIDTH = 512
SGU_GROUPS = 4
SGU_CHUNK = 128
EVEN_COLS = 2688

RWKV_WIDTH = 512
RWKV_HEAD = 64
RWKV_HEADS = 8
RWKV_DECAY_RANK = 32
RWKV_AAA_RANK = 32
RWKV_GATE_RANK = 96
RWKV_GN_EPS = 64e-5
RWKV_IN = 1696
RWKV_COLS = 1792
RWKV_CHUNK = 64
RWKV_GROUP = 2
LOCAL_CHUNKS = 2
CONV_WIDTH = 512
CONV_KERNEL = 31
CONV_HALO = 32
ODD_COLS = RWKV_COLS + 2 * CONV_WIDTH

LANE = 128
VMEM_LIMIT = 56 * 1024 * 1024


def _iota(shape, dim):
    return lax.broadcasted_iota(jnp.int32, shape, dim)


def _dot(a, b):
    return jnp.dot(a, b, preferred_element_type=F32)


def _dot_nt(a, b):
    return lax.dot_general(a, b, (((1,), (1,)), ((), ())), preferred_element_type=F32)


def _dot_tn(a, b):
    return lax.dot_general(a, b, (((0,), (0,)), ((), ())), preferred_element_type=F32)


def _split2(x):
    hi = x.astype(BF16)
    lo = (x - hi.astype(F32)).astype(BF16)
    return hi, lo


def _dot_exact_lhs(m_bf16, x):
    hi, lo = _split2(x)
    return _dot(m_bf16, hi) + _dot(m_bf16, lo)


def _dot_exact_rhs(x, m_bf16):
    hi, lo = _split2(x)
    return _dot(hi, m_bf16) + _dot(lo, m_bf16)


def _rms(x, g, eps):
    return x * lax.rsqrt(jnp.mean(x * x, axis=-1, keepdims=True) + eps) * g


def _layernorm(x, g, b, eps):
    mu = jnp.mean(x, axis=-1, keepdims=True)
    xc = x - mu
    var = jnp.mean(xc * xc, axis=-1, keepdims=True)
    return xc * lax.rsqrt(var + eps) * g + b


def _sigmoid(x):
    return 1.0 / (1.0 + jnp.exp(-x))


def _silu(x):
    return x * _sigmoid(x)


def _softplus(x):
    return jnp.maximum(x, 0.0) + jnp.log(1.0 + jnp.exp(-jnp.abs(x)))


def _gelu_tanh(x):
    c = 0.7978845608028654
    return 0.5 * x * (1.0 + jnp.tanh(c * (x + 0.044715 * (x * x * x))))


def _tril_ones(n, dtype):
    return jnp.where(_iota((n, n), 0) >= _iota((n, n), 1), 1.0, 0.0).astype(dtype)


def _even_kernel(x_ref, gn_ref, win_ref, wal_ref, bal_ref, glan_ref, lng_ref, lnb_ref,
                 sguw_ref, sgub_ref, wout_ref, o_ref,
                 p_ref, la_ref, oa_ref, ob_ref, st_ref, *, tb):
    t_idx = pl.program_id(1)

    @pl.when(t_idx == 0)
    def _():
        st_ref[...] = jnp.zeros_like(st_ref)

    x = x_ref[...]
    h = _rms(x, gn_ref[...], RMS_EPS).astype(BF16)
    p_ref[...] = _dot(h, win_ref[...])

    alr = p_ref[:, 2560:2688]
    a_hi, a_lo = _split2(alr)
    w_hi, w_lo = _split2(wal_ref[...])
    z = _dot(a_hi, w_hi) + _dot(a_hi, w_lo) + _dot(a_lo, w_hi) + bal_ref[...]
    la_ref[...] = -_softplus(-z) * (1.0 / GLA_TAU)

    tril = _tril_ones(GLA_CHUNK, BF16)
    rk = _iota((GLA_KEY, GLA_KEY), 0) >> 6
    ck = _iota((GLA_KEY, GLA_KEY), 1) >> 6
    k4_mask = rk == ck
    rv = _iota((GLA_KEY, GLA_WIDTH), 0) >> 6
    cv = _iota((GLA_KEY, GLA_WIDTH), 1) >> 7
    vbd_mask = rv == cv
    causal = _iota((GLA_CHUNK, GLA_KEY), 0) >= (_iota((GLA_CHUNK, GLA_KEY), 1) & 63)
    rs = _iota((GLA_WIDTH, GLA_KEY), 0) >> 7
    cs = _iota((GLA_WIDTH, GLA_KEY), 1) >> 6
    st_mask = rs == cs

    def gla_chunk(c, carry):
        rows = pl.ds(pl.multiple_of(c * GLA_CHUNK, GLA_CHUNK), GLA_CHUNK)
        q = p_ref[rows, 0:256] * (GLA_DK ** -0.5)
        k = p_ref[rows, 256:512]
        v = p_ref[rows, 512:1024]
        b = _dot_exact_lhs(tril, la_ref[rows, :])
        b_last = b[GLA_CHUNK - 1:GLA_CHUNK, :]
        q_dec = q * jnp.exp(b)
        k_dec = k * jnp.exp(-b)
        k_state = k * jnp.exp(b_last - b)
        st = st_ref[...]
        k4 = jnp.where(k4_mask, jnp.concatenate([k_dec] * GLA_HEADS, axis=0), 0.0)
        att = jnp.where(causal, _dot_nt(q_dec, k4), 0.0)
        v_bd = jnp.where(vbd_mask, jnp.concatenate([v] * GLA_HEADS, axis=0), 0.0)
        o = _dot(att, v_bd) + _dot_nt(q_dec, st)
        st_ref[...] = st * jnp.exp(b_last) + jnp.where(st_mask, _dot_tn(v, k_state), 0.0)
        for hh in range(GLA_HEADS):
            oh = o[:, hh * GLA_DV:(hh + 1) * GLA_DV]
            oh = oh * lax.rsqrt(jnp.mean(oh * oh, axis=-1, keepdims=True) + RMS_EPS)
            oa_ref[rows, hh * GLA_DV:(hh + 1) * GLA_DV] = oh
        return carry

    lax.fori_loop(0, tb // GLA_CHUNK, gla_chunk, 0)

    g = p_ref[:, 1024:1536]
    o_a = oa_ref[...] * glan_ref[...] * _silu(g)

    sv = _layernorm(_gelu_tanh(p_ref[:, 2048:2560]), lng_ref[...], lnb_ref[...], LN_EPS)
    sv = sv.astype(BF16)
    tri128 = _iota((SGU_CHUNK, SGU_CHUNK), 0) >= _iota((SGU_CHUNK, SGU_CHUNK), 1)
    for gg in range(SGU_GROUPS):
        wg = jnp.where(tri128, sguw_ref[gg], 0.0).astype(BF16)
        for c in range(tb // SGU_CHUNK):
            r0 = c * SGU_CHUNK
            s = _dot(wg, sv[r0:r0 + SGU_CHUNK, gg * LANE:(gg + 1) * LANE])
            s = s + sgub_ref[:, gg * LANE:(gg + 1) * LANE]
            u = _gelu_tanh(p_ref[r0:r0 + SGU_CHUNK, 1536 + gg * LANE:1536 + (gg + 1) * LANE])
            ob_ref[r0:r0 + SGU_CHUNK, gg * LANE:(gg + 1) * LANE] = u * s

    m = _dot(o_a.astype(BF16), wout_ref[0:512, :]) + _dot(ob_ref[...].astype(BF16), wout_ref[512:1024, :])
    o_ref[...] = x + m


def _even_layer(x, norm_g, w_in, w_out, w_alpha2, b_alpha, gla_norm, ln_g, ln_b, sgu_w, sgu_b, *, tb):
    B, T, D = x.shape
    w_in_r = jnp.concatenate(
        [w_in[:, :1536], w_in[:, 1552:2576], w_in[:, 1536:1552],
         jnp.zeros((D, EVEN_COLS - 2576), w_in.dtype)], axis=1).astype(BF16)
    wal = jnp.zeros((LANE, GLA_KEY), F32).at[:GLA_GATE_RANK].set(w_alpha2.astype(F32))
    sgub = jnp.repeat(sgu_b.astype(F32).T, LANE, axis=1)
    row = lambda a: a.astype(F32).reshape(1, -1)
    const = lambda shape: pl.BlockSpec(shape, lambda b, t: (0,) * len(shape))
    kern = functools.partial(_even_kernel, tb=tb)
    return pl.pallas_call(
        kern,
        grid=(B, T // tb),
        in_specs=[
            pl.BlockSpec((None, tb, D), lambda b, t: (b, t, 0)),
            const((1, D)), const((D, EVEN_COLS)), const((LANE, GLA_KEY)), const((1, GLA_KEY)),
            const((1, GLA_WIDTH)), const((1, SGU_WIDTH)), const((1, SGU_WIDTH)),
            const((SGU_GROUPS, SGU_CHUNK, SGU_CHUNK)), const((SGU_CHUNK, SGU_WIDTH)),
            const((D, D)),
        ],
        out_specs=pl.BlockSpec((None, tb, D), lambda b, t: (b, t, 0)),
        out_shape=jax.ShapeDtypeStruct((B, T, D), F32),
        scratch_shapes=[
            pltpu.VMEM((tb, EVEN_COLS), F32),
            pltpu.VMEM((tb, GLA_KEY), F32),
            pltpu.VMEM((tb, GLA_WIDTH), F32),
            pltpu.VMEM((tb, SGU_WIDTH), F32),
            pltpu.VMEM((GLA_WIDTH, GLA_KEY), F32),
        ],
        compiler_params=pltpu.CompilerParams(
            dimension_semantics=("arbitrary", "arbitrary"), vmem_limit_bytes=VMEM_LIMIT),
        name="even_mixer",
    )(x, row(norm_g), w_in_r, wal, row(b_alpha), row(gla_norm), row(ln_g), row(ln_b),
      sgu_w.astype(F32), sgub, w_out.astype(BF16))


def _odd_kernel(x_ref, gn_ref, win_ref, mu_ref, wlr_ref, w0_ref, a0_ref, kk_ref, ka_ref, rk_ref,
                gng_ref, gnb_ref, cw_ref, cb_ref, clg_ref, clb_ref, wout_ref, o_ref,
                p_ref, prev_ref, zc_ref, cv_s, cum_s, last_s, y_s, rt_s, pt_s, qt_s, kt_s, qh_s, kh_s, v_s,
                z_s, gl_s, ark_s, uw_s, *, tb):
    t_idx = pl.program_id(1)
    gw = RWKV_GROUP * RWKV_HEAD
    n_groups = RWKV_WIDTH // gw
    n_chunks = tb // RWKV_CHUNK

    @pl.when(t_idx == 0)
    def _():
        prev_ref[...] = jnp.zeros_like(prev_ref)
        zc_ref[0:CONV_HALO, :] = jnp.zeros((CONV_HALO, CONV_WIDTH), F32)
        zc_ref[tb + CONV_HALO:tb + CONV_HALO + 8, :] = jnp.zeros((8, CONV_WIDTH), F32)
        z_s[...] = jnp.zeros_like(z_s)

    @pl.when(t_idx != 0)
    def _():
        zc_ref[0:CONV_HALO, :] = zc_ref[tb:tb + CONV_HALO, :]

    x = x_ref[...]
    h = _rms(x, gn_ref[...], RMS_EPS).astype(BF16)
    p_ref[...] = _dot(h, win_ref[...])

    zc_ref[CONV_HALO:CONV_HALO + tb, :] = p_ref[:, RWKV_COLS:RWKV_COLS + CONV_WIDTH] * _sigmoid(
        p_ref[:, RWKV_COLS + CONV_WIDTH:ODD_COLS])

    def conv_tile(r0, n, lanes):
        first = CONV_HALO - (CONV_KERNEL - 1)
        wn = n + CONV_HALO + 8
        win = zc_ref[pl.ds(r0, wn), lanes]
        acc = jnp.zeros((n, LANE), F32) + cb_ref[:, lanes]
        for ph in range(8):
            sh = win if ph == 0 else pltpu.roll(win, wn - ph, 0)
            for j in range(CONV_KERNEL):
                if (j + first) % 8 == ph:
                    a8 = (j + first) // 8 * 8
                    acc = acc + sh[a8:a8 + n] * cw_ref[j:j + 1, lanes]
        cv_s[pl.ds(r0, n), lanes] = acc

    pr = p_ref[:, 0:RWKV_COLS]
    shifted = pltpu.roll(pr, 1, 0)
    shifted = jnp.where(_iota((tb, RWKV_COLS), 0) == 0, prev_ref[7:8, :], shifted)
    prev_ref[...] = p_ref[tb - 8:tb, 0:RWKV_COLS]
    hs = pr + (shifted - pr) * mu_ref[...]
    r = hs[:, 0:512]
    k = hs[:, 512:1024]
    v = hs[:, 1024:1536]
    lr = hs[:, 1536:1792]
    lane = _iota((tb, 256), 1)
    f = jnp.where(lane < RWKV_DECAY_RANK, jnp.tanh(lr),
                  jnp.where(lane < RWKV_DECAY_RANK + RWKV_AAA_RANK, lr, _sigmoid(lr)))
    lo3 = _dot(f.astype(BF16), wlr_ref[...])
    wlog = -_softplus(-(w0_ref[...] + lo3[:, 0:512])) - 0.5
    lw = -jnp.exp(wlog)
    a = _sigmoid(a0_ref[...] + lo3[:, 512:1024])
    gate = lo3[:, 1024:1536]

    hm = jnp.where((_iota((RWKV_WIDTH, RWKV_WIDTH), 0) >> 6) == (_iota((RWKV_WIDTH, RWKV_WIDTH), 1) >> 6),
                   1.0, 0.0).astype(BF16)
    kk = k * kk_ref[...]
    kk = kk / jnp.maximum(jnp.sqrt(_dot((kk * kk).astype(BF16), hm)), 1e-12)
    k2 = k * (1.0 + (a - 1.0) * ka_ref[...])
    bonus = _dot((r * k2 * rk_ref[...]).astype(BF16), hm) * v

    tril_ones = jnp.concatenate([_tril_ones(RWKV_CHUNK, BF16),
                                 jnp.ones((RWKV_CHUNK, RWKV_CHUNK), BF16)], axis=0)
    for c in range(n_chunks):
        cl = _dot_exact_lhs(tril_ones, lw[c * RWKV_CHUNK:(c + 1) * RWKV_CHUNK])
        cum_s[c * RWKV_CHUNK:(c + 1) * RWKV_CHUNK, :] = cl[0:RWKV_CHUNK]
        last_s[c * RWKV_CHUNK:(c + 1) * RWKV_CHUNK, :] = cl[RWKV_CHUNK:2 * RWKV_CHUNK]
    cum = cum_s[...]
    last = last_s[...]
    for c in range(n_chunks):
        gl_s[c] = jnp.exp(last[c * RWKV_CHUNK:c * RWKV_CHUNK + 8])
    e_neg = jnp.exp(-cum)
    e_hat = jnp.exp(last - cum)
    qa = kk * a
    rt_s[...] = (r * jnp.exp(cum)).astype(BF16)
    pt_s[...] = (-kk * jnp.exp(cum - lw)).astype(BF16)
    qt_s[...] = (qa * e_neg).astype(BF16)
    kt_s[...] = (k2 * e_neg).astype(BF16)
    qh_s[...] = (qa * e_hat).astype(BF16)
    kh_s[...] = (k2 * e_hat).astype(BF16)
    v_s[...] = v.astype(BF16)

    ri = _iota((gw, gw), 0)
    ci = _iota((gw, gw), 1)
    same = (ri >> 6) == (ci >> 6)
    strict = same & (ri > ci)
    ri2 = _iota((gw, 2 * gw), 0)
    ci2 = _iota((gw, 2 * gw), 1) & (gw - 1)
    incl2 = ((ri2 >> 6) == (ci2 >> 6)) & (ri2 >= ci2)
    eye = jnp.where(ri == ci, 1.0, 0.0)

    zero_bf = jnp.zeros((gw, gw), BF16)

    def bd(xs):
        return jnp.where(same, jnp.concatenate([xs] * RWKV_GROUP, axis=0), zero_bf)

    def chunk_local(ci_, carry):
        probs = [(cc, j) for cc in range(LOCAL_CHUNKS) for j in range(n_groups)]
        idx = range(len(probs))
        base = pl.multiple_of(ci_ * (LOCAL_CHUNKS * RWKV_CHUNK), LOCAL_CHUNKS * RWKV_CHUNK)

        def blk(ref, cc, j):
            return bd(ref[pl.ds(base + cc * RWKV_CHUNK, RWKV_CHUNK), j * gw:(j + 1) * gw])

        def slot(cc, j):
            return (ci_ * LOCAL_CHUNKS + cc) * n_groups + j

        P2 = [blk(pt_s, cc, j) for cc, j in probs]
        PR = [jnp.concatenate([P2[i], blk(rt_s, cc, j)], axis=0) for i, (cc, j) in enumerate(probs)]
        QK = [jnp.concatenate([blk(qt_s, cc, j), blk(kt_s, cc, j)], axis=0) for cc, j in probs]
        V2 = [blk(v_s, cc, j) for cc, j in probs]
        sc =[_dot_nt(PR[i], QK[i]) for i in idx]
        A_pq = [jnp.where(strict, sc[i][0:gw, 0:gw], 0.0) for i in idx]
        A_pk = [jnp.where(strict, sc[i][0:gw, gw:2 * gw], 0.0).astype(BF16) for i in idx]
        for i, (cc, j) in enumerate(probs):
            ark_s[slot(cc, j)] = jnp.where(incl2, sc[i][gw:2 * gw, :], 0.0).astype(BF16)
        Xb = [A_pq[i].astype(BF16) for i in idx]
        Xp = [_dot(Xb[i], Xb[i]) for i in idx]
        av = [_dot(A_pk[i], V2[i]).astype(BF16) for i in idx]
        Tm = [eye + A_pq[i] for i in idx]
        for step in range(4):
            Xb = [Xp[i].astype(BF16) for i in idx]
            XT = [_dot(Xb[i], jnp.concatenate([Xb[i], Tm[i].astype(BF16)], axis=1)) for i in idx]
            Xp = [XT[i][:, 0:gw] for i in idx]
            Tm = [Tm[i] + XT[i][:, gw:2 * gw] for i in idx]
        XT = [_dot(Xp[i].astype(BF16), Tm[i].astype(BF16)) for i in idx]
        Tm = [(Tm[i] + XT[i]).astype(BF16) for i in idx]
        uw = [_dot(Tm[i], jnp.concatenate([av[i], P2[i]], axis=1)) for i in idx]
        for i, (cc, j) in enumerate(probs):
            uw_s[slot(cc, j)] = uw[i].astype(BF16)
        return carry

    def chunk_state(c, carry):
        rows = pl.ds(pl.multiple_of(c * RWKV_CHUNK, RWKV_CHUNK), RWKV_CHUNK)
        e_last = gl_s[c][0:1, :]
        groups = range(n_groups)
        sls = [slice(j * gw, (j + 1) * gw) for j in groups]
        R2 = [bd(rt_s[rows, s]) for s in sls]
        QK = [jnp.concatenate([bd(qh_s[rows, s]), bd(kh_s[rows, s])], axis=0) for s in sls]
        V2 = [bd(v_s[rows, s]) for s in sls]
        uw = [uw_s[c * n_groups + j] for j in groups]
        Z = [z_s[j] for j in groups]
        hw = [_dot_nt(jnp.concatenate([uw[j][:, gw:2 * gw], R2[j]], axis=0), Z[j].astype(BF16))
              for j in groups]
        UV = [jnp.concatenate([(uw[j][:, 0:gw].astype(F32) + hw[j][0:gw]).astype(BF16), V2[j]], axis=0)
              for j in groups]
        zn = [_dot_tn(UV[j], QK[j]) for j in groups]
        Y2 = [hw[j][gw:2 * gw] + _dot(ark_s[c * n_groups + j], UV[j]) for j in groups]
        for j in groups:
            z_s[j] = Z[j] * e_last[:, sls[j]] + zn[j]
            y = Y2[j][0:RWKV_CHUNK]
            for e in range(1, RWKV_GROUP):
                y = y + Y2[j][e * RWKV_CHUNK:(e + 1) * RWKV_CHUNK]
            y_s[rows, sls[j]] = y
        for lb in range(CONV_WIDTH // LANE):
            conv_tile(pl.multiple_of(c * RWKV_CHUNK, RWKV_CHUNK), RWKV_CHUNK,
                      slice(lb * LANE, (lb + 1) * LANE))
        return carry

    lax.fori_loop(0, n_chunks // LOCAL_CHUNKS, chunk_local, 0)
    lax.fori_loop(0, n_chunks, chunk_state, 0)

    o_d = _silu(_layernorm(cv_s[...], clg_ref[...], clb_ref[...], LN_EPS))

    y = y_s[...]
    inv_n = 1.0 / RWKV_HEAD
    mu_y = _dot(y.astype(BF16), hm) * inv_n
    yc = y - mu_y
    var_y = _dot((yc * yc).astype(BF16), hm) * inv_n
    y = yc * lax.rsqrt(var_y + RWKV_GN_EPS) * gng_ref[...] + gnb_ref[...] + bonus
    o_c = y * gate

    m = _dot(o_c.astype(BF16), wout_ref[0:512, :]) + _dot(o_d.astype(BF16), wout_ref[512:1024, :])
    o_ref[...] = x + m


def _odd_layer(x, norm_g, w_in, w_out, mu, w0, w2, a0, a2, g2, k_k, k_a, r_k, gn_g, gn_b,
               conv_w, conv_b, cln_g, cln_b, *, tb):
    B, T, D = x.shape
    pad_lr = 256 - (RWKV_DECAY_RANK + RWKV_AAA_RANK + RWKV_GATE_RANK)
    w_in_r = jnp.concatenate(
        [w_in[:, :RWKV_IN], jnp.zeros((D, pad_lr), w_in.dtype), w_in[:, RWKV_IN:]], axis=1).astype(BF16)
    mu_r = jnp.concatenate([mu.astype(F32), jnp.zeros((pad_lr,), F32)]).reshape(1, RWKV_COLS)
    wlr = jnp.zeros((256, 3 * RWKV_WIDTH), F32)
    wlr = wlr.at[0:32, 0:512].set(w2.astype(F32))
    wlr = wlr.at[32:64, 512:1024].set(a2.astype(F32))
    wlr = wlr.at[64:160, 1024:1536].set(g2.astype(F32))
    row = lambda a: a.astype(F32).reshape(1, -1)
    const = lambda shape: pl.BlockSpec(shape, lambda b, t: (0,) * len(shape))
    gw = RWKV_GROUP * RWKV_HEAD
    kern = functools.partial(_odd_kernel, tb=tb)
    return pl.pallas_call(
        kern,
        grid=(B, T // tb),
        in_specs=[
            pl.BlockSpec((None, tb, D), lambda b, t: (b, t, 0)),
            const((1, D)), const((D, ODD_COLS)), const((1, RWKV_COLS)), const((256, 3 * RWKV_WIDTH)),
            const((1, 512)), const((1, 512)), const((1, 512)), const((1, 512)), const((1, 512)),
            const((1, 512)), const((1, 512)),
            const((CONV_KERNEL, CONV_WIDTH)), const((1, 512)), const((1, 512)), const((1, 512)),
            const((D, D)),
        ],
        out_specs=pl.BlockSpec((None, tb, D), lambda b, t: (b, t, 0)),
        out_shape=jax.ShapeDtypeStruct((B, T, D), F32),
        scratch_shapes=[
            pltpu.VMEM((tb, ODD_COLS), F32),
            pltpu.VMEM((8, RWKV_COLS), F32),
            pltpu.VMEM((tb + CONV_HALO + 8, CONV_WIDTH), F32),
            pltpu.VMEM((tb, 512), F32),
            pltpu.VMEM((tb, 512), F32), pltpu.VMEM((tb, 512), F32),
            pltpu.VMEM((tb, 512), F32),
            pltpu.VMEM((tb, 512), BF16), pltpu.VMEM((tb, 512), BF16), pltpu.VMEM((tb, 512), BF16),
            pltpu.VMEM((tb, 512), BF16), pltpu.VMEM((tb, 512), BF16), pltpu.VMEM((tb, 512), BF16),
            pltpu.VMEM((tb, 512), BF16),
            pltpu.VMEM((RWKV_WIDTH // gw, gw, gw), F32),
            pltpu.VMEM((tb // RWKV_CHUNK, 8, RWKV_WIDTH), F32),
            pltpu.VMEM((tb // RWKV_CHUNK * (RWKV_WIDTH // gw), gw, 2 * gw), BF16),
            pltpu.VMEM((tb // RWKV_CHUNK * (RWKV_WIDTH // gw), gw, 2 * gw), BF16),
        ],
        compiler_params=pltpu.CompilerParams(
            dimension_semantics=("arbitrary", "arbitrary"), vmem_limit_bytes=VMEM_LIMIT),
        name="odd_mixer",
    )(x, row(norm_g), w_in_r, mu_r, wlr.astype(BF16), row(w0), row(a0), row(k_k), row(k_a), row(r_k),
      row(gn_g), row(gn_b), conv_w.astype(F32), row(conv_b), row(cln_g), row(cln_b), w_out.astype(BF16))


def _mlp_kernel(x_ref, g_ref, wup_ref, wdn_ref, gf_ref, o_ref, *, ff_chunk, final_norm):
    x = x_ref[...]
    xb = _rms(x, g_ref[...], RMS_EPS).astype(BF16)
    acc = x
    d_ff = wup_ref.shape[1]
    for f0 in range(0, d_ff, ff_chunk):
        hcol = _dot(xb, wup_ref[:, f0:f0 + ff_chunk])
        hcol = jnp.square(jnp.maximum(hcol, 0.0))
        acc = acc + _dot(hcol.astype(BF16), wdn_ref[f0:f0 + ff_chunk, :])
    if final_norm:
        acc = _rms(acc, gf_ref[...], RMS_EPS)
    o_ref[...] = acc


def _mlp_layer(x2, norm_g, w_up, w_down, final_g, *, tm, final_norm):
    M, D = x2.shape
    d_ff = w_up.shape[1]
    kern = functools.partial(_mlp_kernel, ff_chunk=1024, final_norm=final_norm)
    return pl.pallas_call(
        kern,
        grid=(M // tm,),
        in_specs=[
            pl.BlockSpec((tm, D), lambda i: (i, 0)),
            pl.BlockSpec((1, D), lambda i: (0, 0)),
            pl.BlockSpec((D, d_ff), lambda i: (0, 0)),
            pl.BlockSpec((d_ff, D), lambda i: (0, 0)),
            pl.BlockSpec((1, D), lambda i: (0, 0)),
        ],
        out_specs=pl.BlockSpec((tm, D), lambda i: (i, 0)),
        out_shape=jax.ShapeDtypeStruct((M, D), F32),
        compiler_params=pltpu.CompilerParams(
            dimension_semantics=("arbitrary",), vmem_limit_bytes=VMEM_LIMIT),
        name="mlp_final" if final_norm else "mlp",
    )(x2, norm_g.astype(F32).reshape(1, D), w_up.astype(BF16), w_down.astype(BF16),
      final_g.astype(F32).reshape(1, D))


def _pick_block(n, target):
    b = min(n, target)
    while n % b:
        b //= 2
    return b


def kernel(x, norm_mix, norm_ffn, w_up, w_down, norm_final, even_w_in, even_w_out, gla_w_alpha2, gla_b_alpha, gla_norm, sgu_ln_g, sgu_ln_b, sgu_w, sgu_b, odd_w_in, odd_w_out, rwkv_mu, rwkv_w0, rwkv_w2, rwkv_a0, rwkv_a2, rwkv_g2, rwkv_k_k, rwkv_k_a, rwkv_r_k, rwkv_gn_g, rwkv_gn_b, conv_w, conv_b, conv_ln_g, conv_ln_b):
    B, T, D = x.shape
    depth = norm_mix.shape[0]
    tb = _pick_block(T, 512)
    tm = _pick_block(B * T, 512)
    assert tb % SGU_CHUNK == 0 and D == 1024
    for layer in range(depth):
        j = layer // 2
        if layer % 2 == 0:
            x = _even_layer(x, norm_mix[layer], even_w_in[j], even_w_out[j], gla_w_alpha2[j],
                            gla_b_alpha[j], gla_norm[j], sgu_ln_g[j], sgu_ln_b[j], sgu_w[j], sgu_b[j],
                            tb=tb)
        else:
            x = _odd_layer(x, norm_mix[layer], odd_w_in[j], odd_w_out[j], rwkv_mu[j], rwkv_w0[j],
                           rwkv_w2[j], rwkv_a0[j], rwkv_a2[j], rwkv_g2[j], rwkv_k_k[j], rwkv_k_a[j],
                           rwkv_r_k[j].reshape(-1), rwkv_gn_g[j], rwkv_gn_b[j], conv_w[j], conv_b[j],
                           conv_ln_g[j], conv_ln_b[j], tb=tb)
        last = layer == depth - 1
        x = _mlp_layer(x.reshape(B * T, D), norm_ffn[layer], w_up[layer], w_down[layer], norm_final,
                       tm=tm, final_norm=last).reshape(B, T, D)
    return x
```

```python
import functools

import jax
import jax.numpy as jnp
from jax import lax
from jax.experimental import pallas as pl
from jax.experimental.pallas import tpu as pltpu

F32 = jnp.float32
BF16 = jnp.bfloat16

RMS_EPS = 1e-6
LN_EPS = 1e-5

GLA_HEADS = 4
GLA_DK = 64
GLA_DV = 128
GLA_KEY = GLA_HEADS * GLA_DK
GLA_WIDTH = GLA_HEADS * GLA_DV
GLA_GATE_RANK = 16
GLA_TAU = 16.0
GLA_CHUNK = 64
GLA_LOCAL = 2
SGU_WIDTH = 512
SGU_GROUPS = 4
SGU_CHUNK = 128
EVEN_COLS = 2688

RWKV_WIDTH = 512
RWKV_HEAD = 64
RWKV_HEADS = 8
RWKV_DECAY_RANK = 32
RWKV_AAA_RANK = 32
RWKV_GATE_RANK = 96
RWKV_GN_EPS = 64e-5
RWKV_IN = 1696
RWKV_COLS = 1792
RWKV_CHUNK = 64
RWKV_GROUP = 2
LOCAL_CHUNKS = 2
CONV_WIDTH = 512
CONV_KERNEL = 31
CONV_HALO = 32
ODD_COLS = RWKV_COLS + 2 * CONV_WIDTH

LANE = 128
VMEM_LIMIT = 56 * 1024 * 1024
EVEN_BLOCK = 1024
ODD_BLOCK = 512
MLP_BLOCK = 1024


def _iota(shape, dim):
    return lax.broadcasted_iota(jnp.int32, shape, dim)


def _dot(a, b):
    return jnp.dot(a, b, preferred_element_type=F32)


def _dot_nt(a, b):
    return lax.dot_general(a, b, (((1,), (1,)), ((), ())), preferred_element_type=F32)


def _dot_tn(a, b):
    return lax.dot_general(a, b, (((0,), (0,)), ((), ())), preferred_element_type=F32)


def _split2(x):
    hi = x.astype(BF16)
    lo = (x - hi.astype(F32)).astype(BF16)
    return hi, lo


def _dot_exact_lhs(m_bf16, x):
    hi, lo = _split2(x)
    return _dot(m_bf16, hi) + _dot(m_bf16, lo)


def _dot_exact_rhs(x, m_bf16):
    hi, lo = _split2(x)
    return _dot(hi, m_bf16) + _dot(lo, m_bf16)


def _rms(x, g, eps):
    return x * lax.rsqrt(jnp.mean(x * x, axis=-1, keepdims=True) + eps) * g


def _layernorm(x, g, b, eps):
    mu = jnp.mean(x, axis=-1, keepdims=True)
    xc = x - mu
    var = jnp.mean(xc * xc, axis=-1, keepdims=True)
    return xc * lax.rsqrt(var + eps) * g + b


def _sigmoid(x):
    return 1.0 / (1.0 + jnp.exp(-x))


def _silu(x):
    return x * _sigmoid(x)


def _softplus(x):
    return jnp.maximum(x, 0.0) + jnp.log(1.0 + jnp.exp(-jnp.abs(x)))


def _gelu_tanh(x):
    c = 0.7978845608028654
    return 0.5 * x * (1.0 + jnp.tanh(c * (x + 0.044715 * (x * x * x))))


def _tril_ones(n, dtype):
    return jnp.where(_iota((n, n), 0) >= _iota((n, n), 1), 1.0, 0.0).astype(dtype)


def _even_kernel(x_ref, gn_ref, win_ref, wal_ref, bal_ref, glan_ref, lng_ref, lnb_ref,
                 sguw_ref, sgub_ref, wout_ref, o_ref,
                 p_ref, la_ref, bl_ref, dl_ref, qd_ref, kd_ref, ks_ref, vb_ref, oa_ref, ob_ref, st_ref,
                 *, tb):
    t_idx = pl.program_id(1)

    @pl.when(t_idx == 0)
    def _():
        st_ref[...] = jnp.zeros_like(st_ref)

    x = x_ref[...]
    h = _rms(x, gn_ref[...], RMS_EPS).astype(BF16)
    p_ref[...] = _dot(h, win_ref[...])

    alr = p_ref[:, 2560:2688]
    a_hi, a_lo = _split2(alr)
    w_hi, w_lo = _split2(wal_ref[...])
    z = _dot(a_hi, w_hi) + _dot(a_hi, w_lo) + _dot(a_lo, w_hi) + bal_ref[...]
    la = -_softplus(-z) * (1.0 / GLA_TAU)

    n_chunks = tb // GLA_CHUNK
    tril_ones = jnp.concatenate([_tril_ones(GLA_CHUNK, BF16),
                                 jnp.ones((GLA_CHUNK, GLA_CHUNK), BF16)], axis=0)
    for c in range(n_chunks):
        cl = _dot_exact_lhs(tril_ones, la[c * GLA_CHUNK:(c + 1) * GLA_CHUNK])
        la_ref[c * GLA_CHUNK:(c + 1) * GLA_CHUNK, :] = cl[0:GLA_CHUNK]
        bl_ref[c * GLA_CHUNK:(c + 1) * GLA_CHUNK, :] = cl[GLA_CHUNK:2 * GLA_CHUNK]
    b = la_ref[...]
    b_last = bl_ref[...]
    for c in range(n_chunks):
        dl_ref[c] = jnp.exp(b_last[c * GLA_CHUNK:c * GLA_CHUNK + 8])
    k = p_ref[:, 256:512]
    qd_ref[...] = (p_ref[:, 0:256] * (GLA_DK ** -0.5) * jnp.exp(b)).astype(BF16)
    kd_ref[...] = (k * jnp.exp(-b)).astype(BF16)
    ks_ref[...] = (k * jnp.exp(b_last - b)).astype(BF16)
    vb_ref[...] = p_ref[:, 512:1024].astype(BF16)

    rk = _iota((GLA_KEY, GLA_KEY), 0) >> 6
    ck = _iota((GLA_KEY, GLA_KEY), 1) >> 6
    k4_mask = rk == ck
    rv = _iota((GLA_KEY, GLA_WIDTH), 0) >> 6
    cv = _iota((GLA_KEY, GLA_WIDTH), 1) >> 7
    vbd_mask = rv == cv
    causal = _iota((GLA_CHUNK, GLA_KEY), 0) >= (_iota((GLA_CHUNK, GLA_KEY), 1) & 63)
    rs = _iota((GLA_WIDTH, GLA_KEY), 0) >> 7
    cs = _iota((GLA_WIDTH, GLA_KEY), 1) >> 6
    st_mask = rs == cs

    zero_k = jnp.zeros((GLA_KEY, GLA_KEY), BF16)
    zero_v = jnp.zeros((GLA_KEY, GLA_WIDTH), BF16)

    def gla_step(ci_, carry):
        base = pl.multiple_of(ci_ * (GLA_LOCAL * GLA_CHUNK), GLA_LOCAL * GLA_CHUNK)
        ccs = range(GLA_LOCAL)
        rows = [pl.ds(base + cc * GLA_CHUNK, GLA_CHUNK) for cc in ccs]
        q_dec = [qd_ref[r, :] for r in rows]
        k4 = [jnp.where(k4_mask, jnp.concatenate([kd_ref[r, :]] * GLA_HEADS, axis=0), zero_k)
              for r in rows]
        v = [vb_ref[r, :] for r in rows]
        v_bd = [jnp.where(vbd_mask, jnp.concatenate([v[cc]] * GLA_HEADS, axis=0), zero_v) for cc in ccs]
        att = [_dot_nt(q_dec[cc], k4[cc]) for cc in ccs]
        kv = [_dot_tn(v[cc], ks_ref[rows[cc], :]) for cc in ccs]
        att = [jnp.where(causal, att[cc], 0.0).astype(BF16) for cc in ccs]
        o_intra = [_dot(att[cc], v_bd[cc]) for cc in ccs]
        st = st_ref[...]
        for cc in ccs:
            o = o_intra[cc] + _dot_nt(q_dec[cc], st.astype(BF16))
            dl = dl_ref[ci_ * GLA_LOCAL + cc][0:1, :]
            st = st * dl + jnp.where(st_mask, kv[cc], 0.0)
            for hh in range(GLA_HEADS):
                oh = o[:, hh * GLA_DV:(hh + 1) * GLA_DV]
                oh = oh * lax.rsqrt(jnp.mean(oh * oh, axis=-1, keepdims=True) + RMS_EPS)
                oa_ref[rows[cc], hh * GLA_DV:(hh + 1) * GLA_DV] = oh
        st_ref[...] = st
        return carry

    lax.fori_loop(0, n_chunks // GLA_LOCAL, gla_step, 0)

    g = p_ref[:, 1024:1536]
    o_a = oa_ref[...] * glan_ref[...] * _silu(g)

    sv = _layernorm(_gelu_tanh(p_ref[:, 2048:2560]), lng_ref[...], lnb_ref[...], LN_EPS)
    sv = sv.astype(BF16)
    tri128 = _iota((SGU_CHUNK, SGU_CHUNK), 0) >= _iota((SGU_CHUNK, SGU_CHUNK), 1)
    for gg in range(SGU_GROUPS):
        wg = jnp.where(tri128, sguw_ref[gg], 0.0).astype(BF16)
        for c in range(tb // SGU_CHUNK):
            r0 = c * SGU_CHUNK
            s = _dot(wg, sv[r0:r0 + SGU_CHUNK, gg * LANE:(gg + 1) * LANE])
            s = s + sgub_ref[:, gg * LANE:(gg + 1) * LANE]
            u = _gelu_tanh(p_ref[r0:r0 + SGU_CHUNK, 1536 + gg * LANE:1536 + (gg + 1) * LANE])
            ob_ref[r0:r0 + SGU_CHUNK, gg * LANE:(gg + 1) * LANE] = u * s

    m = _dot(o_a.astype(BF16), wout_ref[0:512, :]) + _dot(ob_ref[...].astype(BF16), wout_ref[512:1024, :])
    o_ref[...] = x + m


def _even_layer(x, norm_g, w_in, w_out, w_alpha2, b_alpha, gla_norm, ln_g, ln_b, sgu_w, sgu_b, *, tb):
    B, T, D = x.shape
    w_in_r = jnp.concatenate(
        [w_in[:, :1536], w_in[:, 1552:2576], w_in[:, 1536:1552],
         jnp.zeros((D, EVEN_COLS - 2576), w_in.dtype)], axis=1).astype(BF16)
    wal = jnp.zeros((LANE, GLA_KEY), F32).at[:GLA_GATE_RANK].set(w_alpha2.astype(F32))
    sgub = jnp.repeat(sgu_b.astype(F32).T, LANE, axis=1)
    row = lambda a: a.astype(F32).reshape(1, -1)
    const = lambda shape: pl.BlockSpec(shape, lambda b, t: (0,) * len(shape),
                                       pipeline_mode=pl.Buffered(1))
    kern = functools.partial(_even_kernel, tb=tb)
    return pl.pallas_call(
        kern,
        grid=(B, T // tb),
        in_specs=[
            pl.BlockSpec((None, tb, D), lambda b, t: (b, t, 0)),
            const((1, D)), const((D, EVEN_COLS)), const((LANE, GLA_KEY)), const((1, GLA_KEY)),
            const((1, GLA_WIDTH)), const((1, SGU_WIDTH)), const((1, SGU_WIDTH)),
            const((SGU_GROUPS, SGU_CHUNK, SGU_CHUNK)), const((SGU_CHUNK, SGU_WIDTH)),
            const((D, D)),
        ],
        out_specs=pl.BlockSpec((None, tb, D), lambda b, t: (b, t, 0)),
        out_shape=jax.ShapeDtypeStruct((B, T, D), F32),
        scratch_shapes=[
            pltpu.VMEM((tb, EVEN_COLS), F32),
            pltpu.VMEM((tb, GLA_KEY), F32),
            pltpu.VMEM((tb, GLA_KEY), F32),
            pltpu.VMEM((tb // GLA_CHUNK, 8, GLA_KEY), F32),
            pltpu.VMEM((tb, GLA_KEY), BF16), pltpu.VMEM((tb, GLA_KEY), BF16),
            pltpu.VMEM((tb, GLA_KEY), BF16),
            pltpu.VMEM((tb, GLA_WIDTH), BF16),
            pltpu.VMEM((tb, GLA_WIDTH), F32),
            pltpu.VMEM((tb, SGU_WIDTH), F32),
            pltpu.VMEM((GLA_WIDTH, GLA_KEY), F32),
        ],
        compiler_params=pltpu.CompilerParams(
            dimension_semantics=("arbitrary", "arbitrary"), vmem_limit_bytes=VMEM_LIMIT),
        name="even_mixer",
    )(x, row(norm_g), w_in_r, wal, row(b_alpha), row(gla_norm), row(ln_g), row(ln_b),
      sgu_w.astype(F32), sgub, w_out.astype(BF16))


def _odd_kernel(x_ref, gn_ref, win_ref, mu_ref, wlr_ref, w0_ref, a0_ref, kk_ref, ka_ref, rk_ref,
                gng_ref, gnb_ref, cw_ref, cb_ref, clg_ref, clb_ref, wout_ref, o_ref,
                p_ref, prev_ref, zc_ref, cv_s, cum_s, last_s, y_s, rt_s, pt_s, qt_s, kt_s, qh_s, kh_s, v_s,
                z_s, gl_s, ark_s, uw_s, *, tb):
    t_idx = pl.program_id(1)
    gw = RWKV_GROUP * RWKV_HEAD
    n_groups = RWKV_WIDTH // gw
    n_chunks = tb // RWKV_CHUNK

    @pl.when(t_idx == 0)
    def _():
        prev_ref[...] = jnp.zeros_like(prev_ref)
        zc_ref[0:CONV_HALO, :] = jnp.zeros((CONV_HALO, CONV_WIDTH), F32)
        zc_ref[tb + CONV_HALO:tb + CONV_HALO + 8, :] = jnp.zeros((8, CONV_WIDTH), F32)
        z_s[...] = jnp.zeros_like(z_s)

    @pl.when(t_idx != 0)
    def _():
        zc_ref[0:CONV_HALO, :] = zc_ref[tb:tb + CONV_HALO, :]

    x = x_ref[...]
    h = _rms(x, gn_ref[...], RMS_EPS).astype(BF16)
    p_ref[...] = _dot(h, win_ref[...])

    zc_ref[CONV_HALO:CONV_HALO + tb, :] = p_ref[:, RWKV_COLS:RWKV_COLS + CONV_WIDTH] * _sigmoid(
        p_ref[:, RWKV_COLS + CONV_WIDTH:ODD_COLS])

    def conv_tile(r0, n, lanes):
        first = CONV_HALO - (CONV_KERNEL - 1)
        wn = n + CONV_HALO + 8
        win = zc_ref[pl.ds(r0, wn), lanes]
        acc = jnp.zeros((n, LANE), F32) + cb_ref[:, lanes]
        for ph in range(8):
            sh = win if ph == 0 else pltpu.roll(win, wn - ph, 0)
            for j in range(CONV_KERNEL):
                if (j + first) % 8 == ph:
                    a8 = (j + first) // 8 * 8
                    acc = acc + sh[a8:a8 + n] * cw_ref[j:j + 1, lanes]
        cv_s[pl.ds(r0, n), lanes] = acc

    pr = p_ref[:, 0:RWKV_COLS]
    shifted = pltpu.roll(pr, 1, 0)
    shifted = jnp.where(_iota((tb, RWKV_COLS), 0) == 0, prev_ref[7:8, :], shifted)
    prev_ref[...] = p_ref[tb - 8:tb, 0:RWKV_COLS]
    hs = pr + (shifted - pr) * mu_ref[...]
    r = hs[:, 0:512]
    k = hs[:, 512:1024]
    v = hs[:, 1024:1536]
    lr = hs[:, 1536:1792]
    lane = _iota((tb, 256), 1)
    f = jnp.where(lane < RWKV_DECAY_RANK, jnp.tanh(lr),
                  jnp.where(lane < RWKV_DECAY_RANK + RWKV_AAA_RANK, lr, _sigmoid(lr)))
    lo3 = _dot(f.astype(BF16), wlr_ref[...])
    wlog = -_softplus(-(w0_ref[...] + lo3[:, 0:512])) - 0.5
    lw = -jnp.exp(wlog)
    a = _sigmoid(a0_ref[...] + lo3[:, 512:1024])
    gate = lo3[:, 1024:1536]

    hm = jnp.where((_iota((RWKV_WIDTH, RWKV_WIDTH), 0) >> 6) == (_iota((RWKV_WIDTH, RWKV_WIDTH), 1) >> 6),
                   1.0, 0.0).astype(BF16)
    kk = k * kk_ref[...]
    kk = kk / jnp.maximum(jnp.sqrt(_dot((kk * kk).astype(BF16), hm)), 1e-12)
    k2 = k * (1.0 + (a - 1.0) * ka_ref[...])
    bonus = _dot((r * k2 * rk_ref[...]).astype(BF16), hm) * v

    tril_ones = jnp.concatenate([_tril_ones(RWKV_CHUNK, BF16),
                                 jnp.ones((RWKV_CHUNK, RWKV_CHUNK), BF16)], axis=0)
    for c in range(n_chunks):
        cl = _dot_exact_lhs(tril_ones, lw[c * RWKV_CHUNK:(c + 1) * RWKV_CHUNK])
        cum_s[c * RWKV_CHUNK:(c + 1) * RWKV_CHUNK, :] = cl[0:RWKV_CHUNK]
        last_s[c * RWKV_CHUNK:(c + 1) * RWKV_CHUNK, :] = cl[RWKV_CHUNK:2 * RWKV_CHUNK]
    cum = cum_s[...]
    last = last_s[...]
    for c in range(n_chunks):
        gl_s[c] = jnp.exp(last[c * RWKV_CHUNK:c * RWKV_CHUNK + 8])
    e_neg = jnp.exp(-cum)
    e_hat = jnp.exp(last - cum)
    qa = kk * a
    rt_s[...] = (r * jnp.exp(cum)).astype(BF16)
    pt_s[...] = (-kk * jnp.exp(cum - lw)).astype(BF16)
    qt_s[...] = (qa * e_neg).astype(BF16)
    kt_s[...] = (k2 * e_neg).astype(BF16)
    qh_s[...] = (qa * e_hat).astype(BF16)
    kh_s[...] = (k2 * e_hat).astype(BF16)
    v_s[...] = v.astype(BF16)

    ri = _iota((gw, gw), 0)
    ci = _iota((gw, gw), 1)
    same = (ri >> 6) == (ci >> 6)
    strict = same & (ri > ci)
    ri2 = _iota((gw, 2 * gw), 0)
    ci2 = _iota((gw, 2 * gw), 1) & (gw - 1)
    incl2 = ((ri2 >> 6) == (ci2 >> 6)) & (ri2 >= ci2)
    eye = jnp.where(ri == ci, 1.0, 0.0)

    zero_bf = jnp.zeros((gw, gw), BF16)

    def bd(xs):
        return jnp.where(same, jnp.concatenate([xs] * RWKV_GROUP, axis=0), zero_bf)

    def chunk_local(ci_, carry):
        probs = [(cc, j) for cc in range(LOCAL_CHUNKS) for j in range(n_groups)]
        idx = range(len(probs))
        base = pl.multiple_of(ci_ * (LOCAL_CHUNKS * RWKV_CHUNK), LOCAL_CHUNKS * RWKV_CHUNK)

        def blk(ref, cc, j):
            return bd(ref[pl.ds(base + cc * RWKV_CHUNK, RWKV_CHUNK), j * gw:(j + 1) * gw])

        def slot(cc, j):
            return (ci_ * LOCAL_CHUNKS + cc) * n_groups + j

        P2 = [blk(pt_s, cc, j) for cc, j in probs]
        PR = [jnp.concatenate([P2[i], blk(rt_s, cc, j)], axis=0) for i, (cc, j) in enumerate(probs)]
        QK = [jnp.concatenate([blk(qt_s, cc, j), blk(kt_s, cc, j)], axis=0) for cc, j in probs]
        V2 = [blk(v_s, cc, j) for cc, j in probs]
        sc =[_dot_nt(PR[i], QK[i]) for i in idx]
        A_pq = [jnp.where(strict, sc[i][0:gw, 0:gw], 0.0) for i in idx]
        A_pk = [jnp.where(strict, sc[i][0:gw, gw:2 * gw], 0.0).astype(BF16) for i in idx]
        for i, (cc, j) in enumerate(probs):
            ark_s[slot(cc, j)] = jnp.where(incl2, sc[i][gw:2 * gw, :], 0.0).astype(BF16)
        Xb = [A_pq[i].astype(BF16) for i in idx]
        Xp = [_dot(Xb[i], Xb[i]) for i in idx]
        av = [_dot(A_pk[i], V2[i]).astype(BF16) for i in idx]
        Tm = [eye + A_pq[i] for i in idx]
        for step in range(4):
            Xb = [Xp[i].astype(BF16) for i in idx]
            XT = [_dot(Xb[i], jnp.concatenate([Xb[i], Tm[i].astype(BF16)], axis=1)) for i in idx]
            Xp = [XT[i][:, 0:gw] for i in idx]
            Tm = [Tm[i] + XT[i][:, gw:2 * gw] for i in idx]
        XT = [_dot(Xp[i].astype(BF16), Tm[i].astype(BF16)) for i in idx]
        Tm = [(Tm[i] + XT[i]).astype(BF16) for i in idx]
        uw = [_dot(Tm[i], jnp.concatenate([av[i], P2[i]], axis=1)) for i in idx]
        for i, (cc, j) in enumerate(probs):
            uw_s[slot(cc, j)] = uw[i].astype(BF16)
        return carry

    def chunk_state(c, carry):
        rows = pl.ds(pl.multiple_of(c * RWKV_CHUNK, RWKV_CHUNK), RWKV_CHUNK)
        e_last = gl_s[c][0:1, :]
        groups = range(n_groups)
        sls = [slice(j * gw, (j + 1) * gw) for j in groups]
        R2 = [bd(rt_s[rows, s]) for s in sls]
        QK = [jnp.concatenate([bd(qh_s[rows, s]), bd(kh_s[rows, s])], axis=0) for s in sls]
        V2 = [bd(v_s[rows, s]) for s in sls]
        uw = [uw_s[c * n_groups + j] for j in groups]
        Z = [z_s[j] for j in groups]
        hw = [_dot_nt(jnp.concatenate([uw[j][:, gw:2 * gw], R2[j]], axis=0), Z[j].astype(BF16))
              for j in groups]
        UV = [jnp.concatenate([(uw[j][:, 0:gw].astype(F32) + hw[j][0:gw]).astype(BF16), V2[j]], axis=0)
              for j in groups]
        zn = [_dot_tn(UV[j], QK[j]) for j in groups]
        Y2 = [hw[j][gw:2 * gw] + _dot(ark_s[c * n_groups + j], UV[j]) for j in groups]
        for j in groups:
            z_s[j] = Z[j] * e_last[:, sls[j]] + zn[j]
            y = Y2[j][0:RWKV_CHUNK]
            for e in range(1, RWKV_GROUP):
                y = y + Y2[j][e * RWKV_CHUNK:(e + 1) * RWKV_CHUNK]
            y_s[rows, sls[j]] = y
        for lb in range(CONV_WIDTH // LANE):
            conv_tile(pl.multiple_of(c * RWKV_CHUNK, RWKV_CHUNK), RWKV_CHUNK,
                      slice(lb * LANE, (lb + 1) * LANE))
        return carry

    lax.fori_loop(0, n_chunks // LOCAL_CHUNKS, chunk_local, 0)
    lax.fori_loop(0, n_chunks, chunk_state, 0)

    o_d = _silu(_layernorm(cv_s[...], clg_ref[...], clb_ref[...], LN_EPS))

    y = y_s[...]
    inv_n = 1.0 / RWKV_HEAD
    mu_y = _dot(y.astype(BF16), hm) * inv_n
    yc = y - mu_y
    var_y = _dot((yc * yc).astype(BF16), hm) * inv_n
    y = yc * lax.rsqrt(var_y + RWKV_GN_EPS) * gng_ref[...] + gnb_ref[...] + bonus
    o_c = y * gate

    m = _dot(o_c.astype(BF16), wout_ref[0:512, :]) + _dot(o_d.astype(BF16), wout_ref[512:1024, :])
    o_ref[...] = x + m


def _odd_layer(x, norm_g, w_in, w_out, mu, w0, w2, a0, a2, g2, k_k, k_a, r_k, gn_g, gn_b,
               conv_w, conv_b, cln_g, cln_b, *, tb):
    B, T, D = x.shape
    pad_lr = 256 - (RWKV_DECAY_RANK + RWKV_AAA_RANK + RWKV_GATE_RANK)
    w_in_r = jnp.concatenate(
        [w_in[:, :RWKV_IN], jnp.zeros((D, pad_lr), w_in.dtype), w_in[:, RWKV_IN:]], axis=1).astype(BF16)
    mu_r = jnp.concatenate([mu.astype(F32), jnp.zeros((pad_lr,), F32)]).reshape(1, RWKV_COLS)
    wlr = jnp.zeros((256, 3 * RWKV_WIDTH), F32)
    wlr = wlr.at[0:32, 0:512].set(w2.astype(F32))
    wlr = wlr.at[32:64, 512:1024].set(a2.astype(F32))
    wlr = wlr.at[64:160, 1024:1536].set(g2.astype(F32))
    row = lambda a: a.astype(F32).reshape(1, -1)
    const = lambda shape: pl.BlockSpec(shape, lambda b, t: (0,) * len(shape),
                                       pipeline_mode=pl.Buffered(1))
    gw = RWKV_GROUP * RWKV_HEAD
    kern = functools.partial(_odd_kernel, tb=tb)
    return pl.pallas_call(
        kern,
        grid=(B, T // tb),
        in_specs=[
            pl.BlockSpec((None, tb, D), lambda b, t: (b, t, 0)),
            const((1, D)), const((D, ODD_COLS)), const((1, RWKV_COLS)), const((256, 3 * RWKV_WIDTH)),
            const((1, 512)), const((1, 512)), const((1, 512)), const((1, 512)), const((1, 512)),
            const((1, 512)), const((1, 512)),
            const((CONV_KERNEL, CONV_WIDTH)), const((1, 512)), const((1, 512)), const((1, 512)),
            const((D, D)),
        ],
        out_specs=pl.BlockSpec((None, tb, D), lambda b, t: (b, t, 0)),
        out_shape=jax.ShapeDtypeStruct((B, T, D), F32),
        scratch_shapes=[
            pltpu.VMEM((tb, ODD_COLS), F32),
            pltpu.VMEM((8, RWKV_COLS), F32),
            pltpu.VMEM((tb + CONV_HALO + 8, CONV_WIDTH), F32),
            pltpu.VMEM((tb, 512), F32),
            pltpu.VMEM((tb, 512), F32), pltpu.VMEM((tb, 512), F32),
            pltpu.VMEM((tb, 512), F32),
            pltpu.VMEM((tb, 512), BF16), pltpu.VMEM((tb, 512), BF16), pltpu.VMEM((tb, 512), BF16),
            pltpu.VMEM((tb, 512), BF16), pltpu.VMEM((tb, 512), BF16), pltpu.VMEM((tb, 512), BF16),
            pltpu.VMEM((tb, 512), BF16),
            pltpu.VMEM((RWKV_WIDTH // gw, gw, gw), F32),
            pltpu.VMEM((tb // RWKV_CHUNK, 8, RWKV_WIDTH), F32),
            pltpu.VMEM((tb // RWKV_CHUNK * (RWKV_WIDTH // gw), gw, 2 * gw), BF16),
            pltpu.VMEM((tb // RWKV_CHUNK * (RWKV_WIDTH // gw), gw, 2 * gw), BF16),
        ],
        compiler_params=pltpu.CompilerParams(
            dimension_semantics=("arbitrary", "arbitrary"), vmem_limit_bytes=VMEM_LIMIT),
        name="odd_mixer",
    )(x, row(norm_g), w_in_r, mu_r, wlr.astype(BF16), row(w0), row(a0), row(k_k), row(k_a), row(r_k),
      row(gn_g), row(gn_b), conv_w.astype(F32), row(conv_b), row(cln_g), row(cln_b), w_out.astype(BF16))


def _mlp_kernel(x_ref, g_ref, wup_ref, wdn_ref, gf_ref, o_ref, *, ff_chunk, final_norm):
    x = x_ref[...]
    xb = _rms(x, g_ref[...], RMS_EPS).astype(BF16)
    acc = x
    d_ff = wup_ref.shape[1]
    for f0 in range(0, d_ff, ff_chunk):
        hcol = _dot(xb, wup_ref[:, f0:f0 + ff_chunk])
        hcol = jnp.square(jnp.maximum(hcol, 0.0))
        acc = acc + _dot(hcol.astype(BF16), wdn_ref[f0:f0 + ff_chunk, :])
    if final_norm:
        acc = _rms(acc, gf_ref[...], RMS_EPS)
    o_ref[...] = acc


def _mlp_layer(x2, norm_g, w_up, w_down, final_g, *, tm, final_norm):
    M, D = x2.shape
    d_ff = w_up.shape[1]
    kern = functools.partial(_mlp_kernel, ff_chunk=1024, final_norm=final_norm)
    return pl.pallas_call(
        kern,
        grid=(M // tm,),
        in_specs=[
            pl.BlockSpec((tm, D), lambda i: (i, 0)),
            pl.BlockSpec((1, D), lambda i: (0, 0)),
            pl.BlockSpec((D, d_ff), lambda i: (0, 0), pipeline_mode=pl.Buffered(1)),
            pl.BlockSpec((d_ff, D), lambda i: (0, 0), pipeline_mode=pl.Buffered(1)),
            pl.BlockSpec((1, D), lambda i: (0, 0)),
        ],
        out_specs=pl.BlockSpec((tm, D), lambda i: (i, 0)),
        out_shape=jax.ShapeDtypeStruct((M, D), F32),
        compiler_params=pltpu.CompilerParams(
            dimension_semantics=("arbitrary",), vmem_limit_bytes=VMEM_LIMIT),
        name="mlp_final" if final_norm else "mlp",
    )(x2, norm_g.astype(F32).reshape(1, D), w_up.astype(BF16), w_down.astype(BF16),
      final_g.astype(F32).reshape(1, D))


def _pick_block(n, target):
    b = min(n, target)
    while n % b:
        b //= 2
    return b


def kernel(x, norm_mix, norm_ffn, w_up, w_down, norm_final, even_w_in, even_w_out, gla_w_alpha2, gla_b_alpha, gla_norm, sgu_ln_g, sgu_ln_b, sgu_w, sgu_b, odd_w_in, odd_w_out, rwkv_mu, rwkv_w0, rwkv_w2, rwkv_a0, rwkv_a2, rwkv_g2, rwkv_k_k, rwkv_k_a, rwkv_r_k, rwkv_gn_g, rwkv_gn_b, conv_w, conv_b, conv_ln_g, conv_ln_b):
    B, T, D = x.shape
    depth = norm_mix.shape[0]
    tb_even = _pick_block(T, EVEN_BLOCK)
    tb_odd = _pick_block(T, ODD_BLOCK)
    tm = _pick_block(B * T, MLP_BLOCK)
    assert tb_even % (GLA_LOCAL * GLA_CHUNK) == 0 and tb_even % SGU_CHUNK == 0
    assert tb_odd % (LOCAL_CHUNKS * RWKV_CHUNK) == 0 and D == 1024
    for layer in range(depth):
        j = layer // 2
        if layer % 2 == 0:
            x = _even_layer(x, norm_mix[layer], even_w_in[j], even_w_out[j], gla_w_alpha2[j],
                            gla_b_alpha[j], gla_norm[j], sgu_ln_g[j], sgu_ln_b[j], sgu_w[j], sgu_b[j],
                            tb=tb_even)
        else:
            x = _odd_layer(x, norm_mix[layer], odd_w_in[j], odd_w_out[j], rwkv_mu[j], rwkv_w0[j],
                           rwkv_w2[j], rwkv_a0[j], rwkv_a2[j], rwkv_g2[j], rwkv_k_k[j], rwkv_k_a[j],
                           rwkv_r_k[j].reshape(-1), rwkv_gn_g[j], rwkv_gn_b[j], conv_w[j], conv_b[j],
                           conv_ln_g[j], conv_ln_b[j], tb=tb_odd)
        last = layer == depth - 1
        x = _mlp_layer(x.reshape(B * T, D), norm_ffn[layer], w_up[layer], w_down[layer], norm_final,
                       tm=tm, final_norm=last).reshape(B, T, D)
    return x
```

```python
import functools

import jax
import jax.numpy as jnp
from jax import lax
from jax.experimental import pallas as pl
from jax.experimental.pallas import tpu as pltpu

F32 = jnp.float32
BF16 = jnp.bfloat16

RMS_EPS = 1e-6
LN_EPS = 1e-5

GLA_HEADS = 4
GLA_DK = 64
GLA_DV = 128
GLA_KEY = GLA_HEADS * GLA_DK
GLA_WIDTH = GLA_HEADS * GLA_DV
GLA_GATE_RANK = 16
GLA_TAU = 16.0
GLA_CHUNK = 64
GLA_LOCAL = 2
SGU_WIDTH = 512
SGU_GROUPS = 4
SGU_CHUNK = 128
EVEN_COLS = 2688

RWKV_WIDTH = 512
RWKV_HEAD = 64
RWKV_HEADS = 8
RWKV_DECAY_RANK = 32
RWKV_AAA_RANK = 32
RWKV_GATE_RANK = 96
RWKV_GN_EPS = 64e-5
RWKV_IN = 1696
RWKV_COLS = 1792
RWKV_CHUNK = 64
RWKV_GROUP = 2
LOCAL_CHUNKS = 2
PIPELINE_ORDER = "LSLLSLSLLSLL"
CONV_WIDTH = 512
CONV_KERNEL = 31
CONV_HALO = 32
ODD_COLS = RWKV_COLS + 2 * CONV_WIDTH

LANE = 128
VMEM_LIMIT = 56 * 1024 * 1024
EVEN_BLOCK = 1024
ODD_BLOCK = 512
MLP_BLOCK = 1024


def _iota(shape, dim):
    return lax.broadcasted_iota(jnp.int32, shape, dim)


def _dot(a, b):
    return jnp.dot(a, b, preferred_element_type=F32)


def _dot_nt(a, b):
    return lax.dot_general(a, b, (((1,), (1,)), ((), ())), preferred_element_type=F32)


def _dot_tn(a, b):
    return lax.dot_general(a, b, (((0,), (0,)), ((), ())), preferred_element_type=F32)


def _split2(x):
    hi = x.astype(BF16)
    lo = (x - hi.astype(F32)).astype(BF16)
    return hi, lo


def _chunk_cumsum(tril2, x):
    hi, lo = _split2(x)
    cum = _dot(tril2, jnp.concatenate([hi, lo], axis=0))
    n = x.shape[0]
    return cum, jnp.broadcast_to(cum[n - 1:n], cum.shape)


def _head_sums(x, hm_half):
    half = hm_half.shape[0]
    xb = x.astype(BF16)
    return jnp.concatenate([_dot(xb[:, 0:half], hm_half), _dot(xb[:, half:2 * half], hm_half)], axis=1)


def _rms(x, g, eps):
    return x * lax.rsqrt(jnp.mean(x * x, axis=-1, keepdims=True) + eps) * g


def _layernorm(x, g, b, eps):
    mu = jnp.mean(x, axis=-1, keepdims=True)
    xc = x - mu
    var = jnp.mean(xc * xc, axis=-1, keepdims=True)
    return xc * lax.rsqrt(var + eps) * g + b


def _sigmoid(x):
    return 1.0 / (1.0 + jnp.exp(-x))


def _silu(x):
    return x * _sigmoid(x)


def _softplus(x):
    return jnp.maximum(x, 0.0) + jnp.log(1.0 + jnp.exp(-jnp.abs(x)))


def _gelu_tanh(x):
    c = 0.7978845608028654
    return 0.5 * x * (1.0 + jnp.tanh(c * (x + 0.044715 * (x * x * x))))


def _tril_ones(n, dtype):
    return jnp.where(_iota((n, n), 0) >= _iota((n, n), 1), 1.0, 0.0).astype(dtype)


def _even_kernel(x_ref, gn_ref, win_ref, wal_ref, bal_ref, glan_ref, lng_ref, lnb_ref,
                 sguw_ref, sgub_ref, wout_ref, o_ref,
                 p_ref, la_ref, bl_ref, dl_ref, qd_ref, kd_ref, ks_ref, vb_ref, oa_ref, ob_ref, st_ref,
                 *, tb):
    t_idx = pl.program_id(1)

    @pl.when(t_idx == 0)
    def _():
        st_ref[...] = jnp.zeros_like(st_ref)

    x = x_ref[...]
    h = _rms(x, gn_ref[...], RMS_EPS).astype(BF16)
    p_ref[...] = _dot(h, win_ref[...])

    alr = p_ref[:, 2560:2688]
    a_hi, a_lo = _split2(alr)
    w_hi, w_lo = _split2(wal_ref[...])
    z = _dot(a_hi, w_hi) + _dot(a_hi, w_lo) + _dot(a_lo, w_hi) + bal_ref[...]
    la = -_softplus(-z) * (1.0 / GLA_TAU)

    n_chunks = tb // GLA_CHUNK
    tril2 = jnp.concatenate([_tril_ones(GLA_CHUNK, BF16)] * 2, axis=1)
    for c in range(n_chunks):
        cum_c, last_c = _chunk_cumsum(tril2, la[c * GLA_CHUNK:(c + 1) * GLA_CHUNK])
        la_ref[c * GLA_CHUNK:(c + 1) * GLA_CHUNK, :] = cum_c
        bl_ref[c * GLA_CHUNK:(c + 1) * GLA_CHUNK, :] = last_c
    b = la_ref[...]
    b_last = bl_ref[...]
    for c in range(n_chunks):
        dl_ref[c] = jnp.exp(b_last[c * GLA_CHUNK:c * GLA_CHUNK + 8])
    k = p_ref[:, 256:512]
    qd_ref[...] = (p_ref[:, 0:256] * (GLA_DK ** -0.5) * jnp.exp(b)).astype(BF16)
    kd_ref[...] = (k * jnp.exp(-b)).astype(BF16)
    ks_ref[...] = (k * jnp.exp(b_last - b)).astype(BF16)
    vb_ref[...] = p_ref[:, 512:1024].astype(BF16)

    rk = _iota((GLA_KEY, GLA_KEY), 0) >> 6
    ck = _iota((GLA_KEY, GLA_KEY), 1) >> 6
    k4_mask = rk == ck
    rv = _iota((GLA_KEY, GLA_WIDTH), 0) >> 6
    cv = _iota((GLA_KEY, GLA_WIDTH), 1) >> 7
    vbd_mask = rv == cv
    causal = _iota((GLA_CHUNK, GLA_KEY), 0) >= (_iota((GLA_CHUNK, GLA_KEY), 1) & 63)
    rs = _iota((GLA_WIDTH, GLA_KEY), 0) >> 7
    cs = _iota((GLA_WIDTH, GLA_KEY), 1) >> 6
    st_mask = rs == cs

    zero_k = jnp.zeros((GLA_KEY, GLA_KEY), BF16)
    zero_v = jnp.zeros((GLA_KEY, GLA_WIDTH), BF16)

    def gla_step(ci_, carry):
        base = pl.multiple_of(ci_ * (GLA_LOCAL * GLA_CHUNK), GLA_LOCAL * GLA_CHUNK)
        ccs = range(GLA_LOCAL)
        rows = [pl.ds(base + cc * GLA_CHUNK, GLA_CHUNK) for cc in ccs]
        q_dec = [qd_ref[r, :] for r in rows]
        k4 = [jnp.where(k4_mask, jnp.concatenate([kd_ref[r, :]] * GLA_HEADS, axis=0), zero_k)
              for r in rows]
        v = [vb_ref[r, :] for r in rows]
        v_bd = [jnp.where(vbd_mask, jnp.concatenate([v[cc]] * GLA_HEADS, axis=0), zero_v) for cc in ccs]
        att = [_dot_nt(q_dec[cc], k4[cc]) for cc in ccs]
        kv = [_dot_tn(v[cc], ks_ref[rows[cc], :]) for cc in ccs]
        att = [jnp.where(causal, att[cc], 0.0).astype(BF16) for cc in ccs]
        o_intra = [_dot(att[cc], v_bd[cc]) for cc in ccs]
        st = st_ref[...]
        for cc in ccs:
            o = o_intra[cc] + _dot_nt(q_dec[cc], st.astype(BF16))
            dl = dl_ref[ci_ * GLA_LOCAL + cc][0:1, :]
            st = st * dl + jnp.where(st_mask, kv[cc], 0.0)
            for hh in range(GLA_HEADS):
                oh = o[:, hh * GLA_DV:(hh + 1) * GLA_DV]
                oh = oh * lax.rsqrt(jnp.mean(oh * oh, axis=-1, keepdims=True) + RMS_EPS)
                oa_ref[rows[cc], hh * GLA_DV:(hh + 1) * GLA_DV] = oh
        st_ref[...] = st
        return carry

    lax.fori_loop(0, n_chunks // GLA_LOCAL, gla_step, 0)

    g = p_ref[:, 1024:1536]
    o_a = oa_ref[...] * glan_ref[...] * _silu(g)

    sv = _layernorm(_gelu_tanh(p_ref[:, 2048:2560]), lng_ref[...], lnb_ref[...], LN_EPS)
    sv = sv.astype(BF16)
    tri128 = _iota((SGU_CHUNK, SGU_CHUNK), 0) >= _iota((SGU_CHUNK, SGU_CHUNK), 1)
    for gg in range(SGU_GROUPS):
        wg = jnp.where(tri128, sguw_ref[gg], 0.0).astype(BF16)
        for c in range(tb // SGU_CHUNK):
            r0 = c * SGU_CHUNK
            s = _dot(wg, sv[r0:r0 + SGU_CHUNK, gg * LANE:(gg + 1) * LANE])
            s = s + sgub_ref[:, gg * LANE:(gg + 1) * LANE]
            u = _gelu_tanh(p_ref[r0:r0 + SGU_CHUNK, 1536 + gg * LANE:1536 + (gg + 1) * LANE])
            ob_ref[r0:r0 + SGU_CHUNK, gg * LANE:(gg + 1) * LANE] = u * s

    m = _dot(o_a.astype(BF16), wout_ref[0:512, :]) + _dot(ob_ref[...].astype(BF16), wout_ref[512:1024, :])
    o_ref[...] = x + m


def _even_layer(x, norm_g, w_in, w_out, w_alpha2, b_alpha, gla_norm, ln_g, ln_b, sgu_w, sgu_b, *, tb):
    B, T, D = x.shape
    w_in_r = jnp.concatenate(
        [w_in[:, :1536], w_in[:, 1552:2576], w_in[:, 1536:1552],
         jnp.zeros((D, EVEN_COLS - 2576), w_in.dtype)], axis=1).astype(BF16)
    wal = jnp.zeros((LANE, GLA_KEY), F32).at[:GLA_GATE_RANK].set(w_alpha2.astype(F32))
    sgub = jnp.repeat(sgu_b.astype(F32).T, LANE, axis=1)
    row = lambda a: a.astype(F32).reshape(1, -1)
    const = lambda shape: pl.BlockSpec(shape, lambda b, t: (0,) * len(shape),
                                       pipeline_mode=pl.Buffered(1))
    kern = functools.partial(_even_kernel, tb=tb)
    return pl.pallas_call(
        kern,
        grid=(B, T // tb),
        in_specs=[
            pl.BlockSpec((None, tb, D), lambda b, t: (b, t, 0)),
            const((1, D)), const((D, EVEN_COLS)), const((LANE, GLA_KEY)), const((1, GLA_KEY)),
            const((1, GLA_WIDTH)), const((1, SGU_WIDTH)), const((1, SGU_WIDTH)),
            const((SGU_GROUPS, SGU_CHUNK, SGU_CHUNK)), const((SGU_CHUNK, SGU_WIDTH)),
            const((D, D)),
        ],
        out_specs=pl.BlockSpec((None, tb, D), lambda b, t: (b, t, 0)),
        out_shape=jax.ShapeDtypeStruct((B, T, D), F32),
        scratch_shapes=[
            pltpu.VMEM((tb, EVEN_COLS), F32),
            pltpu.VMEM((tb, GLA_KEY), F32),
            pltpu.VMEM((tb, GLA_KEY), F32),
            pltpu.VMEM((tb // GLA_CHUNK, 8, GLA_KEY), F32),
            pltpu.VMEM((tb, GLA_KEY), BF16), pltpu.VMEM((tb, GLA_KEY), BF16),
            pltpu.VMEM((tb, GLA_KEY), BF16),
            pltpu.VMEM((tb, GLA_WIDTH), BF16),
            pltpu.VMEM((tb, GLA_WIDTH), F32),
            pltpu.VMEM((tb, SGU_WIDTH), F32),
            pltpu.VMEM((GLA_WIDTH, GLA_KEY), F32),
        ],
        compiler_params=pltpu.CompilerParams(
            dimension_semantics=("arbitrary", "arbitrary"), vmem_limit_bytes=VMEM_LIMIT),
        name="even_mixer",
    )(x, row(norm_g), w_in_r, wal, row(b_alpha), row(gla_norm), row(ln_g), row(ln_b),
      sgu_w.astype(F32), sgub, w_out.astype(BF16))


def _odd_kernel(x_ref, gn_ref, win_ref, mu_ref, wlr_ref, w0_ref, a0_ref, kk_ref, ka_ref, rk_ref,
                gng_ref, gnb_ref, cw_ref, cb_ref, clg_ref, clb_ref, wout_ref, o_ref,
                p_ref, prev_ref, zc_ref, cv_s, cum_s, last_s, y_s, rt_s, pt_s, qt_s, kt_s, qh_s, kh_s, v_s,
                z_s, gl_s, ark_s, uw_s, *, tb):
    t_idx = pl.program_id(1)
    gw = RWKV_GROUP * RWKV_HEAD
    n_groups = RWKV_WIDTH // gw
    n_chunks = tb // RWKV_CHUNK

    @pl.when(t_idx == 0)
    def _():
        prev_ref[...] = jnp.zeros_like(prev_ref)
        zc_ref[0:CONV_HALO, :] = jnp.zeros((CONV_HALO, CONV_WIDTH), F32)
        zc_ref[tb + CONV_HALO:tb + CONV_HALO + 8, :] = jnp.zeros((8, CONV_WIDTH), F32)
        z_s[...] = jnp.zeros_like(z_s)

    @pl.when(t_idx != 0)
    def _():
        zc_ref[0:CONV_HALO, :] = zc_ref[tb:tb + CONV_HALO, :]

    x = x_ref[...]
    h = _rms(x, gn_ref[...], RMS_EPS).astype(BF16)
    c_r, c_k, c_v = slice(0, 512), slice(512, 1024), slice(1024, 1536)
    c_lr = slice(3 * RWKV_WIDTH, RWKV_COLS)
    c_conv = slice(RWKV_COLS, ODD_COLS)

    def project(cols):
        p_ref[:, cols] = _dot(h, win_ref[:, cols])

    def shift_mix(cols):
        pr = p_ref[:, cols]
        sh = pltpu.roll(pr, 1, 0)
        head = jnp.where(_iota((8, pr.shape[1]), 0) == 0, prev_ref[7:8, cols], sh[0:8])
        sh = jnp.concatenate([head, sh[8:]], axis=0)
        return pr + (sh - pr) * mu_ref[:, cols]

    def conv_tile(r0, n, lanes):
        first = CONV_HALO - (CONV_KERNEL - 1)
        wn = n + CONV_HALO + 8
        win = zc_ref[pl.ds(r0, wn), lanes]
        acc = jnp.zeros((n, LANE), F32) + cb_ref[:, lanes]
        for ph in range(8):
            sh = win if ph == 0 else pltpu.roll(win, wn - ph, 0)
            for j in range(CONV_KERNEL):
                if (j + first) % 8 == ph:
                    a8 = (j + first) // 8 * 8
                    acc = acc + sh[a8:a8 + n] * cw_ref[j:j + 1, lanes]
        return acc

    hm = jnp.where((_iota((256, 256), 0) >> 6) == (_iota((256, 256), 1) >> 6),
                   1.0, 0.0).astype(BF16)
    project(c_lr)
    project(c_k)
    lr = shift_mix(c_lr)
    lane = _iota((tb, 256), 1)
    f = jnp.where(lane < RWKV_DECAY_RANK, jnp.tanh(lr),
                  jnp.where(lane < RWKV_DECAY_RANK + RWKV_AAA_RANK, lr, _sigmoid(lr)))
    project(c_r)
    lo3 = _dot(f.astype(BF16), wlr_ref[...])
    k = shift_mix(c_k)
    kk = k * kk_ref[...]
    kk = kk * lax.rsqrt(jnp.maximum(_head_sums(kk * kk, hm), 1e-24))
    project(c_v)
    r = shift_mix(c_r)
    lw = (-0.6065306597126334) * _sigmoid(w0_ref[...] + lo3[:, 0:512])
    a = _sigmoid(a0_ref[...] + lo3[:, 512:1024])
    gate = lo3[:, 1024:1536]
    k2 = k * (1.0 + (a - 1.0) * ka_ref[...])
    project(c_conv)
    v = shift_mix(c_v)
    prev_ref[...] = p_ref[tb - 8:tb, 0:RWKV_COLS]
    bonus = _head_sums(r * k2 * rk_ref[...], hm) * v

    tril2 = jnp.concatenate([_tril_ones(RWKV_CHUNK, BF16)] * 2, axis=1)
    for c in range(n_chunks):
        cum_c, last_c = _chunk_cumsum(tril2, lw[c * RWKV_CHUNK:(c + 1) * RWKV_CHUNK])
        cum_s[c * RWKV_CHUNK:(c + 1) * RWKV_CHUNK, :] = cum_c
        last_s[c * RWKV_CHUNK:(c + 1) * RWKV_CHUNK, :] = last_c
    cum = cum_s[...]
    e_last = jnp.exp(last_s[...])
    for c in range(n_chunks):
        gl_s[c] = e_last[c * RWKV_CHUNK:c * RWKV_CHUNK + 8]
    e_neg = jnp.exp(-cum)
    e_hat = e_last * e_neg
    qa = kk * a
    rt_s[...] = (r * jnp.exp(cum)).astype(BF16)
    pt_s[...] = (-kk * jnp.exp(cum - lw)).astype(BF16)
    qt_s[...] = (qa * e_neg).astype(BF16)
    kt_s[...] = (k2 * e_neg).astype(BF16)
    qh_s[...] = (qa * e_hat).astype(BF16)
    kh_s[...] = (k2 * e_hat).astype(BF16)
    v_s[...] = v.astype(BF16)

    zc_ref[CONV_HALO:CONV_HALO + tb, :] = p_ref[:, RWKV_COLS:RWKV_COLS + CONV_WIDTH] * _sigmoid(
        p_ref[:, RWKV_COLS + CONV_WIDTH:ODD_COLS])

    ri = _iota((gw, gw), 0)
    ci = _iota((gw, gw), 1)
    same = (ri >> 6) == (ci >> 6)
    strict = same & (ri > ci)
    ri2 = _iota((gw, 2 * gw), 0)
    ci2 = _iota((gw, 2 * gw), 1) & (gw - 1)
    incl2 = ((ri2 >> 6) == (ci2 >> 6)) & (ri2 >= ci2)
    eye = jnp.where(ri == ci, 1.0, 0.0)

    zero_bf = jnp.zeros((gw, gw), BF16)

    def bd(xs):
        return jnp.where(same, jnp.concatenate([xs] * RWKV_GROUP, axis=0), zero_bf)

    def aligned(i, m):
        return i if isinstance(i, int) else pl.multiple_of(i, m)

    def local_stages(ci_):
        probs = [(cc, j) for cc in range(LOCAL_CHUNKS) for j in range(n_groups)]
        idx = range(len(probs))
        base = aligned(ci_ * (LOCAL_CHUNKS * RWKV_CHUNK), LOCAL_CHUNKS * RWKV_CHUNK)

        def blk(ref, cc, j):
            return bd(ref[pl.ds(base + cc * RWKV_CHUNK, RWKV_CHUNK), j * gw:(j + 1) * gw])

        def slot(cc, j):
            return (ci_ * LOCAL_CHUNKS + cc) * n_groups + j

        P2 = [blk(pt_s, cc, j) for cc, j in probs]
        PR = [jnp.concatenate([P2[i], blk(rt_s, cc, j)], axis=0) for i, (cc, j) in enumerate(probs)]
        QK = [jnp.concatenate([blk(qt_s, cc, j), blk(kt_s, cc, j)], axis=0) for cc, j in probs]
        V2 = [blk(v_s, cc, j) for cc, j in probs]
        sc = [_dot_nt(PR[i], QK[i]) for i in idx]
        yield
        A_pq = [jnp.where(strict, sc[i][0:gw, 0:gw], 0.0) for i in idx]
        A_pk = [jnp.where(strict, sc[i][0:gw, gw:2 * gw], 0.0).astype(BF16) for i in idx]
        for i, (cc, j) in enumerate(probs):
            ark_s[slot(cc, j)] = jnp.where(incl2, sc[i][gw:2 * gw, :], 0.0).astype(BF16)
        Xb = [A_pq[i].astype(BF16) for i in idx]
        Xp = [_dot(Xb[i], Xb[i]) for i in idx]
        av = [_dot(A_pk[i], V2[i]).astype(BF16) for i in idx]
        yield
        Tm = [eye + A_pq[i] for i in idx]
        for step in range(4):
            Xb = [Xp[i].astype(BF16) for i in idx]
            XT = [_dot(Xb[i], jnp.concatenate([Xb[i], Tm[i].astype(BF16)], axis=1)) for i in idx]
            yield
            Xp = [XT[i][:, 0:gw] for i in idx]
            Tm = [Tm[i] + XT[i][:, gw:2 * gw] for i in idx]
        XT = [_dot(Xp[i].astype(BF16), Tm[i].astype(BF16)) for i in idx]
        yield
        Tm = [(Tm[i] + XT[i]).astype(BF16) for i in idx]
        uw = [_dot(Tm[i], jnp.concatenate([av[i], P2[i]], axis=1)) for i in idx]
        for i, (cc, j) in enumerate(probs):
            uw_s[slot(cc, j)] = uw[i].astype(BF16)
        yield

    def state_stages(ci_):
        groups = range(n_groups)
        sls = [slice(j * gw, (j + 1) * gw) for j in groups]
        for cc in range(LOCAL_CHUNKS):
            c = ci_ * LOCAL_CHUNKS + cc
            rows = pl.ds(aligned(c * RWKV_CHUNK, RWKV_CHUNK), RWKV_CHUNK)
            e_last = gl_s[c][0:1, :]
            R2 = [bd(rt_s[rows, s]) for s in sls]
            uw = [uw_s[c * n_groups + j] for j in groups]
            Z = [z_s[j] for j in groups]
            hw = [_dot_nt(jnp.concatenate([uw[j][:, gw:2 * gw], R2[j]], axis=0), Z[j].astype(BF16))
                  for j in groups]
            yield
            QK = [jnp.concatenate([bd(qh_s[rows, s]), bd(kh_s[rows, s])], axis=0) for s in sls]
            V2 = [bd(v_s[rows, s]) for s in sls]
            UV = [jnp.concatenate([(uw[j][:, 0:gw].astype(F32) + hw[j][0:gw]).astype(BF16), V2[j]],
                                  axis=0) for j in groups]
            zn = [_dot_tn(UV[j], QK[j]) for j in groups]
            Y2 = [hw[j][gw:2 * gw] + _dot(ark_s[c * n_groups + j], UV[j]) for j in groups]
            for j in groups:
                z_s[j] = Z[j] * e_last[:, sls[j]] + zn[j]
                y = Y2[j][0:RWKV_CHUNK]
                for e in range(1, RWKV_GROUP):
                    y = y + Y2[j][e * RWKV_CHUNK:(e + 1) * RWKV_CHUNK]
                y_s[rows, sls[j]] = y
            yield

    def conv_group(ci_):
        for cc in range(LOCAL_CHUNKS):
            r0 = aligned((ci_ * LOCAL_CHUNKS + cc) * RWKV_CHUNK, RWKV_CHUNK)
            for lb in range(CONV_WIDTH // LANE):
                lanes = slice(lb * LANE, (lb + 1) * LANE)
                cv_s[pl.ds(r0, RWKV_CHUNK), lanes] = conv_tile(r0, RWKV_CHUNK, lanes)

    n_groups_t = n_chunks // LOCAL_CHUNKS
    for _ in local_stages(0):
        pass
    conv_group(0)

    def pipelined(i, carry):
        g_local, g_state = local_stages(i), state_stages(i - 1)
        for which in PIPELINE_ORDER:
            next(g_local if which == "L" else g_state)
        conv_group(i)
        return carry

    lax.fori_loop(1, n_groups_t, pipelined, 0)
    for _ in state_stages(n_groups_t - 1):
        pass

    o_d = _silu(_layernorm(cv_s[...], clg_ref[...], clb_ref[...], LN_EPS))

    y = y_s[...]
    inv_n = 1.0 / RWKV_HEAD
    mu_y = _head_sums(y, hm) * inv_n
    yc = y - mu_y
    var_y = _head_sums(yc * yc, hm) * inv_n
    y = yc * lax.rsqrt(var_y + RWKV_GN_EPS) * gng_ref[...] + gnb_ref[...] + bonus
    o_c = y * gate

    m = _dot(o_c.astype(BF16), wout_ref[0:512, :]) + _dot(o_d.astype(BF16), wout_ref[512:1024, :])
    o_ref[...] = x + m


def _odd_layer(x, norm_g, w_in, w_out, mu, w0, w2, a0, a2, g2, k_k, k_a, r_k, gn_g, gn_b,
               conv_w, conv_b, cln_g, cln_b, *, tb):
    B, T, D = x.shape
    pad_lr = 256 - (RWKV_DECAY_RANK + RWKV_AAA_RANK + RWKV_GATE_RANK)
    w_in_r = jnp.concatenate(
        [w_in[:, :RWKV_IN], jnp.zeros((D, pad_lr), w_in.dtype), w_in[:, RWKV_IN:]], axis=1).astype(BF16)
    mu_r = jnp.concatenate([mu.astype(F32), jnp.zeros((pad_lr,), F32)]).reshape(1, RWKV_COLS)
    wlr = jnp.zeros((256, 3 * RWKV_WIDTH), F32)
    wlr = wlr.at[0:32, 0:512].set(w2.astype(F32))
    wlr = wlr.at[32:64, 512:1024].set(a2.astype(F32))
    wlr = wlr.at[64:160, 1024:1536].set(g2.astype(F32))
    row = lambda a: a.astype(F32).reshape(1, -1)
    const = lambda shape: pl.BlockSpec(shape, lambda b, t: (0,) * len(shape),
                                       pipeline_mode=pl.Buffered(1))
    gw = RWKV_GROUP * RWKV_HEAD
    kern = functools.partial(_odd_kernel, tb=tb)
    return pl.pallas_call(
        kern,
        grid=(B, T // tb),
        in_specs=[
            pl.BlockSpec((None, tb, D), lambda b, t: (b, t, 0)),
            const((1, D)), const((D, ODD_COLS)), const((1, RWKV_COLS)), const((256, 3 * RWKV_WIDTH)),
            const((1, 512)), const((1, 512)), const((1, 512)), const((1, 512)), const((1, 512)),
            const((1, 512)), const((1, 512)),
            const((CONV_KERNEL, CONV_WIDTH)), const((1, 512)), const((1, 512)), const((1, 512)),
            const((D, D)),
        ],
        out_specs=pl.BlockSpec((None, tb, D), lambda b, t: (b, t, 0)),
        out_shape=jax.ShapeDtypeStruct((B, T, D), F32),
        scratch_shapes=[
            pltpu.VMEM((tb, ODD_COLS), F32),
            pltpu.VMEM((8, RWKV_COLS), F32),
            pltpu.VMEM((tb + CONV_HALO + 8, CONV_WIDTH), F32),
            pltpu.VMEM((tb, 512), F32),
            pltpu.VMEM((tb, 512), F32), pltpu.VMEM((tb, 512), F32),
            pltpu.VMEM((tb, 512), F32),
            pltpu.VMEM((tb, 512), BF16), pltpu.VMEM((tb, 512), BF16), pltpu.VMEM((tb, 512), BF16),
            pltpu.VMEM((tb, 512), BF16), pltpu.VMEM((tb, 512), BF16), pltpu.VMEM((tb, 512), BF16),
            pltpu.VMEM((tb, 512), BF16),
            pltpu.VMEM((RWKV_WIDTH // gw, gw, gw), F32),
            pltpu.VMEM((tb // RWKV_CHUNK, 8, RWKV_WIDTH), F32),
            pltpu.VMEM((tb // RWKV_CHUNK * (RWKV_WIDTH // gw), gw, 2 * gw), BF16),
            pltpu.VMEM((tb // RWKV_CHUNK * (RWKV_WIDTH // gw), gw, 2 * gw), BF16),
        ],
        compiler_params=pltpu.CompilerParams(
            dimension_semantics=("arbitrary", "arbitrary"), vmem_limit_bytes=VMEM_LIMIT),
        name="odd_mixer",
    )(x, row(norm_g), w_in_r, mu_r, wlr.astype(BF16), row(w0), row(a0), row(k_k), row(k_a), row(r_k),
      row(gn_g), row(gn_b), conv_w.astype(F32), row(conv_b), row(cln_g), row(cln_b), w_out.astype(BF16))


def _mlp_kernel(x_ref, g_ref, wup_ref, wdn_ref, gf_ref, o_ref, *, ff_chunk, final_norm):
    x = x_ref[...]
    xb = _rms(x, g_ref[...], RMS_EPS).astype(BF16)
    acc = x
    d_ff = wup_ref.shape[1]
    for f0 in range(0, d_ff, ff_chunk):
        hcol = _dot(xb, wup_ref[:, f0:f0 + ff_chunk])
        hcol = jnp.square(jnp.maximum(hcol, 0.0))
        acc = acc + _dot(hcol.astype(BF16), wdn_ref[f0:f0 + ff_chunk, :])
    if final_norm:
        acc = _rms(acc, gf_ref[...], RMS_EPS)
    o_ref[...] = acc


def _mlp_layer(x2, norm_g, w_up, w_down, final_g, *, tm, final_norm):
    M, D = x2.shape
    d_ff = w_up.shape[1]
    kern = functools.partial(_mlp_kernel, ff_chunk=1024, final_norm=final_norm)
    return pl.pallas_call(
        kern,
        grid=(M // tm,),
        in_specs=[
            pl.BlockSpec((tm, D), lambda i: (i, 0)),
            pl.BlockSpec((1, D), lambda i: (0, 0)),
            pl.BlockSpec((D, d_ff), lambda i: (0, 0), pipeline_mode=pl.Buffered(1)),
            pl.BlockSpec((d_ff, D), lambda i: (0, 0), pipeline_mode=pl.Buffered(1)),
            pl.BlockSpec((1, D), lambda i: (0, 0)),
        ],
        out_specs=pl.BlockSpec((tm, D), lambda i: (i, 0)),
        out_shape=jax.ShapeDtypeStruct((M, D), F32),
        compiler_params=pltpu.CompilerParams(
            dimension_semantics=("arbitrary",), vmem_limit_bytes=VMEM_LIMIT),
        name="mlp_final" if final_norm else "mlp",
    )(x2, norm_g.astype(F32).reshape(1, D), w_up.astype(BF16), w_down.astype(BF16),
      final_g.astype(F32).reshape(1, D))


def _pick_block(n, target):
    b = min(n, target)
    while n % b:
        b //= 2
    return b


def kernel(x, norm_mix, norm_ffn, w_up, w_down, norm_final, even_w_in, even_w_out, gla_w_alpha2, gla_b_alpha, gla_norm, sgu_ln_g, sgu_ln_b, sgu_w, sgu_b, odd_w_in, odd_w_out, rwkv_mu, rwkv_w0, rwkv_w2, rwkv_a0, rwkv_a2, rwkv_g2, rwkv_k_k, rwkv_k_a, rwkv_r_k, rwkv_gn_g, rwkv_gn_b, conv_w, conv_b, conv_ln_g, conv_ln_b):
    B, T, D = x.shape
    depth = norm_mix.shape[0]
    tb_even = _pick_block(T, EVEN_BLOCK)
    tb_odd = _pick_block(T, ODD_BLOCK)
    tm = _pick_block(B * T, MLP_BLOCK)
    assert tb_even % (GLA_LOCAL * GLA_CHUNK) == 0 and tb_even % SGU_CHUNK == 0
    assert tb_odd % (LOCAL_CHUNKS * RWKV_CHUNK) == 0 and D == 1024
    for layer in range(depth):
        j = layer // 2
        if layer % 2 == 0:
            x = _even_layer(x, norm_mix[layer], even_w_in[j], even_w_out[j], gla_w_alpha2[j],
                            gla_b_alpha[j], gla_norm[j], sgu_ln_g[j], sgu_ln_b[j], sgu_w[j], sgu_b[j],
                            tb=tb_even)
        else:
            x = _odd_layer(x, norm_mix[layer], odd_w_in[j], odd_w_out[j], rwkv_mu[j], rwkv_w0[j],
                           rwkv_w2[j], rwkv_a0[j], rwkv_a2[j], rwkv_g2[j], rwkv_k_k[j], rwkv_k_a[j],
                           rwkv_r_k[j].reshape(-1), rwkv_gn_g[j], rwkv_gn_b[j], conv_w[j], conv_b[j],
                           conv_ln_g[j], conv_ln_b[j], tb=tb_odd)
        last = layer == depth - 1
        x = _mlp_layer(x.reshape(B * T, D), norm_ffn[layer], w_up[layer], w_down[layer], norm_final,
                       tm=tm, final_norm=last).reshape(B, T, D)
    return x
```

```python
import functools

import jax
import jax.numpy as jnp
from jax import lax
from jax.experimental import pallas as pl
from jax.experimental.pallas import tpu as pltpu

F32 = jnp.float32
BF16 = jnp.bfloat16

RMS_EPS = 1e-6
LN_EPS = 1e-5

GLA_HEADS = 4
GLA_DK = 64
GLA_DV = 128
GLA_KEY = GLA_HEADS * GLA_DK
GLA_WIDTH = GLA_HEADS * GLA_DV
GLA_GATE_RANK = 16
GLA_TAU = 16.0
GLA_CHUNK = 64
GLA_LOCAL = 2
SGU_WIDTH = 512
SGU_GROUPS = 4
SGU_CHUNK = 128
EVEN_COLS = 2688

RWKV_WIDTH = 512
RWKV_HEAD = 64
RWKV_HEADS = 8
RWKV_DECAY_RANK = 32
RWKV_AAA_RANK = 32
RWKV_GATE_RANK = 96
RWKV_GN_EPS = 64e-5
RWKV_IN = 1696
RWKV_COLS = 1792
RWKV_CHUNK = 64
RWKV_GROUP = 2
LOCAL_CHUNKS = 2
LOCAL_STAGES = 8
PIPELINE_ORDER = "LSLLSLSLLSLL"
CONV_WIDTH = 512
CONV_KERNEL = 31
CONV_HALO = 32
CONV_TILE_ROWS = 32
ODD_COLS = RWKV_COLS + 2 * CONV_WIDTH

LANE = 128
VMEM_LIMIT = 56 * 1024 * 1024
EVEN_BLOCK = 1024
ODD_BLOCK = 512
MLP_BLOCK = 1024


def _iota(shape, dim):
    return lax.broadcasted_iota(jnp.int32, shape, dim)


def _dot(a, b):
    return jnp.dot(a, b, preferred_element_type=F32)


def _dot_nt(a, b):
    return lax.dot_general(a, b, (((1,), (1,)), ((), ())), preferred_element_type=F32)


def _dot_tn(a, b):
    return lax.dot_general(a, b, (((0,), (0,)), ((), ())), preferred_element_type=F32)


def _split2(x):
    hi = x.astype(BF16)
    lo = (x - hi.astype(F32)).astype(BF16)
    return hi, lo


def _chunk_cumsum(tril2, x):
    hi, lo = _split2(x)
    cum = _dot(tril2, jnp.concatenate([hi, lo], axis=0))
    n = x.shape[0]
    return cum, jnp.broadcast_to(cum[n - 1:n], cum.shape)


def _head_sums(x, hm_half):
    half = hm_half.shape[0]
    xb = x.astype(BF16)
    return jnp.concatenate([_dot(xb[:, 0:half], hm_half), _dot(xb[:, half:2 * half], hm_half)], axis=1)


def _rms(x, g, eps):
    return x * lax.rsqrt(jnp.mean(x * x, axis=-1, keepdims=True) + eps) * g


def _layernorm(x, g, b, eps):
    mu = jnp.mean(x, axis=-1, keepdims=True)
    xc = x - mu
    var = jnp.mean(xc * xc, axis=-1, keepdims=True)
    return xc * lax.rsqrt(var + eps) * g + b


def _sigmoid(x):
    return 1.0 / (1.0 + jnp.exp(-x))


def _silu(x):
    return x * _sigmoid(x)


def _softplus(x):
    return jnp.maximum(x, 0.0) + jnp.log(1.0 + jnp.exp(-jnp.abs(x)))


def _gelu_tanh(x):
    c = 0.7978845608028654
    return 0.5 * x * (1.0 + jnp.tanh(c * (x + 0.044715 * (x * x * x))))


def _tril_ones(n, dtype):
    return jnp.where(_iota((n, n), 0) >= _iota((n, n), 1), 1.0, 0.0).astype(dtype)


def _even_kernel(x_ref, gn_ref, win_ref, wal_ref, bal_ref, glan_ref, lng_ref, lnb_ref,
                 sguw_ref, sgub_ref, wout_ref, o_ref,
                 p_ref, la_ref, bl_ref, dl_ref, qd_ref, kd_ref, ks_ref, vb_ref, oa_ref, ob_ref, st_ref,
                 *, tb):
    t_idx = pl.program_id(1)

    @pl.when(t_idx == 0)
    def _():
        st_ref[...] = jnp.zeros_like(st_ref)

    x = x_ref[...]
    h = _rms(x, gn_ref[...], RMS_EPS).astype(BF16)
    p_ref[...] = _dot(h, win_ref[...])

    alr = p_ref[:, 2560:2688]
    a_hi, a_lo = _split2(alr)
    w_hi, w_lo = _split2(wal_ref[...])
    z = _dot(a_hi, w_hi) + _dot(a_hi, w_lo) + _dot(a_lo, w_hi) + bal_ref[...]
    la = -_softplus(-z) * (1.0 / GLA_TAU)

    n_chunks = tb // GLA_CHUNK
    tril2 = jnp.concatenate([_tril_ones(GLA_CHUNK, BF16)] * 2, axis=1)
    for c in range(n_chunks):
        cum_c, last_c = _chunk_cumsum(tril2, la[c * GLA_CHUNK:(c + 1) * GLA_CHUNK])
        la_ref[c * GLA_CHUNK:(c + 1) * GLA_CHUNK, :] = cum_c
        bl_ref[c * GLA_CHUNK:(c + 1) * GLA_CHUNK, :] = last_c
    b = la_ref[...]
    b_last = bl_ref[...]
    for c in range(n_chunks):
        dl_ref[c] = jnp.exp(b_last[c * GLA_CHUNK:c * GLA_CHUNK + 8])
    k = p_ref[:, 256:512]
    qd_ref[...] = (p_ref[:, 0:256] * (GLA_DK ** -0.5) * jnp.exp(b)).astype(BF16)
    kd_ref[...] = (k * jnp.exp(-b)).astype(BF16)
    ks_ref[...] = (k * jnp.exp(b_last - b)).astype(BF16)
    vb_ref[...] = p_ref[:, 512:1024].astype(BF16)

    rk = _iota((GLA_KEY, GLA_KEY), 0) >> 6
    ck = _iota((GLA_KEY, GLA_KEY), 1) >> 6
    k4_mask = rk == ck
    rv = _iota((GLA_KEY, GLA_WIDTH), 0) >> 6
    cv = _iota((GLA_KEY, GLA_WIDTH), 1) >> 7
    vbd_mask = rv == cv
    causal = _iota((GLA_CHUNK, GLA_KEY), 0) >= (_iota((GLA_CHUNK, GLA_KEY), 1) & 63)
    rs = _iota((GLA_WIDTH, GLA_KEY), 0) >> 7
    cs = _iota((GLA_WIDTH, GLA_KEY), 1) >> 6
    st_mask = rs == cs

    zero_k = jnp.zeros((GLA_KEY, GLA_KEY), BF16)
    zero_v = jnp.zeros((GLA_KEY, GLA_WIDTH), BF16)

    def gla_step(ci_, carry):
        base = pl.multiple_of(ci_ * (GLA_LOCAL * GLA_CHUNK), GLA_LOCAL * GLA_CHUNK)
        ccs = range(GLA_LOCAL)
        rows = [pl.ds(base + cc * GLA_CHUNK, GLA_CHUNK) for cc in ccs]
        q_dec = [qd_ref[r, :] for r in rows]
        k4 = [jnp.where(k4_mask, jnp.concatenate([kd_ref[r, :]] * GLA_HEADS, axis=0), zero_k)
              for r in rows]
        v = [vb_ref[r, :] for r in rows]
        v_bd = [jnp.where(vbd_mask, jnp.concatenate([v[cc]] * GLA_HEADS, axis=0), zero_v) for cc in ccs]
        att = [_dot_nt(q_dec[cc], k4[cc]) for cc in ccs]
        kv = [_dot_tn(v[cc], ks_ref[rows[cc], :]) for cc in ccs]
        att = [jnp.where(causal, att[cc], 0.0).astype(BF16) for cc in ccs]
        o_intra = [_dot(att[cc], v_bd[cc]) for cc in ccs]
        st = st_ref[...]
        for cc in ccs:
            o = o_intra[cc] + _dot_nt(q_dec[cc], st.astype(BF16))
            dl = dl_ref[ci_ * GLA_LOCAL + cc][0:1, :]
            st = st * dl + jnp.where(st_mask, kv[cc], 0.0)
            for hh in range(GLA_HEADS):
                oh = o[:, hh * GLA_DV:(hh + 1) * GLA_DV]
                oh = oh * lax.rsqrt(jnp.mean(oh * oh, axis=-1, keepdims=True) + RMS_EPS)
                oa_ref[rows[cc], hh * GLA_DV:(hh + 1) * GLA_DV] = oh
        st_ref[...] = st
        return carry

    lax.fori_loop(0, n_chunks // GLA_LOCAL, gla_step, 0)

    g = p_ref[:, 1024:1536]
    o_a = oa_ref[...] * glan_ref[...] * _silu(g)

    sv = _layernorm(_gelu_tanh(p_ref[:, 2048:2560]), lng_ref[...], lnb_ref[...], LN_EPS)
    sv = sv.astype(BF16)
    tri128 = _iota((SGU_CHUNK, SGU_CHUNK), 0) >= _iota((SGU_CHUNK, SGU_CHUNK), 1)
    for gg in range(SGU_GROUPS):
        wg = jnp.where(tri128, sguw_ref[gg], 0.0).astype(BF16)
        for c in range(tb // SGU_CHUNK):
            r0 = c * SGU_CHUNK
            s = _dot(wg, sv[r0:r0 + SGU_CHUNK, gg * LANE:(gg + 1) * LANE])
            s = s + sgub_ref[:, gg * LANE:(gg + 1) * LANE]
            u = _gelu_tanh(p_ref[r0:r0 + SGU_CHUNK, 1536 + gg * LANE:1536 + (gg + 1) * LANE])
            ob_ref[r0:r0 + SGU_CHUNK, gg * LANE:(gg + 1) * LANE] = u * s

    m = _dot(o_a.astype(BF16), wout_ref[0:512, :]) + _dot(ob_ref[...].astype(BF16), wout_ref[512:1024, :])
    o_ref[...] = x + m


def _even_layer(x, norm_g, w_in, w_out, w_alpha2, b_alpha, gla_norm, ln_g, ln_b, sgu_w, sgu_b, *, tb):
    B, T, D = x.shape
    w_in_r = jnp.concatenate(
        [w_in[:, :1536], w_in[:, 1552:2576], w_in[:, 1536:1552],
         jnp.zeros((D, EVEN_COLS - 2576), w_in.dtype)], axis=1).astype(BF16)
    wal = jnp.zeros((LANE, GLA_KEY), F32).at[:GLA_GATE_RANK].set(w_alpha2.astype(F32))
    sgub = jnp.repeat(sgu_b.astype(F32).T, LANE, axis=1)
    row = lambda a: a.astype(F32).reshape(1, -1)
    const = lambda shape: pl.BlockSpec(shape, lambda b, t: (0,) * len(shape),
                                       pipeline_mode=pl.Buffered(1))
    kern = functools.partial(_even_kernel, tb=tb)
    return pl.pallas_call(
        kern,
        grid=(B, T // tb),
        in_specs=[
            pl.BlockSpec((None, tb, D), lambda b, t: (b, t, 0)),
            const((1, D)), const((D, EVEN_COLS)), const((LANE, GLA_KEY)), const((1, GLA_KEY)),
            const((1, GLA_WIDTH)), const((1, SGU_WIDTH)), const((1, SGU_WIDTH)),
            const((SGU_GROUPS, SGU_CHUNK, SGU_CHUNK)), const((SGU_CHUNK, SGU_WIDTH)),
            const((D, D)),
        ],
        out_specs=pl.BlockSpec((None, tb, D), lambda b, t: (b, t, 0)),
        out_shape=jax.ShapeDtypeStruct((B, T, D), F32),
        scratch_shapes=[
            pltpu.VMEM((tb, EVEN_COLS), F32),
            pltpu.VMEM((tb, GLA_KEY), F32),
            pltpu.VMEM((tb, GLA_KEY), F32),
            pltpu.VMEM((tb // GLA_CHUNK, 8, GLA_KEY), F32),
            pltpu.VMEM((tb, GLA_KEY), BF16), pltpu.VMEM((tb, GLA_KEY), BF16),
            pltpu.VMEM((tb, GLA_KEY), BF16),
            pltpu.VMEM((tb, GLA_WIDTH), BF16),
            pltpu.VMEM((tb, GLA_WIDTH), F32),
            pltpu.VMEM((tb, SGU_WIDTH), F32),
            pltpu.VMEM((GLA_WIDTH, GLA_KEY), F32),
        ],
        compiler_params=pltpu.CompilerParams(
            dimension_semantics=("arbitrary", "arbitrary"), vmem_limit_bytes=VMEM_LIMIT),
        name="even_mixer",
    )(x, row(norm_g), w_in_r, wal, row(b_alpha), row(gla_norm), row(ln_g), row(ln_b),
      sgu_w.astype(F32), sgub, w_out.astype(BF16))


def _odd_kernel(x_ref, gn_ref, win_ref, mu_ref, wlr_ref, w0_ref, a0_ref, kk_ref, ka_ref, rk_ref,
                gng_ref, gnb_ref, cw_ref, cb_ref, clg_ref, clb_ref, wout_ref, o_ref,
                prev_ref, zc_ref, cv_s, cum_s, last_s, y_s, rt_s, pt_s, qt_s, kt_s, qh_s, kh_s, v_s,
                z_s, gl_s, ark_s, uw_s, *, tb):
    t_idx = pl.program_id(1)
    gw = RWKV_GROUP * RWKV_HEAD
    n_groups = RWKV_WIDTH // gw
    n_chunks = tb // RWKV_CHUNK

    @pl.when(t_idx == 0)
    def _():
        prev_ref[...] = jnp.zeros_like(prev_ref)
        zc_ref[0:CONV_HALO, :] = jnp.zeros((CONV_HALO, CONV_WIDTH), F32)
        zc_ref[tb + CONV_HALO:tb + CONV_HALO + 8, :] = jnp.zeros((8, CONV_WIDTH), F32)
        z_s[...] = jnp.zeros_like(z_s)

    @pl.when(t_idx != 0)
    def _():
        zc_ref[0:CONV_HALO, :] = zc_ref[tb:tb + CONV_HALO, :]

    x = x_ref[...]
    h = _rms(x, gn_ref[...], RMS_EPS).astype(BF16)
    c_r, c_k, c_v = slice(0, 512), slice(512, 1024), slice(1024, 1536)
    c_lr = slice(3 * RWKV_WIDTH, RWKV_COLS)
    c_conv = slice(RWKV_COLS, ODD_COLS)

    def project(cols):
        return _dot(h, win_ref[:, cols])

    def shift_mix(pr, cols):
        sh = pltpu.roll(pr, 1, 0)
        head = jnp.where(_iota((8, pr.shape[1]), 0) == 0, prev_ref[7:8, cols], sh[0:8])
        prev_ref[:, cols] = pr[tb - 8:tb]
        sh = jnp.concatenate([head, sh[8:]], axis=0)
        return pr + (sh - pr) * mu_ref[:, cols]

    never = jnp.full((CONV_TILE_ROWS, LANE), t_idx, jnp.int32) < 0

    def conv_tile(r0, n, lanes, after):
        first = CONV_HALO - (CONV_KERNEL - 1)
        wn = n + CONV_HALO + 8
        win = zc_ref[pl.ds(r0, wn), lanes]
        acc = jnp.where(never, jnp.concatenate([after] * (n // 8), axis=0),
                        jnp.zeros((n, LANE), F32) + cb_ref[:, lanes])
        for ph in range(8):
            sh = win if ph == 0 else pltpu.roll(win, wn - ph, 0)
            for j in range(CONV_KERNEL):
                if (j + first) % 8 == ph:
                    a8 = (j + first) // 8 * 8
                    acc = acc + sh[a8:a8 + n] * cw_ref[j:j + 1, lanes]
        return acc

    hm = jnp.where((_iota((256, 256), 0) >> 6) == (_iota((256, 256), 1) >> 6),
                   1.0, 0.0).astype(BF16)
    p_lr = project(c_lr)
    p_k = project(c_k)
    lr = shift_mix(p_lr, c_lr)
    lane = _iota((tb, 256), 1)
    f = jnp.where(lane < RWKV_DECAY_RANK, jnp.tanh(lr),
                  jnp.where(lane < RWKV_DECAY_RANK + RWKV_AAA_RANK, lr, _sigmoid(lr)))
    p_r = project(c_r)
    lo3 = _dot(f.astype(BF16), wlr_ref[...])
    k = shift_mix(p_k, c_k)
    kk = k * kk_ref[...]
    kk = kk * lax.rsqrt(jnp.maximum(_head_sums(kk * kk, hm), 1e-24))
    p_v = project(c_v)
    r = shift_mix(p_r, c_r)
    lw = (-0.6065306597126334) * _sigmoid(w0_ref[...] + lo3[:, 0:512])
    a = _sigmoid(a0_ref[...] + lo3[:, 512:1024])
    gate = lo3[:, 1024:1536]
    k2 = k * (1.0 + (a - 1.0) * ka_ref[...])
    p_conv = project(c_conv)
    v = shift_mix(p_v, c_v)
    bonus = _head_sums(r * k2 * rk_ref[...], hm) * v

    tril2 = jnp.concatenate([_tril_ones(RWKV_CHUNK, BF16)] * 2, axis=1)
    for c in range(n_chunks):
        cum_c, last_c = _chunk_cumsum(tril2, lw[c * RWKV_CHUNK:(c + 1) * RWKV_CHUNK])
        cum_s[c * RWKV_CHUNK:(c + 1) * RWKV_CHUNK, :] = cum_c
        last_s[c * RWKV_CHUNK:(c + 1) * RWKV_CHUNK, :] = last_c
    cum = cum_s[...]
    e_last = jnp.exp(last_s[...])
    for c in range(n_chunks):
        gl_s[c] = e_last[c * RWKV_CHUNK:c * RWKV_CHUNK + 8]
    e_neg = jnp.exp(-cum)
    e_hat = e_last * e_neg
    qa = kk * a
    rt_s[...] = (r * jnp.exp(cum)).astype(BF16)
    pt_s[...] = (-kk * jnp.exp(cum - lw)).astype(BF16)
    qt_s[...] = (qa * e_neg).astype(BF16)
    kt_s[...] = (k2 * e_neg).astype(BF16)
    qh_s[...] = (qa * e_hat).astype(BF16)
    kh_s[...] = (k2 * e_hat).astype(BF16)
    v_s[...] = v.astype(BF16)

    zc_ref[CONV_HALO:CONV_HALO + tb, :] = p_conv[:, 0:CONV_WIDTH] * _sigmoid(
        p_conv[:, CONV_WIDTH:2 * CONV_WIDTH])

    ri = _iota((gw, gw), 0)
    ci = _iota((gw, gw), 1)
    same = (ri >> 6) == (ci >> 6)
    strict = same & (ri > ci)
    ri2 = _iota((gw, 2 * gw), 0)
    ci2 = _iota((gw, 2 * gw), 1) & (gw - 1)
    incl2 = ((ri2 >> 6) == (ci2 >> 6)) & (ri2 >= ci2)
    eye = jnp.where(ri == ci, 1.0, 0.0)

    zero_bf = jnp.zeros((gw, gw), BF16)

    def bd(xs):
        return jnp.where(same, jnp.concatenate([xs] * RWKV_GROUP, axis=0), zero_bf)

    def aligned(i, m):
        return i if isinstance(i, int) else pl.multiple_of(i, m)

    def local_stages(ci_):
        probs = [(cc, j) for cc in range(LOCAL_CHUNKS) for j in range(n_groups)]
        idx = range(len(probs))
        base = aligned(ci_ * (LOCAL_CHUNKS * RWKV_CHUNK), LOCAL_CHUNKS * RWKV_CHUNK)

        def blk(ref, cc, j):
            return bd(ref[pl.ds(base + cc * RWKV_CHUNK, RWKV_CHUNK), j * gw:(j + 1) * gw])

        def slot(cc, j):
            return (ci_ * LOCAL_CHUNKS + cc) * n_groups + j

        P2 = [blk(pt_s, cc, j) for cc, j in probs]
        PR = [jnp.concatenate([P2[i], blk(rt_s, cc, j)], axis=0) for i, (cc, j) in enumerate(probs)]
        QK = [jnp.concatenate([blk(qt_s, cc, j), blk(kt_s, cc, j)], axis=0) for cc, j in probs]
        V2 = [blk(v_s, cc, j) for cc, j in probs]
        sc = [_dot_nt(PR[i], QK[i]) for i in idx]
        yield sc[-1][0:8, 0:LANE]
        A_pq = [jnp.where(strict, sc[i][0:gw, 0:gw], 0.0) for i in idx]
        A_pk = [jnp.where(strict, sc[i][0:gw, gw:2 * gw], 0.0).astype(BF16) for i in idx]
        for i, (cc, j) in enumerate(probs):
            ark_s[slot(cc, j)] = jnp.where(incl2, sc[i][gw:2 * gw, :], 0.0).astype(BF16)
        Xb = [A_pq[i].astype(BF16) for i in idx]
        Xp = [_dot(Xb[i], Xb[i]) for i in idx]
        av = [_dot(A_pk[i], V2[i]) for i in idx]
        yield av[-1][0:8, 0:LANE]
        av = [av[i].astype(BF16) for i in idx]
        Tm = [eye + A_pq[i] for i in idx]
        for step in range(4):
            Xb = [Xp[i].astype(BF16) for i in idx]
            XT = [_dot(Xb[i], jnp.concatenate([Xb[i], Tm[i].astype(BF16)], axis=1)) for i in idx]
            yield XT[-1][0:8, 0:LANE]
            Xp = [XT[i][:, 0:gw] for i in idx]
            Tm = [Tm[i] + XT[i][:, gw:2 * gw] for i in idx]
        XT = [_dot(Xp[i].astype(BF16), Tm[i].astype(BF16)) for i in idx]
        yield XT[-1][0:8, 0:LANE]
        Tm = [(Tm[i] + XT[i]).astype(BF16) for i in idx]
        uw = [_dot(Tm[i], jnp.concatenate([av[i], P2[i]], axis=1)) for i in idx]
        for i, (cc, j) in enumerate(probs):
            uw_s[slot(cc, j)] = uw[i].astype(BF16)
        yield uw[-1][0:8, 0:LANE]

    def state_stages(ci_):
        groups = range(n_groups)
        sls = [slice(j * gw, (j + 1) * gw) for j in groups]
        for cc in range(LOCAL_CHUNKS):
            c = ci_ * LOCAL_CHUNKS + cc
            rows = pl.ds(aligned(c * RWKV_CHUNK, RWKV_CHUNK), RWKV_CHUNK)
            e_last = gl_s[c][0:1, :]
            R2 = [bd(rt_s[rows, s]) for s in sls]
            uw = [uw_s[c * n_groups + j] for j in groups]
            Z = [z_s[j] for j in groups]
            hw = [_dot_nt(jnp.concatenate([uw[j][:, gw:2 * gw], R2[j]], axis=0), Z[j].astype(BF16))
                  for j in groups]
            yield
            QK = [jnp.concatenate([bd(qh_s[rows, s]), bd(kh_s[rows, s])], axis=0) for s in sls]
            V2 = [bd(v_s[rows, s]) for s in sls]
            UV = [jnp.concatenate([(uw[j][:, 0:gw].astype(F32) + hw[j][0:gw]).astype(BF16), V2[j]],
                                  axis=0) for j in groups]
            zn = [_dot_tn(UV[j], QK[j]) for j in groups]
            Y2 = [hw[j][gw:2 * gw] + _dot(ark_s[c * n_groups + j], UV[j]) for j in groups]
            for j in groups:
                z_s[j] = Z[j] * e_last[:, sls[j]] + zn[j]
                y = Y2[j][0:RWKV_CHUNK]
                for e in range(1, RWKV_GROUP):
                    y = y + Y2[j][e * RWKV_CHUNK:(e + 1) * RWKV_CHUNK]
                y_s[rows, sls[j]] = y
            yield

    lane_blocks = CONV_WIDTH // LANE

    conv_tiles = [(r, lb) for r in range(0, LOCAL_CHUNKS * RWKV_CHUNK, CONV_TILE_ROWS)
                  for lb in range(lane_blocks)]
    tiles_per_stage = len(conv_tiles) // LOCAL_STAGES

    def conv_after(ci_, k, after):
        for r, lb in conv_tiles[k * tiles_per_stage:(k + 1) * tiles_per_stage]:
            lanes = slice(lb * LANE, (lb + 1) * LANE)
            r0 = aligned(ci_ * (LOCAL_CHUNKS * RWKV_CHUNK) + r, CONV_TILE_ROWS)
            cv_s[pl.ds(r0, CONV_TILE_ROWS), lanes] = conv_tile(r0, CONV_TILE_ROWS, lanes, after)

    n_groups_t = n_chunks // LOCAL_CHUNKS
    for k, after in enumerate(local_stages(0)):
        conv_after(0, k, after)

    def pipelined(i, carry):
        g_local, g_state = local_stages(i), state_stages(i - 1)
        k = 0
        for which in PIPELINE_ORDER:
            if which == "L":
                conv_after(i, k, next(g_local))
                k += 1
            else:
                next(g_state)
        return carry

    lax.fori_loop(1, n_groups_t, pipelined, 0)
    for _ in state_stages(n_groups_t - 1):
        pass

    o_d = _silu(_layernorm(cv_s[...], clg_ref[...], clb_ref[...], LN_EPS))

    y = y_s[...]
    inv_n = 1.0 / RWKV_HEAD
    mu_y = _head_sums(y, hm) * inv_n
    yc = y - mu_y
    var_y = _head_sums(yc * yc, hm) * inv_n
    y = yc * lax.rsqrt(var_y + RWKV_GN_EPS) * gng_ref[...] + gnb_ref[...] + bonus
    o_c = y * gate

    m = _dot(o_c.astype(BF16), wout_ref[0:512, :]) + _dot(o_d.astype(BF16), wout_ref[512:1024, :])
    o_ref[...] = x + m


def _odd_layer(x, norm_g, w_in, w_out, mu, w0, w2, a0, a2, g2, k_k, k_a, r_k, gn_g, gn_b,
               conv_w, conv_b, cln_g, cln_b, *, tb):
    B, T, D = x.shape
    pad_lr = 256 - (RWKV_DECAY_RANK + RWKV_AAA_RANK + RWKV_GATE_RANK)
    w_in_r = jnp.concatenate(
        [w_in[:, :RWKV_IN], jnp.zeros((D, pad_lr), w_in.dtype), w_in[:, RWKV_IN:]], axis=1).astype(BF16)
    mu_r = jnp.concatenate([mu.astype(F32), jnp.zeros((pad_lr,), F32)]).reshape(1, RWKV_COLS)
    wlr = jnp.zeros((256, 3 * RWKV_WIDTH), F32)
    wlr = wlr.at[0:32, 0:512].set(w2.astype(F32))
    wlr = wlr.at[32:64, 512:1024].set(a2.astype(F32))
    wlr = wlr.at[64:160, 1024:1536].set(g2.astype(F32))
    row = lambda a: a.astype(F32).reshape(1, -1)
    const = lambda shape: pl.BlockSpec(shape, lambda b, t: (0,) * len(shape),
                                       pipeline_mode=pl.Buffered(1))
    gw = RWKV_GROUP * RWKV_HEAD
    kern = functools.partial(_odd_kernel, tb=tb)
    return pl.pallas_call(
        kern,
        grid=(B, T // tb),
        in_specs=[
            pl.BlockSpec((None, tb, D), lambda b, t: (b, t, 0)),
            const((1, D)), const((D, ODD_COLS)), const((1, RWKV_COLS)), const((256, 3 * RWKV_WIDTH)),
            const((1, 512)), const((1, 512)), const((1, 512)), const((1, 512)), const((1, 512)),
            const((1, 512)), const((1, 512)),
            const((CONV_KERNEL, CONV_WIDTH)), const((1, 512)), const((1, 512)), const((1, 512)),
            const((D, D)),
        ],
        out_specs=pl.BlockSpec((None, tb, D), lambda b, t: (b, t, 0)),
        out_shape=jax.ShapeDtypeStruct((B, T, D), F32),
        scratch_shapes=[
            pltpu.VMEM((8, RWKV_COLS), F32),
            pltpu.VMEM((tb + CONV_HALO + 8, CONV_WIDTH), F32),
            pltpu.VMEM((tb, 512), F32),
            pltpu.VMEM((tb, 512), F32), pltpu.VMEM((tb, 512), F32),
            pltpu.VMEM((tb, 512), F32),
            pltpu.VMEM((tb, 512), BF16), pltpu.VMEM((tb, 512), BF16), pltpu.VMEM((tb, 512), BF16),
            pltpu.VMEM((tb, 512), BF16), pltpu.VMEM((tb, 512), BF16), pltpu.VMEM((tb, 512), BF16),
            pltpu.VMEM((tb, 512), BF16),
            pltpu.VMEM((RWKV_WIDTH // gw, gw, gw), F32),
            pltpu.VMEM((tb // RWKV_CHUNK, 8, RWKV_WIDTH), F32),
            pltpu.VMEM((tb // RWKV_CHUNK * (RWKV_WIDTH // gw), gw, 2 * gw), BF16),
            pltpu.VMEM((tb // RWKV_CHUNK * (RWKV_WIDTH // gw), gw, 2 * gw), BF16),
        ],
        compiler_params=pltpu.CompilerParams(
            dimension_semantics=("arbitrary", "arbitrary"), vmem_limit_bytes=VMEM_LIMIT),
        name="odd_mixer",
    )(x, row(norm_g), w_in_r, mu_r, wlr.astype(BF16), row(w0), row(a0), row(k_k), row(k_a), row(r_k),
      row(gn_g), row(gn_b), conv_w.astype(F32), row(conv_b), row(cln_g), row(cln_b), w_out.astype(BF16))


def _mlp_kernel(x_ref, g_ref, wup_ref, wdn_ref, gf_ref, o_ref, *, ff_chunk, final_norm):
    x = x_ref[...]
    xb = _rms(x, g_ref[...], RMS_EPS).astype(BF16)
    acc = x
    d_ff = wup_ref.shape[1]
    for f0 in range(0, d_ff, ff_chunk):
        hcol = _dot(xb, wup_ref[:, f0:f0 + ff_chunk])
        hcol = jnp.square(jnp.maximum(hcol, 0.0))
        acc = acc + _dot(hcol.astype(BF16), wdn_ref[f0:f0 + ff_chunk, :])
    if final_norm:
        acc = _rms(acc, gf_ref[...], RMS_EPS)
    o_ref[...] = acc


def _mlp_layer(x2, norm_g, w_up, w_down, final_g, *, tm, final_norm):
    M, D = x2.shape
    d_ff = w_up.shape[1]
    kern = functools.partial(_mlp_kernel, ff_chunk=1024, final_norm=final_norm)
    return pl.pallas_call(
        kern,
        grid=(M // tm,),
        in_specs=[
            pl.BlockSpec((tm, D), lambda i: (i, 0)),
            pl.BlockSpec((1, D), lambda i: (0, 0)),
            pl.BlockSpec((D, d_ff), lambda i: (0, 0), pipeline_mode=pl.Buffered(1)),
            pl.BlockSpec((d_ff, D), lambda i: (0, 0), pipeline_mode=pl.Buffered(1)),
            pl.BlockSpec((1, D), lambda i: (0, 0)),
        ],
        out_specs=pl.BlockSpec((tm, D), lambda i: (i, 0)),
        out_shape=jax.ShapeDtypeStruct((M, D), F32),
        compiler_params=pltpu.CompilerParams(
            dimension_semantics=("arbitrary",), vmem_limit_bytes=VMEM_LIMIT),
        name="mlp_final" if final_norm else "mlp",
    )(x2, norm_g.astype(F32).reshape(1, D), w_up.astype(BF16), w_down.astype(BF16),
      final_g.astype(F32).reshape(1, D))


def _pick_block(n, target):
    b = min(n, target)
    while n % b:
        b //= 2
    return b


def kernel(x, norm_mix, norm_ffn, w_up, w_down, norm_final, even_w_in, even_w_out, gla_w_alpha2, gla_b_alpha, gla_norm, sgu_ln_g, sgu_ln_b, sgu_w, sgu_b, odd_w_in, odd_w_out, rwkv_mu, rwkv_w0, rwkv_w2, rwkv_a0, rwkv_a2, rwkv_g2, rwkv_k_k, rwkv_k_a, rwkv_r_k, rwkv_gn_g, rwkv_gn_b, conv_w, conv_b, conv_ln_g, conv_ln_b):
    B, T, D = x.shape
    depth = norm_mix.shape[0]
    tb_even = _pick_block(T, EVEN_BLOCK)
    tb_odd = _pick_block(T, ODD_BLOCK)
    tm = _pick_block(B * T, MLP_BLOCK)
    assert tb_even % (GLA_LOCAL * GLA_CHUNK) == 0 and tb_even % SGU_CHUNK == 0
    assert tb_odd % (LOCAL_CHUNKS * RWKV_CHUNK) == 0 and D == 1024
    for layer in range(depth):
        j = layer // 2
        if layer % 2 == 0:
            x = _even_layer(x, norm_mix[layer], even_w_in[j], even_w_out[j], gla_w_alpha2[j],
                            gla_b_alpha[j], gla_norm[j], sgu_ln_g[j], sgu_ln_b[j], sgu_w[j], sgu_b[j],
                            tb=tb_even)
        else:
            x = _odd_layer(x, norm_mix[layer], odd_w_in[j], odd_w_out[j], rwkv_mu[j], rwkv_w0[j],
                           rwkv_w2[j], rwkv_a0[j], rwkv_a2[j], rwkv_g2[j], rwkv_k_k[j], rwkv_k_a[j],
                           rwkv_r_k[j].reshape(-1), rwkv_gn_g[j], rwkv_gn_b[j], conv_w[j], conv_b[j],
                           conv_ln_g[j], conv_ln_b[j], tb=tb_odd)
        last = layer == depth - 1
        x = _mlp_layer(x.reshape(B * T, D), norm_ffn[layer], w_up[layer], w_down[layer], norm_final,
                       tm=tm, final_norm=last).reshape(B, T, D)
    return x
```

```python
import functools

import jax
import jax.numpy as jnp
from jax import lax
from jax.experimental import pallas as pl
from jax.experimental.pallas import tpu as pltpu

F32 = jnp.float32
BF16 = jnp.bfloat16

RMS_EPS = 1e-6
LN_EPS = 1e-5

GLA_HEADS = 4
GLA_DK = 64
GLA_DV = 128
GLA_KEY = GLA_HEADS * GLA_DK
GLA_WIDTH = GLA_HEADS * GLA_DV
GLA_GATE_RANK = 16
GLA_TAU = 16.0
GLA_CHUNK = 64
GLA_LOCAL = 4
SGU_WIDTH = 512
SGU_GROUPS = 4
SGU_CHUNK = 128
EVEN_COLS = 2688

RWKV_WIDTH = 512
RWKV_HEAD = 64
RWKV_HEADS = 8
RWKV_DECAY_RANK = 32
RWKV_AAA_RANK = 32
RWKV_GATE_RANK = 96
RWKV_GN_EPS = 64e-5
RWKV_IN = 1696
RWKV_COLS = 1792
RWKV_CHUNK = 64
RWKV_GROUP = 2
LOCAL_CHUNKS = 2
LOCAL_STAGES = 8
PIPELINE_ORDER = "LSLLSLSLLSLL"
CONV_WIDTH = 512
CONV_KERNEL = 31
CONV_HALO = 32
CONV_TILE_ROWS = 32
ODD_COLS = RWKV_COLS + 2 * CONV_WIDTH

LANE = 128
VMEM_LIMIT = 56 * 1024 * 1024
EVEN_BLOCK = 1024
ODD_BLOCK = 512
MLP_BLOCK = 1024


def _iota(shape, dim):
    return lax.broadcasted_iota(jnp.int32, shape, dim)


def _dot(a, b):
    return jnp.dot(a, b, preferred_element_type=F32)


def _dot_nt(a, b):
    return lax.dot_general(a, b, (((1,), (1,)), ((), ())), preferred_element_type=F32)


def _dot_tn(a, b):
    return lax.dot_general(a, b, (((0,), (0,)), ((), ())), preferred_element_type=F32)


def _split2(x):
    hi = x.astype(BF16)
    lo = (x - hi.astype(F32)).astype(BF16)
    return hi, lo


def _chunk_cumsum(tril2, x):
    hi, lo = _split2(x)
    cum = _dot(tril2, jnp.concatenate([hi, lo], axis=0))
    n = x.shape[0]
    return cum, jnp.broadcast_to(cum[n - 1:n], cum.shape)


def _head_sums(x, hm_half):
    half = hm_half.shape[0]
    xb = x.astype(BF16)
    return jnp.concatenate([_dot(xb[:, 0:half], hm_half), _dot(xb[:, half:2 * half], hm_half)], axis=1)


def _rms(x, g, eps):
    return x * lax.rsqrt(jnp.mean(x * x, axis=-1, keepdims=True) + eps) * g


def _layernorm(x, g, b, eps):
    mu = jnp.mean(x, axis=-1, keepdims=True)
    xc = x - mu
    var = jnp.mean(xc * xc, axis=-1, keepdims=True)
    return xc * lax.rsqrt(var + eps) * g + b


def _sigmoid(x):
    return 1.0 / (1.0 + jnp.exp(-x))


def _silu(x):
    return x * _sigmoid(x)


def _softplus(x):
    return jnp.maximum(x, 0.0) + jnp.log(1.0 + jnp.exp(-jnp.abs(x)))


def _gelu_tanh(x):
    c = 0.7978845608028654
    return 0.5 * x * (1.0 + jnp.tanh(c * (x + 0.044715 * (x * x * x))))


def _tril_ones(n, dtype):
    return jnp.where(_iota((n, n), 0) >= _iota((n, n), 1), 1.0, 0.0).astype(dtype)


def _even_kernel(x_ref, gn_ref, win_ref, wal_ref, bal_ref, glan_ref, lng_ref, lnb_ref,
                 sguw_ref, sgub_ref, wout_ref, o_ref,
                 la_ref, bl_ref, dl_ref, qd_ref, kd_ref, ks_ref, vb_ref, gt_ref, oa_ref, ob_ref, st_ref,
                 *, tb):
    t_idx = pl.program_id(1)

    @pl.when(t_idx == 0)
    def _():
        st_ref[...] = jnp.zeros_like(st_ref)

    x = x_ref[...]
    h = _rms(x, gn_ref[...], RMS_EPS).astype(BF16)

    def project(c0, c1):
        return _dot(h, win_ref[:, c0:c1])

    p_alr = project(2560, 2688)
    p_sv = project(2048, 2560)
    a_hi, a_lo = _split2(p_alr)
    w_hi, w_lo = _split2(wal_ref[...])
    z = _dot(a_hi, w_hi) + _dot(a_hi, w_lo) + _dot(a_lo, w_hi) + bal_ref[...]
    la = -_softplus(-z) * (1.0 / GLA_TAU)
    p_u = project(1536, 2048)
    sv = _layernorm(_gelu_tanh(p_sv), lng_ref[...], lnb_ref[...], LN_EPS).astype(BF16)

    n_chunks = tb // GLA_CHUNK
    tril2 = jnp.concatenate([_tril_ones(GLA_CHUNK, BF16)] * 2, axis=1)
    for c in range(n_chunks):
        cum_c, last_c = _chunk_cumsum(tril2, la[c * GLA_CHUNK:(c + 1) * GLA_CHUNK])
        la_ref[c * GLA_CHUNK:(c + 1) * GLA_CHUNK, :] = cum_c
        bl_ref[c * GLA_CHUNK:(c + 1) * GLA_CHUNK, :] = last_c
    p_qk = project(0, 512)
    ug = _gelu_tanh(p_u)

    tri128 = _iota((SGU_CHUNK, SGU_CHUNK), 0) >= _iota((SGU_CHUNK, SGU_CHUNK), 1)
    for gg in range(SGU_GROUPS):
        wg = jnp.where(tri128, sguw_ref[gg], 0.0).astype(BF16)
        for c in range(tb // SGU_CHUNK):
            r0 = c * SGU_CHUNK
            s = _dot(wg, sv[r0:r0 + SGU_CHUNK, gg * LANE:(gg + 1) * LANE])
            s = s + sgub_ref[:, gg * LANE:(gg + 1) * LANE]
            ob_ref[r0:r0 + SGU_CHUNK, gg * LANE:(gg + 1) * LANE] = (
                ug[r0:r0 + SGU_CHUNK, gg * LANE:(gg + 1) * LANE] * s)

    p_v = project(512, 1024)
    b = la_ref[...]
    b_last = bl_ref[...]
    for c in range(n_chunks):
        dl_ref[c] = jnp.exp(b_last[c * GLA_CHUNK:c * GLA_CHUNK + 8])
    k = p_qk[:, 256:512]
    qd_ref[...] = (p_qk[:, 0:256] * (GLA_DK ** -0.5) * jnp.exp(b)).astype(BF16)
    kd_ref[...] = (k * jnp.exp(-b)).astype(BF16)
    ks_ref[...] = (k * jnp.exp(b_last - b)).astype(BF16)
    p_g = project(1024, 1536)
    vb_ref[...] = p_v.astype(BF16)
    gt_ref[...] = glan_ref[...] * _silu(p_g)

    rk = _iota((GLA_KEY, GLA_KEY), 0) >> 6
    ck = _iota((GLA_KEY, GLA_KEY), 1) >> 6
    k4_mask = rk == ck
    rv = _iota((GLA_KEY, GLA_WIDTH), 0) >> 6
    cv = _iota((GLA_KEY, GLA_WIDTH), 1) >> 7
    vbd_mask = rv == cv
    causal = _iota((GLA_CHUNK, GLA_KEY), 0) >= (_iota((GLA_CHUNK, GLA_KEY), 1) & 63)
    rs = _iota((GLA_WIDTH, GLA_KEY), 0) >> 7
    cs = _iota((GLA_WIDTH, GLA_KEY), 1) >> 6
    st_mask = rs == cs

    zero_k = jnp.zeros((GLA_KEY, GLA_KEY), BF16)
    zero_v = jnp.zeros((GLA_KEY, GLA_WIDTH), BF16)

    def gla_step(ci_, carry):
        base = pl.multiple_of(ci_ * (GLA_LOCAL * GLA_CHUNK), GLA_LOCAL * GLA_CHUNK)
        ccs = range(GLA_LOCAL)
        rows = [pl.ds(base + cc * GLA_CHUNK, GLA_CHUNK) for cc in ccs]
        q_dec = [qd_ref[r, :] for r in rows]
        k4 = [jnp.where(k4_mask, jnp.concatenate([kd_ref[r, :]] * GLA_HEADS, axis=0), zero_k)
              for r in rows]
        v = [vb_ref[r, :] for r in rows]
        v_bd = [jnp.where(vbd_mask, jnp.concatenate([v[cc]] * GLA_HEADS, axis=0), zero_v) for cc in ccs]
        att = [_dot_nt(q_dec[cc], k4[cc]) for cc in ccs]
        kv = [_dot_tn(v[cc], ks_ref[rows[cc], :]) for cc in ccs]
        att = [jnp.where(causal, att[cc], 0.0).astype(BF16) for cc in ccs]
        o_intra = [_dot(att[cc], v_bd[cc]) for cc in ccs]
        st = st_ref[...]
        for cc in ccs:
            o = o_intra[cc] + _dot_nt(q_dec[cc], st.astype(BF16))
            dl = dl_ref[ci_ * GLA_LOCAL + cc][0:1, :]
            st = st * dl + jnp.where(st_mask, kv[cc], 0.0)
            for hh in range(GLA_HEADS):
                oh = o[:, hh * GLA_DV:(hh + 1) * GLA_DV]
                oh = oh * lax.rsqrt(jnp.mean(oh * oh, axis=-1, keepdims=True) + RMS_EPS)
                oa_ref[rows[cc], hh * GLA_DV:(hh + 1) * GLA_DV] = oh
        st_ref[...] = st
        return carry

    lax.fori_loop(0, n_chunks // GLA_LOCAL, gla_step, 0)

    o_a = oa_ref[...] * gt_ref[...]
    m = _dot(o_a.astype(BF16), wout_ref[0:512, :]) + _dot(ob_ref[...].astype(BF16), wout_ref[512:1024, :])
    o_ref[...] = x + m


def _even_layer(x, norm_g, w_in, w_out, w_alpha2, b_alpha, gla_norm, ln_g, ln_b, sgu_w, sgu_b, *, tb):
    B, T, D = x.shape
    w_in_r = jnp.concatenate(
        [w_in[:, :1536], w_in[:, 1552:2576], w_in[:, 1536:1552],
         jnp.zeros((D, EVEN_COLS - 2576), w_in.dtype)], axis=1).astype(BF16)
    wal = jnp.zeros((LANE, GLA_KEY), F32).at[:GLA_GATE_RANK].set(w_alpha2.astype(F32))
    sgub = jnp.repeat(sgu_b.astype(F32).T, LANE, axis=1)
    row = lambda a: a.astype(F32).reshape(1, -1)
    const = lambda shape: pl.BlockSpec(shape, lambda b, t: (0,) * len(shape),
                                       pipeline_mode=pl.Buffered(1))
    kern = functools.partial(_even_kernel, tb=tb)
    return pl.pallas_call(
        kern,
        grid=(B, T // tb),
        in_specs=[
            pl.BlockSpec((None, tb, D), lambda b, t: (b, t, 0)),
            const((1, D)), const((D, EVEN_COLS)), const((LANE, GLA_KEY)), const((1, GLA_KEY)),
            const((1, GLA_WIDTH)), const((1, SGU_WIDTH)), const((1, SGU_WIDTH)),
            const((SGU_GROUPS, SGU_CHUNK, SGU_CHUNK)), const((SGU_CHUNK, SGU_WIDTH)),
            const((D, D)),
        ],
        out_specs=pl.BlockSpec((None, tb, D), lambda b, t: (b, t, 0)),
        out_shape=jax.ShapeDtypeStruct((B, T, D), F32),
        scratch_shapes=[
            pltpu.VMEM((tb, GLA_KEY), F32),
            pltpu.VMEM((tb, GLA_KEY), F32),
            pltpu.VMEM((tb // GLA_CHUNK, 8, GLA_KEY), F32),
            pltpu.VMEM((tb, GLA_KEY), BF16), pltpu.VMEM((tb, GLA_KEY), BF16),
            pltpu.VMEM((tb, GLA_KEY), BF16),
            pltpu.VMEM((tb, GLA_WIDTH), BF16),
            pltpu.VMEM((tb, GLA_WIDTH), F32),
            pltpu.VMEM((tb, GLA_WIDTH), F32),
            pltpu.VMEM((tb, SGU_WIDTH), F32),
            pltpu.VMEM((GLA_WIDTH, GLA_KEY), F32),
        ],
        compiler_params=pltpu.CompilerParams(
            dimension_semantics=("arbitrary", "arbitrary"), vmem_limit_bytes=VMEM_LIMIT),
        name="even_mixer",
    )(x, row(norm_g), w_in_r, wal, row(b_alpha), row(gla_norm), row(ln_g), row(ln_b),
      sgu_w.astype(F32), sgub, w_out.astype(BF16))


def _odd_kernel(x_ref, gn_ref, win_ref, mu_ref, wlr_ref, w0_ref, a0_ref, kk_ref, ka_ref, rk_ref,
                gng_ref, gnb_ref, cw_ref, cb_ref, clg_ref, clb_ref, wout_ref, o_ref,
                prev_ref, zc_ref, cv_s, cum_s, last_s, y_s, rt_s, pt_s, qt_s, kt_s, qh_s, kh_s, v_s,
                z_s, gl_s, ark_s, uw_s, *, tb):
    t_idx = pl.program_id(1)
    gw = RWKV_GROUP * RWKV_HEAD
    n_groups = RWKV_WIDTH // gw
    n_chunks = tb // RWKV_CHUNK

    @pl.when(t_idx == 0)
    def _():
        prev_ref[...] = jnp.zeros_like(prev_ref)
        zc_ref[0:CONV_HALO, :] = jnp.zeros((CONV_HALO, CONV_WIDTH), F32)
        zc_ref[tb + CONV_HALO:tb + CONV_HALO + 8, :] = jnp.zeros((8, CONV_WIDTH), F32)
        z_s[...] = jnp.zeros_like(z_s)

    @pl.when(t_idx != 0)
    def _():
        zc_ref[0:CONV_HALO, :] = zc_ref[tb:tb + CONV_HALO, :]

    x = x_ref[...]
    h = _rms(x, gn_ref[...], RMS_EPS).astype(BF16)
    c_r, c_k, c_v = slice(0, 512), slice(512, 1024), slice(1024, 1536)
    c_lr = slice(3 * RWKV_WIDTH, RWKV_COLS)
    c_conv = slice(RWKV_COLS, ODD_COLS)

    def project(cols):
        return _dot(h, win_ref[:, cols])

    def shift_mix(pr, cols):
        sh = pltpu.roll(pr, 1, 0)
        head = jnp.where(_iota((8, pr.shape[1]), 0) == 0, prev_ref[7:8, cols], sh[0:8])
        prev_ref[:, cols] = pr[tb - 8:tb]
        sh = jnp.concatenate([head, sh[8:]], axis=0)
        return pr + (sh - pr) * mu_ref[:, cols]

    never = jnp.full((CONV_TILE_ROWS, LANE), t_idx, jnp.int32) < 0

    def conv_tile(r0, n, lanes, after):
        first = CONV_HALO - (CONV_KERNEL - 1)
        wn = n + CONV_HALO + 8
        win = zc_ref[pl.ds(r0, wn), lanes]
        acc = jnp.where(never, jnp.concatenate([after] * (n // 8), axis=0),
                        jnp.zeros((n, LANE), F32) + cb_ref[:, lanes])
        for ph in range(8):
            sh = win if ph == 0 else pltpu.roll(win, wn - ph, 0)
            for j in range(CONV_KERNEL):
                if (j + first) % 8 == ph:
                    a8 = (j + first) // 8 * 8
                    acc = acc + sh[a8:a8 + n] * cw_ref[j:j + 1, lanes]
        return acc

    hm = jnp.where((_iota((256, 256), 0) >> 6) == (_iota((256, 256), 1) >> 6),
                   1.0, 0.0).astype(BF16)
    p_lr = project(c_lr)
    p_k = project(c_k)
    lr = shift_mix(p_lr, c_lr)
    lane = _iota((tb, 256), 1)
    f = jnp.where(lane < RWKV_DECAY_RANK, jnp.tanh(lr),
                  jnp.where(lane < RWKV_DECAY_RANK + RWKV_AAA_RANK, lr, _sigmoid(lr)))
    p_r = project(c_r)
    lo3 = _dot(f.astype(BF16), wlr_ref[...])
    k = shift_mix(p_k, c_k)
    kk = k * kk_ref[...]
    kk = kk * lax.rsqrt(jnp.maximum(_head_sums(kk * kk, hm), 1e-24))
    p_v = project(c_v)
    r = shift_mix(p_r, c_r)
    lw = (-0.6065306597126334) * _sigmoid(w0_ref[...] + lo3[:, 0:512])
    a = _sigmoid(a0_ref[...] + lo3[:, 512:1024])
    gate = lo3[:, 1024:1536]
    k2 = k * (1.0 + (a - 1.0) * ka_ref[...])
    p_conv = project(c_conv)
    v = shift_mix(p_v, c_v)
    bonus = _head_sums(r * k2 * rk_ref[...], hm) * v

    tril2 = jnp.concatenate([_tril_ones(RWKV_CHUNK, BF16)] * 2, axis=1)
    for c in range(n_chunks):
        cum_c, last_c = _chunk_cumsum(tril2, lw[c * RWKV_CHUNK:(c + 1) * RWKV_CHUNK])
        cum_s[c * RWKV_CHUNK:(c + 1) * RWKV_CHUNK, :] = cum_c
        last_s[c * RWKV_CHUNK:(c + 1) * RWKV_CHUNK, :] = last_c
    cum = cum_s[...]
    e_last = jnp.exp(last_s[...])
    for c in range(n_chunks):
        gl_s[c] = e_last[c * RWKV_CHUNK:c * RWKV_CHUNK + 8]
    e_neg = jnp.exp(-cum)
    e_hat = e_last * e_neg
    qa = kk * a
    rt_s[...] = (r * jnp.exp(cum)).astype(BF16)
    pt_s[...] = (-kk * jnp.exp(cum - lw)).astype(BF16)
    qt_s[...] = (qa * e_neg).astype(BF16)
    kt_s[...] = (k2 * e_neg).astype(BF16)
    qh_s[...] = (qa * e_hat).astype(BF16)
    kh_s[...] = (k2 * e_hat).astype(BF16)
    v_s[...] = v.astype(BF16)

    zc_ref[CONV_HALO:CONV_HALO + tb, :] = p_conv[:, 0:CONV_WIDTH] * _sigmoid(
        p_conv[:, CONV_WIDTH:2 * CONV_WIDTH])

    ri = _iota((gw, gw), 0)
    ci = _iota((gw, gw), 1)
    same = (ri >> 6) == (ci >> 6)
    strict = same & (ri > ci)
    ri2 = _iota((gw, 2 * gw), 0)
    ci2 = _iota((gw, 2 * gw), 1) & (gw - 1)
    incl2 = ((ri2 >> 6) == (ci2 >> 6)) & (ri2 >= ci2)
    eye = jnp.where(ri == ci, 1.0, 0.0)

    zero_bf = jnp.zeros((gw, gw), BF16)

    def bd(xs):
        return jnp.where(same, jnp.concatenate([xs] * RWKV_GROUP, axis=0), zero_bf)

    def aligned(i, m):
        return i if isinstance(i, int) else pl.multiple_of(i, m)

    def local_stages(ci_):
        probs = [(cc, j) for cc in range(LOCAL_CHUNKS) for j in range(n_groups)]
        idx = range(len(probs))
        base = aligned(ci_ * (LOCAL_CHUNKS * RWKV_CHUNK), LOCAL_CHUNKS * RWKV_CHUNK)

        def blk(ref, cc, j):
            return bd(ref[pl.ds(base + cc * RWKV_CHUNK, RWKV_CHUNK), j * gw:(j + 1) * gw])

        def slot(cc, j):
            return (ci_ * LOCAL_CHUNKS + cc) * n_groups + j

        P2 = [blk(pt_s, cc, j) for cc, j in probs]
        PR = [jnp.concatenate([P2[i], blk(rt_s, cc, j)], axis=0) for i, (cc, j) in enumerate(probs)]
        QK = [jnp.concatenate([blk(qt_s, cc, j), blk(kt_s, cc, j)], axis=0) for cc, j in probs]
        V2 = [blk(v_s, cc, j) for cc, j in probs]
        sc = [_dot_nt(PR[i], QK[i]) for i in idx]
        yield sc[-1][0:8, 0:LANE]
        A_pq = [jnp.where(strict, sc[i][0:gw, 0:gw], 0.0) for i in idx]
        A_pk = [jnp.where(strict, sc[i][0:gw, gw:2 * gw], 0.0).astype(BF16) for i in idx]
        for i, (cc, j) in enumerate(probs):
            ark_s[slot(cc, j)] = jnp.where(incl2, sc[i][gw:2 * gw, :], 0.0).astype(BF16)
        Xb = [A_pq[i].astype(BF16) for i in idx]
        Xp = [_dot(Xb[i], Xb[i]) for i in idx]
        av = [_dot(A_pk[i], V2[i]) for i in idx]
        yield av[-1][0:8, 0:LANE]
        av = [av[i].astype(BF16) for i in idx]
        Tm = [eye + A_pq[i] for i in idx]
        for step in range(4):
            Xb = [Xp[i].astype(BF16) for i in idx]
            XT = [_dot(Xb[i], jnp.concatenate([Xb[i], Tm[i].astype(BF16)], axis=1)) for i in idx]
            yield XT[-1][0:8, 0:LANE]
            Xp = [XT[i][:, 0:gw] for i in idx]
            Tm = [Tm[i] + XT[i][:, gw:2 * gw] for i in idx]
        XT = [_dot(Xp[i].astype(BF16), Tm[i].astype(BF16)) for i in idx]
        yield XT[-1][0:8, 0:LANE]
        Tm = [(Tm[i] + XT[i]).astype(BF16) for i in idx]
        uw = [_dot(Tm[i], jnp.concatenate([av[i], P2[i]], axis=1)) for i in idx]
        for i, (cc, j) in enumerate(probs):
            uw_s[slot(cc, j)] = uw[i].astype(BF16)
        yield uw[-1][0:8, 0:LANE]

    def state_stages(ci_):
        groups = range(n_groups)
        sls = [slice(j * gw, (j + 1) * gw) for j in groups]
        for cc in range(LOCAL_CHUNKS):
            c = ci_ * LOCAL_CHUNKS + cc
            rows = pl.ds(aligned(c * RWKV_CHUNK, RWKV_CHUNK), RWKV_CHUNK)
            e_last = gl_s[c][0:1, :]
            R2 = [bd(rt_s[rows, s]) for s in sls]
            uw = [uw_s[c * n_groups + j] for j in groups]
            Z = [z_s[j] for j in groups]
            hw = [_dot_nt(jnp.concatenate([uw[j][:, gw:2 * gw], R2[j]], axis=0), Z[j].astype(BF16))
                  for j in groups]
            yield
            QK = [jnp.concatenate([bd(qh_s[rows, s]), bd(kh_s[rows, s])], axis=0) for s in sls]
            V2 = [bd(v_s[rows, s]) for s in sls]
            UV = [jnp.concatenate([(uw[j][:, 0:gw].astype(F32) + hw[j][0:gw]).astype(BF16), V2[j]],
                                  axis=0) for j in groups]
            zn = [_dot_tn(UV[j], QK[j]) for j in groups]
            Y2 = [hw[j][gw:2 * gw] + _dot(ark_s[c * n_groups + j], UV[j]) for j in groups]
            for j in groups:
                z_s[j] = Z[j] * e_last[:, sls[j]] + zn[j]
                y = Y2[j][0:RWKV_CHUNK]
                for e in range(1, RWKV_GROUP):
                    y = y + Y2[j][e * RWKV_CHUNK:(e + 1) * RWKV_CHUNK]
                y_s[rows, sls[j]] = y
            yield

    lane_blocks = CONV_WIDTH // LANE

    conv_tiles = [(r, lb) for r in range(0, LOCAL_CHUNKS * RWKV_CHUNK, CONV_TILE_ROWS)
                  for lb in range(lane_blocks)]
    tiles_per_stage = len(conv_tiles) // LOCAL_STAGES

    def conv_after(ci_, k, after):
        for r, lb in conv_tiles[k * tiles_per_stage:(k + 1) * tiles_per_stage]:
            lanes = slice(lb * LANE, (lb + 1) * LANE)
            r0 = aligned(ci_ * (LOCAL_CHUNKS * RWKV_CHUNK) + r, CONV_TILE_ROWS)
            cv_s[pl.ds(r0, CONV_TILE_ROWS), lanes] = conv_tile(r0, CONV_TILE_ROWS, lanes, after)

    n_groups_t = n_chunks // LOCAL_CHUNKS
    for k, after in enumerate(local_stages(0)):
        conv_after(0, k, after)

    def pipelined(i, carry):
        g_local, g_state = local_stages(i), state_stages(i - 1)
        k = 0
        for which in PIPELINE_ORDER:
            if which == "L":
                conv_after(i, k, next(g_local))
                k += 1
            else:
                next(g_state)
        return carry

    lax.fori_loop(1, n_groups_t, pipelined, 0)
    for _ in state_stages(n_groups_t - 1):
        pass

    o_d = _silu(_layernorm(cv_s[...], clg_ref[...], clb_ref[...], LN_EPS))

    y = y_s[...]
    inv_n = 1.0 / RWKV_HEAD
    mu_y = _head_sums(y, hm) * inv_n
    yc = y - mu_y
    var_y = _head_sums(yc * yc, hm) * inv_n
    y = yc * lax.rsqrt(var_y + RWKV_GN_EPS) * gng_ref[...] + gnb_ref[...] + bonus
    o_c = y * gate

    m = _dot(o_c.astype(BF16), wout_ref[0:512, :]) + _dot(o_d.astype(BF16), wout_ref[512:1024, :])
    o_ref[...] = x + m


def _odd_layer(x, norm_g, w_in, w_out, mu, w0, w2, a0, a2, g2, k_k, k_a, r_k, gn_g, gn_b,
               conv_w, conv_b, cln_g, cln_b, *, tb):
    B, T, D = x.shape
    pad_lr = 256 - (RWKV_DECAY_RANK + RWKV_AAA_RANK + RWKV_GATE_RANK)
    w_in_r = jnp.concatenate(
        [w_in[:, :RWKV_IN], jnp.zeros((D, pad_lr), w_in.dtype), w_in[:, RWKV_IN:]], axis=1).astype(BF16)
    mu_r = jnp.concatenate([mu.astype(F32), jnp.zeros((pad_lr,), F32)]).reshape(1, RWKV_COLS)
    wlr = jnp.zeros((256, 3 * RWKV_WIDTH), F32)
    wlr = wlr.at[0:32, 0:512].set(w2.astype(F32))
    wlr = wlr.at[32:64, 512:1024].set(a2.astype(F32))
    wlr = wlr.at[64:160, 1024:1536].set(g2.astype(F32))
    row = lambda a: a.astype(F32).reshape(1, -1)
    const = lambda shape: pl.BlockSpec(shape, lambda b, t: (0,) * len(shape),
                                       pipeline_mode=pl.Buffered(1))
    gw = RWKV_GROUP * RWKV_HEAD
    kern = functools.partial(_odd_kernel, tb=tb)
    return pl.pallas_call(
        kern,
        grid=(B, T // tb),
        in_specs=[
            pl.BlockSpec((None, tb, D), lambda b, t: (b, t, 0)),
            const((1, D)), const((D, ODD_COLS)), const((1, RWKV_COLS)), const((256, 3 * RWKV_WIDTH)),
            const((1, 512)), const((1, 512)), const((1, 512)), const((1, 512)), const((1, 512)),
            const((1, 512)), const((1, 512)),
            const((CONV_KERNEL, CONV_WIDTH)), const((1, 512)), const((1, 512)), const((1, 512)),
            const((D, D)),
        ],
        out_specs=pl.BlockSpec((None, tb, D), lambda b, t: (b, t, 0)),
        out_shape=jax.ShapeDtypeStruct((B, T, D), F32),
        scratch_shapes=[
            pltpu.VMEM((8, RWKV_COLS), F32),
            pltpu.VMEM((tb + CONV_HALO + 8, CONV_WIDTH), F32),
            pltpu.VMEM((tb, 512), F32),
            pltpu.VMEM((tb, 512), F32), pltpu.VMEM((tb, 512), F32),
            pltpu.VMEM((tb, 512), F32),
            pltpu.VMEM((tb, 512), BF16), pltpu.VMEM((tb, 512), BF16), pltpu.VMEM((tb, 512), BF16),
            pltpu.VMEM((tb, 512), BF16), pltpu.VMEM((tb, 512), BF16), pltpu.VMEM((tb, 512), BF16),
            pltpu.VMEM((tb, 512), BF16),
            pltpu.VMEM((RWKV_WIDTH // gw, gw, gw), F32),
            pltpu.VMEM((tb // RWKV_CHUNK, 8, RWKV_WIDTH), F32),
            pltpu.VMEM((tb // RWKV_CHUNK * (RWKV_WIDTH // gw), gw, 2 * gw), BF16),
            pltpu.VMEM((tb // RWKV_CHUNK * (RWKV_WIDTH // gw), gw, 2 * gw), BF16),
        ],
        compiler_params=pltpu.CompilerParams(
            dimension_semantics=("arbitrary", "arbitrary"), vmem_limit_bytes=VMEM_LIMIT),
        name="odd_mixer",
    )(x, row(norm_g), w_in_r, mu_r, wlr.astype(BF16), row(w0), row(a0), row(k_k), row(k_a), row(r_k),
      row(gn_g), row(gn_b), conv_w.astype(F32), row(conv_b), row(cln_g), row(cln_b), w_out.astype(BF16))


def _mlp_kernel(x_ref, g_ref, wup_ref, wdn_ref, gf_ref, o_ref, *, ff_chunk, final_norm):
    x = x_ref[...]
    xb = _rms(x, g_ref[...], RMS_EPS).astype(BF16)
    acc = x
    d_ff = wup_ref.shape[1]
    for f0 in range(0, d_ff, ff_chunk):
        hcol = _dot(xb, wup_ref[:, f0:f0 + ff_chunk])
        hcol = jnp.square(jnp.maximum(hcol, 0.0))
        acc = acc + _dot(hcol.astype(BF16), wdn_ref[f0:f0 + ff_chunk, :])
    if final_norm:
        acc = _rms(acc, gf_ref[...], RMS_EPS)
    o_ref[...] = acc


def _mlp_layer(x2, norm_g, w_up, w_down, final_g, *, tm, final_norm):
    M, D = x2.shape
    d_ff = w_up.shape[1]
    kern = functools.partial(_mlp_kernel, ff_chunk=1024, final_norm=final_norm)
    return pl.pallas_call(
        kern,
        grid=(M // tm,),
        in_specs=[
            pl.BlockSpec((tm, D), lambda i: (i, 0)),
            pl.BlockSpec((1, D), lambda i: (0, 0)),
            pl.BlockSpec((D, d_ff), lambda i: (0, 0), pipeline_mode=pl.Buffered(1)),
            pl.BlockSpec((d_ff, D), lambda i: (0, 0), pipeline_mode=pl.Buffered(1)),
            pl.BlockSpec((1, D), lambda i: (0, 0)),
        ],
        out_specs=pl.BlockSpec((tm, D), lambda i: (i, 0)),
        out_shape=jax.ShapeDtypeStruct((M, D), F32),
        compiler_params=pltpu.CompilerParams(
            dimension_semantics=("arbitrary",), vmem_limit_bytes=VMEM_LIMIT),
        name="mlp_final" if final_norm else "mlp",
    )(x2, norm_g.astype(F32).reshape(1, D), w_up.astype(BF16), w_down.astype(BF16),
      final_g.astype(F32).reshape(1, D))


def _pick_block(n, target):
    b = min(n, target)
    while n % b:
        b //= 2
    return b


def kernel(x, norm_mix, norm_ffn, w_up, w_down, norm_final, even_w_in, even_w_out, gla_w_alpha2, gla_b_alpha, gla_norm, sgu_ln_g, sgu_ln_b, sgu_w, sgu_b, odd_w_in, odd_w_out, rwkv_mu, rwkv_w0, rwkv_w2, rwkv_a0, rwkv_a2, rwkv_g2, rwkv_k_k, rwkv_k_a, rwkv_r_k, rwkv_gn_g, rwkv_gn_b, conv_w, conv_b, conv_ln_g, conv_ln_b):
    B, T, D = x.shape
    depth = norm_mix.shape[0]
    tb_even = _pick_block(T, EVEN_BLOCK)
    tb_odd = _pick_block(T, ODD_BLOCK)
    tm = _pick_block(B * T, MLP_BLOCK)
    assert tb_even % (GLA_LOCAL * GLA_CHUNK) == 0 and tb_even % SGU_CHUNK == 0
    assert tb_odd % (LOCAL_CHUNKS * RWKV_CHUNK) == 0 and D == 1024
    for layer in range(depth):
        j = layer // 2
        if layer % 2 == 0:
            x = _even_layer(x, norm_mix[layer], even_w_in[j], even_w_out[j], gla_w_alpha2[j],
                            gla_b_alpha[j], gla_norm[j], sgu_ln_g[j], sgu_ln_b[j], sgu_w[j], sgu_b[j],
                            tb=tb_even)
        else:
            x = _odd_layer(x, norm_mix[layer], odd_w_in[j], odd_w_out[j], rwkv_mu[j], rwkv_w0[j],
                           rwkv_w2[j], rwkv_a0[j], rwkv_a2[j], rwkv_g2[j], rwkv_k_k[j], rwkv_k_a[j],
                           rwkv_r_k[j].reshape(-1), rwkv_gn_g[j], rwkv_gn_b[j], conv_w[j], conv_b[j],
                           conv_ln_g[j], conv_ln_b[j], tb=tb_odd)
        last = layer == depth - 1
        x = _mlp_layer(x.reshape(B * T, D), norm_ffn[layer], w_up[layer], w_down[layer], norm_final,
                       tm=tm, final_norm=last).reshape(B, T, D)
    return x
```

```python
import functools

import jax
import jax.numpy as jnp
from jax import lax
from jax.experimental import pallas as pl
from jax.experimental.pallas import tpu as pltpu

F32 = jnp.float32
BF16 = jnp.bfloat16

RMS_EPS = 1e-6
LN_EPS = 1e-5

GLA_HEADS = 4
GLA_DK = 64
GLA_DV = 128
GLA_KEY = GLA_HEADS * GLA_DK
GLA_WIDTH = GLA_HEADS * GLA_DV
GLA_GATE_RANK = 16
GLA_TAU = 16.0
GLA_CHUNK = 64
GLA_LOCAL = 4
SGU_WIDTH = 512
SGU_GROUPS = 4
SGU_CHUNK = 128
EVEN_COLS = 2688

RWKV_WIDTH = 512
RWKV_HEAD = 64
RWKV_HEADS = 8
RWKV_DECAY_RANK = 32
RWKV_AAA_RANK = 32
RWKV_GATE_RANK = 96
RWKV_GN_EPS = 64e-5
RWKV_IN = 1696
RWKV_COLS = 1792
RWKV_CHUNK = 64
RWKV_GROUP = 2
LOCAL_CHUNKS = 2
LOCAL_STAGES = 8
PIPELINE_ORDER = "LSLLSLSLLSLL"
CONV_WIDTH = 512
CONV_KERNEL = 31
CONV_HALO = 32
CONV_TILE_ROWS = 32
ODD_COLS = RWKV_COLS + 2 * CONV_WIDTH

LANE = 128
VMEM_LIMIT = 56 * 1024 * 1024
EVEN_BLOCK = 1024
ODD_BLOCK = 512
MLP_BLOCK = 1024


def _iota(shape, dim):
    return lax.broadcasted_iota(jnp.int32, shape, dim)


def _dot(a, b):
    return jnp.dot(a, b, preferred_element_type=F32)


def _dot_nt(a, b):
    return lax.dot_general(a, b, (((1,), (1,)), ((), ())), preferred_element_type=F32)


def _dot_tn(a, b):
    return lax.dot_general(a, b, (((0,), (0,)), ((), ())), preferred_element_type=F32)


def _split2(x):
    hi = x.astype(BF16)
    lo = (x - hi.astype(F32)).astype(BF16)
    return hi, lo


def _chunk_cumsum(tril2, x):
    hi, lo = _split2(x)
    cum = _dot(tril2, jnp.concatenate([hi, lo], axis=0))
    n = x.shape[0]
    return cum, jnp.broadcast_to(cum[n - 1:n], cum.shape)


def _head_sums(x, hm_half):
    half = hm_half.shape[0]
    xb = x.astype(BF16)
    return jnp.concatenate([_dot(xb[:, 0:half], hm_half), _dot(xb[:, half:2 * half], hm_half)], axis=1)


def _rms(x, g, eps):
    return x * lax.rsqrt(jnp.mean(x * x, axis=-1, keepdims=True) + eps) * g


def _layernorm(x, g, b, eps):
    mu = jnp.mean(x, axis=-1, keepdims=True)
    xc = x - mu
    var = jnp.mean(xc * xc, axis=-1, keepdims=True)
    return xc * lax.rsqrt(var + eps) * g + b


def _sigmoid(x):
    return 1.0 / (1.0 + jnp.exp(-x))


def _silu(x):
    return x * _sigmoid(x)


def _softplus(x):
    return jnp.maximum(x, 0.0) + jnp.log(1.0 + jnp.exp(-jnp.abs(x)))


def _gelu_tanh(x):
    c = 0.7978845608028654
    return 0.5 * x * (1.0 + jnp.tanh(c * (x + 0.044715 * (x * x * x))))


def _tril_ones(n, dtype):
    return jnp.where(_iota((n, n), 0) >= _iota((n, n), 1), 1.0, 0.0).astype(dtype)


def _even_kernel(x_ref, gn_ref, win_ref, wal_ref, bal_ref, glan_ref, lng_ref, lnb_ref,
                 sguw_ref, sgub_ref, wout_ref, o_ref,
                 la_ref, bl_ref, dl_ref, qd_ref, kd_ref, ks_ref, vb_ref, gt_ref, oa_ref, ob_ref, st_ref,
                 *, tb):
    t_idx = pl.program_id(1)

    @pl.when(t_idx == 0)
    def _():
        st_ref[...] = jnp.zeros_like(st_ref)

    x = x_ref[...]
    h = _rms(x, gn_ref[...], RMS_EPS).astype(BF16)

    def project(c0, c1):
        return _dot(h, win_ref[:, c0:c1])

    p_alr = project(2560, 2688)
    p_sv = project(2048, 2560)
    a_hi, a_lo = _split2(p_alr)
    w_hi, w_lo = _split2(wal_ref[...])
    z = _dot(a_hi, w_hi) + _dot(a_hi, w_lo) + _dot(a_lo, w_hi) + bal_ref[...]
    la = -_softplus(-z) * (1.0 / GLA_TAU)
    p_u = project(1536, 2048)
    sv = _layernorm(_gelu_tanh(p_sv), lng_ref[...], lnb_ref[...], LN_EPS).astype(BF16)

    n_chunks = tb // GLA_CHUNK
    tril2 = jnp.concatenate([_tril_ones(GLA_CHUNK, BF16)] * 2, axis=1)
    for c in range(n_chunks):
        cum_c, last_c = _chunk_cumsum(tril2, la[c * GLA_CHUNK:(c + 1) * GLA_CHUNK])
        la_ref[c * GLA_CHUNK:(c + 1) * GLA_CHUNK, :] = cum_c
        bl_ref[c * GLA_CHUNK:(c + 1) * GLA_CHUNK, :] = last_c
    p_qk = project(0, 512)
    ug = _gelu_tanh(p_u)

    tri128 = _iota((SGU_CHUNK, SGU_CHUNK), 0) >= _iota((SGU_CHUNK, SGU_CHUNK), 1)
    for gg in range(SGU_GROUPS):
        wg = jnp.where(tri128, sguw_ref[gg], 0.0).astype(BF16)
        for c in range(tb // SGU_CHUNK):
            r0 = c * SGU_CHUNK
            s = _dot(wg, sv[r0:r0 + SGU_CHUNK, gg * LANE:(gg + 1) * LANE])
            s = s + sgub_ref[:, gg * LANE:(gg + 1) * LANE]
            ob_ref[r0:r0 + SGU_CHUNK, gg * LANE:(gg + 1) * LANE] = (
                ug[r0:r0 + SGU_CHUNK, gg * LANE:(gg + 1) * LANE] * s)

    p_v = project(512, 1024)
    b = la_ref[...]
    b_last = bl_ref[...]
    for c in range(n_chunks):
        dl_ref[c] = jnp.exp(b_last[c * GLA_CHUNK:c * GLA_CHUNK + 8])
    k = p_qk[:, 256:512]
    qd_ref[...] = (p_qk[:, 0:256] * (GLA_DK ** -0.5) * jnp.exp(b)).astype(BF16)
    kd_ref[...] = (k * jnp.exp(-b)).astype(BF16)
    ks_ref[...] = (k * jnp.exp(b_last - b)).astype(BF16)
    p_g = project(1024, 1536)
    vb_ref[...] = p_v.astype(BF16)
    gt_ref[...] = glan_ref[...] * _silu(p_g)

    rk = _iota((GLA_KEY, GLA_KEY), 0) >> 6
    ck = _iota((GLA_KEY, GLA_KEY), 1) >> 6
    k4_mask = rk == ck
    rv = _iota((GLA_KEY, GLA_WIDTH), 0) >> 6
    cv = _iota((GLA_KEY, GLA_WIDTH), 1) >> 7
    vbd_mask = rv == cv
    causal = _iota((GLA_CHUNK, GLA_KEY), 0) >= (_iota((GLA_CHUNK, GLA_KEY), 1) & 63)
    rs = _iota((GLA_WIDTH, GLA_KEY), 0) >> 7
    cs = _iota((GLA_WIDTH, GLA_KEY), 1) >> 6
    st_mask = rs == cs

    zero_k = jnp.zeros((GLA_KEY, GLA_KEY), BF16)
    zero_v = jnp.zeros((GLA_KEY, GLA_WIDTH), BF16)

    def gla_step(ci_, carry):
        base = pl.multiple_of(ci_ * (GLA_LOCAL * GLA_CHUNK), GLA_LOCAL * GLA_CHUNK)
        ccs = range(GLA_LOCAL)
        rows = [pl.ds(base + cc * GLA_CHUNK, GLA_CHUNK) for cc in ccs]
        q_dec = [qd_ref[r, :] for r in rows]
        k4 = [jnp.where(k4_mask, jnp.concatenate([kd_ref[r, :]] * GLA_HEADS, axis=0), zero_k)
              for r in rows]
        v = [vb_ref[r, :] for r in rows]
        v_bd = [jnp.where(vbd_mask, jnp.concatenate([v[cc]] * GLA_HEADS, axis=0), zero_v) for cc in ccs]
        att = [_dot_nt(q_dec[cc], k4[cc]) for cc in ccs]
        kv = [_dot_tn(v[cc], ks_ref[rows[cc], :]) for cc in ccs]
        att = [jnp.where(causal, att[cc], 0.0).astype(BF16) for cc in ccs]
        o_intra = [_dot(att[cc], v_bd[cc]) for cc in ccs]
        st = st_ref[...]
        for cc in ccs:
            o = o_intra[cc] + _dot_nt(q_dec[cc], st.astype(BF16))
            dl = dl_ref[ci_ * GLA_LOCAL + cc][0:1, :]
            st = st * dl + jnp.where(st_mask, kv[cc], 0.0)
            for hh in range(GLA_HEADS):
                oh = o[:, hh * GLA_DV:(hh + 1) * GLA_DV]
                oh = oh * lax.rsqrt(jnp.mean(oh * oh, axis=-1, keepdims=True) + RMS_EPS)
                oa_ref[rows[cc], hh * GLA_DV:(hh + 1) * GLA_DV] = oh
        st_ref[...] = st
        return carry

    lax.fori_loop(0, n_chunks // GLA_LOCAL, gla_step, 0)

    o_a = oa_ref[...] * gt_ref[...]
    m = _dot(o_a.astype(BF16), wout_ref[0:512, :]) + _dot(ob_ref[...].astype(BF16), wout_ref[512:1024, :])
    o_ref[...] = x + m


def _even_layer(x, norm_g, w_in, w_out, w_alpha2, b_alpha, gla_norm, ln_g, ln_b, sgu_w, sgu_b, *, tb):
    B, T, D = x.shape
    w_in = w_in.astype(BF16)
    w_in_r = jnp.concatenate(
        [w_in[:, :1536], w_in[:, 1552:2576], w_in[:, 1536:1552],
         jnp.zeros((D, EVEN_COLS - 2576), BF16)], axis=1)
    wal = jnp.zeros((LANE, GLA_KEY), F32).at[:GLA_GATE_RANK].set(w_alpha2.astype(F32))
    sgub = jnp.repeat(sgu_b.astype(F32).T, LANE, axis=1)
    row = lambda a: a.astype(F32).reshape(1, -1)
    const = lambda shape: pl.BlockSpec(shape, lambda b, t: (0,) * len(shape),
                                       pipeline_mode=pl.Buffered(1))
    kern = functools.partial(_even_kernel, tb=tb)
    return pl.pallas_call(
        kern,
        grid=(B, T // tb),
        in_specs=[
            pl.BlockSpec((None, tb, D), lambda b, t: (b, t, 0)),
            const((1, D)), const((D, EVEN_COLS)), const((LANE, GLA_KEY)), const((1, GLA_KEY)),
            const((1, GLA_WIDTH)), const((1, SGU_WIDTH)), const((1, SGU_WIDTH)),
            const((SGU_GROUPS, SGU_CHUNK, SGU_CHUNK)), const((SGU_CHUNK, SGU_WIDTH)),
            const((D, D)),
        ],
        out_specs=pl.BlockSpec((None, tb, D), lambda b, t: (b, t, 0)),
        out_shape=jax.ShapeDtypeStruct((B, T, D), F32),
        scratch_shapes=[
            pltpu.VMEM((tb, GLA_KEY), F32),
            pltpu.VMEM((tb, GLA_KEY), F32),
            pltpu.VMEM((tb // GLA_CHUNK, 8, GLA_KEY), F32),
            pltpu.VMEM((tb, GLA_KEY), BF16), pltpu.VMEM((tb, GLA_KEY), BF16),
            pltpu.VMEM((tb, GLA_KEY), BF16),
            pltpu.VMEM((tb, GLA_WIDTH), BF16),
            pltpu.VMEM((tb, GLA_WIDTH), F32),
            pltpu.VMEM((tb, GLA_WIDTH), F32),
            pltpu.VMEM((tb, SGU_WIDTH), F32),
            pltpu.VMEM((GLA_WIDTH, GLA_KEY), F32),
        ],
        compiler_params=pltpu.CompilerParams(
            dimension_semantics=("arbitrary", "arbitrary"), vmem_limit_bytes=VMEM_LIMIT),
        name="even_mixer",
    )(x, row(norm_g), w_in_r, wal, row(b_alpha), row(gla_norm), row(ln_g), row(ln_b),
      sgu_w.astype(F32), sgub, w_out.astype(BF16))


def _odd_kernel(x_ref, gn_ref, win_ref, mu_ref, wlr_ref, w0_ref, a0_ref, kk_ref, ka_ref, rk_ref,
                gng_ref, gnb_ref, cw_ref, cb_ref, clg_ref, clb_ref, wout_ref, o_ref,
                prev_ref, zc_ref, cv_s, cum_s, last_s, y_s, rt_s, pt_s, qt_s, kt_s, qh_s, kh_s, v_s,
                z_s, gl_s, ark_s, uw_s, *, tb):
    t_idx = pl.program_id(1)
    gw = RWKV_GROUP * RWKV_HEAD
    n_groups = RWKV_WIDTH // gw
    n_chunks = tb // RWKV_CHUNK

    @pl.when(t_idx == 0)
    def _():
        prev_ref[...] = jnp.zeros_like(prev_ref)
        zc_ref[0:CONV_HALO, :] = jnp.zeros((CONV_HALO, CONV_WIDTH), F32)
        zc_ref[tb + CONV_HALO:tb + CONV_HALO + 8, :] = jnp.zeros((8, CONV_WIDTH), F32)
        z_s[...] = jnp.zeros_like(z_s)

    @pl.when(t_idx != 0)
    def _():
        zc_ref[0:CONV_HALO, :] = zc_ref[tb:tb + CONV_HALO, :]

    x = x_ref[...]
    h = _rms(x, gn_ref[...], RMS_EPS).astype(BF16)
    c_r, c_k, c_v = slice(0, 512), slice(512, 1024), slice(1024, 1536)
    c_lr = slice(3 * RWKV_WIDTH, RWKV_COLS)
    c_conv = slice(RWKV_COLS, ODD_COLS)

    def project(cols):
        return _dot(h, win_ref[:, cols])

    def shift_mix(pr, cols):
        sh = pltpu.roll(pr, 1, 0)
        head = jnp.where(_iota((8, pr.shape[1]), 0) == 0, prev_ref[7:8, cols], sh[0:8])
        prev_ref[:, cols] = pr[tb - 8:tb]
        sh = jnp.concatenate([head, sh[8:]], axis=0)
        return pr + (sh - pr) * mu_ref[:, cols]

    never = jnp.full((CONV_TILE_ROWS, LANE), t_idx, jnp.int32) < 0

    def conv_tile(r0, n, lanes, after):
        first = CONV_HALO - (CONV_KERNEL - 1)
        wn = n + CONV_HALO + 8
        win = zc_ref[pl.ds(r0, wn), lanes]
        acc = jnp.where(never, jnp.concatenate([after] * (n // 8), axis=0),
                        jnp.zeros((n, LANE), F32) + cb_ref[:, lanes])
        for ph in range(8):
            sh = win if ph == 0 else pltpu.roll(win, wn - ph, 0)
            for j in range(CONV_KERNEL):
                if (j + first) % 8 == ph:
                    a8 = (j + first) // 8 * 8
                    acc = acc + sh[a8:a8 + n] * cw_ref[j:j + 1, lanes]
        return acc

    hm = jnp.where((_iota((256, 256), 0) >> 6) == (_iota((256, 256), 1) >> 6),
                   1.0, 0.0).astype(BF16)
    p_lr = project(c_lr)
    p_k = project(c_k)
    lr = shift_mix(p_lr, c_lr)
    lane = _iota((tb, 256), 1)
    f = jnp.where(lane < RWKV_DECAY_RANK, jnp.tanh(lr),
                  jnp.where(lane < RWKV_DECAY_RANK + RWKV_AAA_RANK, lr, _sigmoid(lr)))
    lo3 = _dot(f.astype(BF16), wlr_ref[...])
    p_r = project(c_r)
    k = shift_mix(p_k, c_k)
    kk = k * kk_ref[...]
    kk = kk * lax.rsqrt(jnp.maximum(_head_sums(kk * kk, hm), 1e-24))
    lw = (-0.6065306597126334) * _sigmoid(w0_ref[...] + lo3[:, 0:512])
    a = _sigmoid(a0_ref[...] + lo3[:, 512:1024])
    gate = lo3[:, 1024:1536]

    tril2 = jnp.concatenate([_tril_ones(RWKV_CHUNK, BF16)] * 2, axis=1)
    for c in range(n_chunks):
        cum_c, last_c = _chunk_cumsum(tril2, lw[c * RWKV_CHUNK:(c + 1) * RWKV_CHUNK])
        cum_s[c * RWKV_CHUNK:(c + 1) * RWKV_CHUNK, :] = cum_c
        last_s[c * RWKV_CHUNK:(c + 1) * RWKV_CHUNK, :] = last_c
    p_v = project(c_v)
    cum = cum_s[...]
    e_last = jnp.exp(last_s[...])
    for c in range(n_chunks):
        gl_s[c] = e_last[c * RWKV_CHUNK:c * RWKV_CHUNK + 8]
    e_neg = jnp.exp(-cum)
    e_hat = e_last * e_neg
    k2 = k * (1.0 + (a - 1.0) * ka_ref[...])
    qa = kk * a
    pt_s[...] = (-kk * jnp.exp(cum - lw)).astype(BF16)
    qt_s[...] = (qa * e_neg).astype(BF16)
    kt_s[...] = (k2 * e_neg).astype(BF16)
    qh_s[...] = (qa * e_hat).astype(BF16)
    kh_s[...] = (k2 * e_hat).astype(BF16)
    p_conv = project(c_conv)
    r = shift_mix(p_r, c_r)
    rt_s[...] = (r * jnp.exp(cum)).astype(BF16)
    v = shift_mix(p_v, c_v)
    v_s[...] = v.astype(BF16)
    bonus = _head_sums(r * k2 * rk_ref[...], hm) * v

    zc_ref[CONV_HALO:CONV_HALO + tb, :] = p_conv[:, 0:CONV_WIDTH] * _sigmoid(
        p_conv[:, CONV_WIDTH:2 * CONV_WIDTH])

    ri = _iota((gw, gw), 0)
    ci = _iota((gw, gw), 1)
    same = (ri >> 6) == (ci >> 6)
    strict = same & (ri > ci)
    ri2 = _iota((gw, 2 * gw), 0)
    ci2 = _iota((gw, 2 * gw), 1) & (gw - 1)
    incl2 = ((ri2 >> 6) == (ci2 >> 6)) & (ri2 >= ci2)
    eye = jnp.where(ri == ci, 1.0, 0.0)

    zero_bf = jnp.zeros((gw, gw), BF16)

    def bd(xs):
        return jnp.where(same, jnp.concatenate([xs] * RWKV_GROUP, axis=0), zero_bf)

    def aligned(i, m):
        return i if isinstance(i, int) else pl.multiple_of(i, m)

    def local_stages(ci_):
        probs = [(cc, j) for cc in range(LOCAL_CHUNKS) for j in range(n_groups)]
        idx = range(len(probs))
        base = aligned(ci_ * (LOCAL_CHUNKS * RWKV_CHUNK), LOCAL_CHUNKS * RWKV_CHUNK)

        def blk(ref, cc, j):
            return bd(ref[pl.ds(base + cc * RWKV_CHUNK, RWKV_CHUNK), j * gw:(j + 1) * gw])

        def slot(cc, j):
            return (ci_ * LOCAL_CHUNKS + cc) * n_groups + j

        P2 = [blk(pt_s, cc, j) for cc, j in probs]
        PR = [jnp.concatenate([P2[i], blk(rt_s, cc, j)], axis=0) for i, (cc, j) in enumerate(probs)]
        QK = [jnp.concatenate([blk(qt_s, cc, j), blk(kt_s, cc, j)], axis=0) for cc, j in probs]
        V2 = [blk(v_s, cc, j) for cc, j in probs]
        sc = [_dot_nt(PR[i], QK[i]) for i in idx]
        yield sc[-1][0:8, 0:LANE]
        A_pq = [jnp.where(strict, sc[i][0:gw, 0:gw], 0.0) for i in idx]
        A_pk = [jnp.where(strict, sc[i][0:gw, gw:2 * gw], 0.0).astype(BF16) for i in idx]
        for i, (cc, j) in enumerate(probs):
            ark_s[slot(cc, j)] = jnp.where(incl2, sc[i][gw:2 * gw, :], 0.0).astype(BF16)
        Xb = [A_pq[i].astype(BF16) for i in idx]
        Xp = [_dot(Xb[i], Xb[i]) for i in idx]
        av = [_dot(A_pk[i], V2[i]) for i in idx]
        yield av[-1][0:8, 0:LANE]
        av = [av[i].astype(BF16) for i in idx]
        Tm = [eye + A_pq[i] for i in idx]
        for step in range(4):
            Xb = [Xp[i].astype(BF16) for i in idx]
            XT = [_dot(Xb[i], jnp.concatenate([Xb[i], Tm[i].astype(BF16)], axis=1)) for i in idx]
            yield XT[-1][0:8, 0:LANE]
            Xp = [XT[i][:, 0:gw] for i in idx]
            Tm = [Tm[i] + XT[i][:, gw:2 * gw] for i in idx]
        XT = [_dot(Xp[i].astype(BF16), Tm[i].astype(BF16)) for i in idx]
        yield XT[-1][0:8, 0:LANE]
        Tm = [(Tm[i] + XT[i]).astype(BF16) for i in idx]
        uw = [_dot(Tm[i], jnp.concatenate([av[i], P2[i]], axis=1)) for i in idx]
        for i, (cc, j) in enumerate(probs):
            uw_s[slot(cc, j)] = uw[i].astype(BF16)
        yield uw[-1][0:8, 0:LANE]

    def state_stages(ci_):
        groups = range(n_groups)
        sls = [slice(j * gw, (j + 1) * gw) for j in groups]
        for cc in range(LOCAL_CHUNKS):
            c = ci_ * LOCAL_CHUNKS + cc
            rows = pl.ds(aligned(c * RWKV_CHUNK, RWKV_CHUNK), RWKV_CHUNK)
            e_last = gl_s[c][0:1, :]
            R2 = [bd(rt_s[rows, s]) for s in sls]
            uw = [uw_s[c * n_groups + j] for j in groups]
            Z = [z_s[j] for j in groups]
            hw = [_dot_nt(jnp.concatenate([uw[j][:, gw:2 * gw], R2[j]], axis=0), Z[j].astype(BF16))
                  for j in groups]
            yield
            QK = [jnp.concatenate([bd(qh_s[rows, s]), bd(kh_s[rows, s])], axis=0) for s in sls]
            V2 = [bd(v_s[rows, s]) for s in sls]
            UV = [jnp.concatenate([(uw[j][:, 0:gw].astype(F32) + hw[j][0:gw]).astype(BF16), V2[j]],
                                  axis=0) for j in groups]
            zn = [_dot_tn(UV[j], QK[j]) for j in groups]
            Y2 = [hw[j][gw:2 * gw] + _dot(ark_s[c * n_groups + j], UV[j]) for j in groups]
            for j in groups:
                z_s[j] = Z[j] * e_last[:, sls[j]] + zn[j]
                y = Y2[j][0:RWKV_CHUNK]
                for e in range(1, RWKV_GROUP):
                    y = y + Y2[j][e * RWKV_CHUNK:(e + 1) * RWKV_CHUNK]
                y_s[rows, sls[j]] = y
            yield

    lane_blocks = CONV_WIDTH // LANE

    conv_tiles = [(r, lb) for r in range(0, LOCAL_CHUNKS * RWKV_CHUNK, CONV_TILE_ROWS)
                  for lb in range(lane_blocks)]
    tiles_per_stage = len(conv_tiles) // LOCAL_STAGES

    def conv_after(ci_, k, after):
        for r, lb in conv_tiles[k * tiles_per_stage:(k + 1) * tiles_per_stage]:
            lanes = slice(lb * LANE, (lb + 1) * LANE)
            r0 = aligned(ci_ * (LOCAL_CHUNKS * RWKV_CHUNK) + r, CONV_TILE_ROWS)
            cv_s[pl.ds(r0, CONV_TILE_ROWS), lanes] = conv_tile(r0, CONV_TILE_ROWS, lanes, after)

    n_groups_t = n_chunks // LOCAL_CHUNKS
    for k, after in enumerate(local_stages(0)):
        conv_after(0, k, after)

    def pipelined(i, carry):
        g_local, g_state = local_stages(i), state_stages(i - 1)
        k = 0
        for which in PIPELINE_ORDER:
            if which == "L":
                conv_after(i, k, next(g_local))
                k += 1
            else:
                next(g_state)
        return carry

    lax.fori_loop(1, n_groups_t, pipelined, 0)
    for _ in state_stages(n_groups_t - 1):
        pass

    o_d = _silu(_layernorm(cv_s[...], clg_ref[...], clb_ref[...], LN_EPS))

    y = y_s[...]
    inv_n = 1.0 / RWKV_HEAD
    mu_y = _head_sums(y, hm) * inv_n
    yc = y - mu_y
    var_y = _head_sums(yc * yc, hm) * inv_n
    y = yc * lax.rsqrt(var_y + RWKV_GN_EPS) * gng_ref[...] + gnb_ref[...] + bonus
    o_c = y * gate

    m = _dot(o_c.astype(BF16), wout_ref[0:512, :]) + _dot(o_d.astype(BF16), wout_ref[512:1024, :])
    o_ref[...] = x + m


def _odd_layer(x, norm_g, w_in, w_out, mu, w0, w2, a0, a2, g2, k_k, k_a, r_k, gn_g, gn_b,
               conv_w, conv_b, cln_g, cln_b, *, tb):
    B, T, D = x.shape
    pad_lr = 256 - (RWKV_DECAY_RANK + RWKV_AAA_RANK + RWKV_GATE_RANK)
    w_in = w_in.astype(BF16)
    w_in_r = jnp.concatenate(
        [w_in[:, :RWKV_IN], jnp.zeros((D, pad_lr), BF16), w_in[:, RWKV_IN:]], axis=1)
    mu_r = jnp.concatenate([mu.astype(F32), jnp.zeros((pad_lr,), F32)]).reshape(1, RWKV_COLS)
    wlr = jnp.zeros((256, 3 * RWKV_WIDTH), F32)
    wlr = wlr.at[0:32, 0:512].set(w2.astype(F32))
    wlr = wlr.at[32:64, 512:1024].set(a2.astype(F32))
    wlr = wlr.at[64:160, 1024:1536].set(g2.astype(F32))
    row = lambda a: a.astype(F32).reshape(1, -1)
    const = lambda shape: pl.BlockSpec(shape, lambda b, t: (0,) * len(shape),
                                       pipeline_mode=pl.Buffered(1))
    gw = RWKV_GROUP * RWKV_HEAD
    kern = functools.partial(_odd_kernel, tb=tb)
    return pl.pallas_call(
        kern,
        grid=(B, T // tb),
        in_specs=[
            pl.BlockSpec((None, tb, D), lambda b, t: (b, t, 0)),
            const((1, D)), const((D, ODD_COLS)), const((1, RWKV_COLS)), const((256, 3 * RWKV_WIDTH)),
            const((1, 512)), const((1, 512)), const((1, 512)), const((1, 512)), const((1, 512)),
            const((1, 512)), const((1, 512)),
            const((CONV_KERNEL, CONV_WIDTH)), const((1, 512)), const((1, 512)), const((1, 512)),
            const((D, D)),
        ],
        out_specs=pl.BlockSpec((None, tb, D), lambda b, t: (b, t, 0)),
        out_shape=jax.ShapeDtypeStruct((B, T, D), F32),
        scratch_shapes=[
            pltpu.VMEM((8, RWKV_COLS), F32),
            pltpu.VMEM((tb + CONV_HALO + 8, CONV_WIDTH), F32),
            pltpu.VMEM((tb, 512), F32),
            pltpu.VMEM((tb, 512), F32), pltpu.VMEM((tb, 512), F32),
            pltpu.VMEM((tb, 512), F32),
            pltpu.VMEM((tb, 512), BF16), pltpu.VMEM((tb, 512), BF16), pltpu.VMEM((tb, 512), BF16),
            pltpu.VMEM((tb, 512), BF16), pltpu.VMEM((tb, 512), BF16), pltpu.VMEM((tb, 512), BF16),
            pltpu.VMEM((tb, 512), BF16),
            pltpu.VMEM((RWKV_WIDTH // gw, gw, gw), F32),
            pltpu.VMEM((tb // RWKV_CHUNK, 8, RWKV_WIDTH), F32),
            pltpu.VMEM((tb // RWKV_CHUNK * (RWKV_WIDTH // gw), gw, 2 * gw), BF16),
            pltpu.VMEM((tb // RWKV_CHUNK * (RWKV_WIDTH // gw), gw, 2 * gw), BF16),
        ],
        compiler_params=pltpu.CompilerParams(
            dimension_semantics=("arbitrary", "arbitrary"), vmem_limit_bytes=VMEM_LIMIT),
        name="odd_mixer",
    )(x, row(norm_g), w_in_r, mu_r, wlr.astype(BF16), row(w0), row(a0), row(k_k), row(k_a), row(r_k),
      row(gn_g), row(gn_b), conv_w.astype(F32), row(conv_b), row(cln_g), row(cln_b), w_out.astype(BF16))


def _mlp_kernel(x_ref, g_ref, wup_ref, wdn_ref, gf_ref, o_ref, *, ff_chunk, final_norm):
    x = x_ref[...]
    xb = _rms(x, g_ref[...], RMS_EPS).astype(BF16)
    acc = x
    d_ff = wup_ref.shape[1]
    for f0 in range(0, d_ff, ff_chunk):
        hcol = _dot(xb, wup_ref[:, f0:f0 + ff_chunk])
        hcol = jnp.square(jnp.maximum(hcol, 0.0))
        acc = acc + _dot(hcol.astype(BF16), wdn_ref[f0:f0 + ff_chunk, :])
    if final_norm:
        acc = _rms(acc, gf_ref[...], RMS_EPS)
    o_ref[...] = acc


def _mlp_layer(x2, norm_g, w_up, w_down, final_g, *, layer, tm, final_norm):
    M, D = x2.shape
    d_ff = w_up.shape[2]
    kern = functools.partial(_mlp_kernel, ff_chunk=1024, final_norm=final_norm)
    return pl.pallas_call(
        kern,
        grid=(M // tm,),
        in_specs=[
            pl.BlockSpec((tm, D), lambda i: (i, 0)),
            pl.BlockSpec((1, D), lambda i: (0, 0)),
            pl.BlockSpec((None, D, d_ff), lambda i: (layer, 0, 0), pipeline_mode=pl.Buffered(1)),
            pl.BlockSpec((None, d_ff, D), lambda i: (layer, 0, 0), pipeline_mode=pl.Buffered(1)),
            pl.BlockSpec((1, D), lambda i: (0, 0)),
        ],
        out_specs=pl.BlockSpec((tm, D), lambda i: (i, 0)),
        out_shape=jax.ShapeDtypeStruct((M, D), F32),
        compiler_params=pltpu.CompilerParams(
            dimension_semantics=("arbitrary",), vmem_limit_bytes=VMEM_LIMIT),
        name="mlp_final" if final_norm else "mlp",
    )(x2, norm_g.astype(F32).reshape(1, D), w_up, w_down, final_g.astype(F32).reshape(1, D))


def _pick_block(n, target):
    b = min(n, target)
    while n % b:
        b //= 2
    return b


def kernel(x, norm_mix, norm_ffn, w_up, w_down, norm_final, even_w_in, even_w_out, gla_w_alpha2, gla_b_alpha, gla_norm, sgu_ln_g, sgu_ln_b, sgu_w, sgu_b, odd_w_in, odd_w_out, rwkv_mu, rwkv_w0, rwkv_w2, rwkv_a0, rwkv_a2, rwkv_g2, rwkv_k_k, rwkv_k_a, rwkv_r_k, rwkv_gn_g, rwkv_gn_b, conv_w, conv_b, conv_ln_g, conv_ln_b):
    B, T, D = x.shape
    depth = norm_mix.shape[0]
    tb_even = _pick_block(T, EVEN_BLOCK)
    tb_odd = _pick_block(T, ODD_BLOCK)
    tm = _pick_block(B * T, MLP_BLOCK)
    assert tb_even % (GLA_LOCAL * GLA_CHUNK) == 0 and tb_even % SGU_CHUNK == 0
    assert tb_odd % (LOCAL_CHUNKS * RWKV_CHUNK) == 0 and D == 1024
    w_up_b, w_down_b = w_up.astype(BF16), w_down.astype(BF16)
    for layer in range(depth):
        j = layer // 2
        if layer % 2 == 0:
            x = _even_layer(x, norm_mix[layer], even_w_in[j], even_w_out[j], gla_w_alpha2[j],
                            gla_b_alpha[j], gla_norm[j], sgu_ln_g[j], sgu_ln_b[j], sgu_w[j], sgu_b[j],
                            tb=tb_even)
        else:
            x = _odd_layer(x, norm_mix[layer], odd_w_in[j], odd_w_out[j], rwkv_mu[j], rwkv_w0[j],
                           rwkv_w2[j], rwkv_a0[j], rwkv_a2[j], rwkv_g2[j], rwkv_k_k[j], rwkv_k_a[j],
                           rwkv_r_k[j].reshape(-1), rwkv_gn_g[j], rwkv_gn_b[j], conv_w[j], conv_b[j],
                           conv_ln_g[j], conv_ln_b[j], tb=tb_odd)
        last = layer == depth - 1
        x = _mlp_layer(x.reshape(B * T, D), norm_ffn[layer], w_up_b, w_down_b, norm_final,
                       layer=layer, tm=tm, final_norm=last).reshape(B, T, D)
    return x
```

```python
import functools

import jax
import jax.numpy as jnp
from jax import lax
from jax.experimental import pallas as pl
from jax.experimental.pallas import tpu as pltpu

F32 = jnp.float32
BF16 = jnp.bfloat16

RMS_EPS = 1e-6
LN_EPS = 1e-5

GLA_HEADS = 4
GLA_DK = 64
GLA_DV = 128
GLA_KEY = GLA_HEADS * GLA_DK
GLA_WIDTH = GLA_HEADS * GLA_DV
GLA_GATE_RANK = 16
GLA_TAU = 16.0
GLA_CHUNK = 64
GLA_LOCAL = 8
SGU_WIDTH = 512
SGU_GROUPS = 4
SGU_CHUNK = 128
EVEN_COLS = 2688

RWKV_WIDTH = 512
RWKV_HEAD = 64
RWKV_HEADS = 8
RWKV_DECAY_RANK = 32
RWKV_AAA_RANK = 32
RWKV_GATE_RANK = 96
RWKV_GN_EPS = 64e-5
RWKV_IN = 1696
RWKV_COLS = 1792
RWKV_CHUNK = 64
RWKV_GROUP = 2
LOCAL_CHUNKS = 2
LOCAL_STAGES = 8
PIPELINE_ORDER = "LSLLSLSLLSLL"
CONV_WIDTH = 512
CONV_KERNEL = 31
CONV_HALO = 32
CONV_TILE_ROWS = 32
ODD_COLS = RWKV_COLS + 2 * CONV_WIDTH

LANE = 128
VMEM_LIMIT = 56 * 1024 * 1024
EVEN_BLOCK = 1024
ODD_BLOCK = 512
MLP_BLOCK = 1024


def _iota(shape, dim):
    return lax.broadcasted_iota(jnp.int32, shape, dim)


def _dot(a, b):
    return jnp.dot(a, b, preferred_element_type=F32)


def _dot_nt(a, b):
    return lax.dot_general(a, b, (((1,), (1,)), ((), ())), preferred_element_type=F32)


def _dot_tn(a, b):
    return lax.dot_general(a, b, (((0,), (0,)), ((), ())), preferred_element_type=F32)


def _split2(x):
    hi = x.astype(BF16)
    lo = (x - hi.astype(F32)).astype(BF16)
    return hi, lo


def _chunk_cumsum(tril2, x):
    hi, lo = _split2(x)
    cum = _dot(tril2, jnp.concatenate([hi, lo], axis=0))
    n = x.shape[0]
    return cum, jnp.broadcast_to(cum[n - 1:n], cum.shape)


def _head_sums(x, hm_half):
    half = hm_half.shape[0]
    xb = x.astype(BF16)
    return jnp.concatenate([_dot(xb[:, 0:half], hm_half), _dot(xb[:, half:2 * half], hm_half)], axis=1)


def _rms(x, g, eps):
    return x * lax.rsqrt(jnp.mean(x * x, axis=-1, keepdims=True) + eps) * g


def _layernorm(x, g, b, eps):
    mu = jnp.mean(x, axis=-1, keepdims=True)
    xc = x - mu
    var = jnp.mean(xc * xc, axis=-1, keepdims=True)
    return xc * lax.rsqrt(var + eps) * g + b


def _sigmoid(x):
    return 1.0 / (1.0 + jnp.exp(-x))


def _silu(x):
    return x * _sigmoid(x)


def _softplus(x):
    return jnp.maximum(x, 0.0) + jnp.log(1.0 + jnp.exp(-jnp.abs(x)))


def _gelu_tanh(x):
    c = 0.7978845608028654
    return 0.5 * x * (1.0 + jnp.tanh(c * (x + 0.044715 * (x * x * x))))


def _tril_ones(n, dtype):
    return jnp.where(_iota((n, n), 0) >= _iota((n, n), 1), 1.0, 0.0).astype(dtype)


def _even_kernel(x_ref, gn_ref, win_ref, wal_ref, bal_ref, glan_ref, lng_ref, lnb_ref,
                 sguw_ref, sgub_ref, wout_ref, o_ref,
                 la_ref, bl_ref, dl_ref, qd_ref, kd_ref, ks_ref, vb_ref, gt_ref, oa_ref, ob_ref, st_ref,
                 *, tb):
    t_idx = pl.program_id(1)

    @pl.when(t_idx == 0)
    def _():
        st_ref[...] = jnp.zeros_like(st_ref)

    x = x_ref[...]
    h = _rms(x, gn_ref[...], RMS_EPS).astype(BF16)

    def project(c0, c1):
        return _dot(h, win_ref[:, c0:c1])

    p_alr = project(2560, 2688)
    p_sv = project(2048, 2560)
    a_hi, a_lo = _split2(p_alr)
    w_hi, w_lo = _split2(wal_ref[...])
    z = _dot(a_hi, w_hi) + _dot(a_hi, w_lo) + _dot(a_lo, w_hi) + bal_ref[...]
    la = -_softplus(-z) * (1.0 / GLA_TAU)
    p_u = project(1536, 2048)
    sv = _layernorm(_gelu_tanh(p_sv), lng_ref[...], lnb_ref[...], LN_EPS).astype(BF16)

    n_chunks = tb // GLA_CHUNK
    tril2 = jnp.concatenate([_tril_ones(GLA_CHUNK, BF16)] * 2, axis=1)
    for c in range(n_chunks):
        cum_c, last_c = _chunk_cumsum(tril2, la[c * GLA_CHUNK:(c + 1) * GLA_CHUNK])
        la_ref[c * GLA_CHUNK:(c + 1) * GLA_CHUNK, :] = cum_c
        bl_ref[c * GLA_CHUNK:(c + 1) * GLA_CHUNK, :] = last_c
    p_qk = project(0, 512)
    ug = _gelu_tanh(p_u)

    tri128 = _iota((SGU_CHUNK, SGU_CHUNK), 0) >= _iota((SGU_CHUNK, SGU_CHUNK), 1)
    for gg in range(SGU_GROUPS):
        wg = jnp.where(tri128, sguw_ref[gg], 0.0).astype(BF16)
        for c in range(tb // SGU_CHUNK):
            r0 = c * SGU_CHUNK
            s = _dot(wg, sv[r0:r0 + SGU_CHUNK, gg * LANE:(gg + 1) * LANE])
            s = s + sgub_ref[:, gg * LANE:(gg + 1) * LANE]
            ob_ref[r0:r0 + SGU_CHUNK, gg * LANE:(gg + 1) * LANE] = (
                ug[r0:r0 + SGU_CHUNK, gg * LANE:(gg + 1) * LANE] * s)

    p_v = project(512, 1024)
    b = la_ref[...]
    b_last = bl_ref[...]
    for c in range(n_chunks):
        dl_ref[c] = jnp.exp(b_last[c * GLA_CHUNK:c * GLA_CHUNK + 8])
    k = p_qk[:, 256:512]
    qd_ref[...] = (p_qk[:, 0:256] * (GLA_DK ** -0.5) * jnp.exp(b)).astype(BF16)
    kd_ref[...] = (k * jnp.exp(-b)).astype(BF16)
    ks_ref[...] = (k * jnp.exp(b_last - b)).astype(BF16)
    p_g = project(1024, 1536)
    vb_ref[...] = p_v.astype(BF16)
    gt_ref[...] = glan_ref[...] * _silu(p_g)

    rk = _iota((GLA_KEY, GLA_KEY), 0) >> 6
    ck = _iota((GLA_KEY, GLA_KEY), 1) >> 6
    k4_mask = rk == ck
    rv = _iota((GLA_KEY, GLA_WIDTH), 0) >> 6
    cv = _iota((GLA_KEY, GLA_WIDTH), 1) >> 7
    vbd_mask = rv == cv
    causal = _iota((GLA_CHUNK, GLA_KEY), 0) >= (_iota((GLA_CHUNK, GLA_KEY), 1) & 63)
    rs = _iota((GLA_WIDTH, GLA_KEY), 0) >> 7
    cs = _iota((GLA_WIDTH, GLA_KEY), 1) >> 6
    st_mask = rs == cs

    zero_k = jnp.zeros((GLA_KEY, GLA_KEY), BF16)
    zero_v = jnp.zeros((GLA_KEY, GLA_WIDTH), BF16)

    def gla_step(ci_, carry):
        base = pl.multiple_of(ci_ * (GLA_LOCAL * GLA_CHUNK), GLA_LOCAL * GLA_CHUNK)
        ccs = range(GLA_LOCAL)
        rows = [pl.ds(base + cc * GLA_CHUNK, GLA_CHUNK) for cc in ccs]
        q_dec = [qd_ref[r, :] for r in rows]
        k4 = [jnp.where(k4_mask, jnp.concatenate([kd_ref[r, :]] * GLA_HEADS, axis=0), zero_k)
              for r in rows]
        v = [vb_ref[r, :] for r in rows]
        v_bd = [jnp.where(vbd_mask, jnp.concatenate([v[cc]] * GLA_HEADS, axis=0), zero_v) for cc in ccs]
        att = [_dot_nt(q_dec[cc], k4[cc]) for cc in ccs]
        kv = [_dot_tn(v[cc], ks_ref[rows[cc], :]) for cc in ccs]
        att = [jnp.where(causal, att[cc], 0.0).astype(BF16) for cc in ccs]
        o_intra = [_dot(att[cc], v_bd[cc]) for cc in ccs]
        st = st_ref[...]
        for cc in ccs:
            o = o_intra[cc] + _dot_nt(q_dec[cc], st.astype(BF16))
            dl = dl_ref[ci_ * GLA_LOCAL + cc][0:1, :]
            st = st * dl + jnp.where(st_mask, kv[cc], 0.0)
            for hh in range(GLA_HEADS):
                oh = o[:, hh * GLA_DV:(hh + 1) * GLA_DV]
                oh = oh * lax.rsqrt(jnp.mean(oh * oh, axis=-1, keepdims=True) + RMS_EPS)
                oa_ref[rows[cc], hh * GLA_DV:(hh + 1) * GLA_DV] = oh
        st_ref[...] = st
        return carry

    lax.fori_loop(0, n_chunks // GLA_LOCAL, gla_step, 0)

    o_a = oa_ref[...] * gt_ref[...]
    m = _dot(o_a.astype(BF16), wout_ref[0:512, :]) + _dot(ob_ref[...].astype(BF16), wout_ref[512:1024, :])
    o_ref[...] = x + m


def _even_layer(x, norm_g, w_in, w_out, w_alpha2, b_alpha, gla_norm, ln_g, ln_b, sgu_w, sgu_b, *, tb):
    B, T, D = x.shape
    w_in = w_in.astype(BF16)
    w_in_r = jnp.concatenate(
        [w_in[:, :1536], w_in[:, 1552:2576], w_in[:, 1536:1552],
         jnp.zeros((D, EVEN_COLS - 2576), BF16)], axis=1)
    wal = jnp.zeros((LANE, GLA_KEY), F32).at[:GLA_GATE_RANK].set(w_alpha2.astype(F32))
    sgub = jnp.repeat(sgu_b.astype(F32).T, LANE, axis=1)
    row = lambda a: a.astype(F32).reshape(1, -1)
    const = lambda shape: pl.BlockSpec(shape, lambda b, t: (0,) * len(shape),
                                       pipeline_mode=pl.Buffered(1))
    kern = functools.partial(_even_kernel, tb=tb)
    return pl.pallas_call(
        kern,
        grid=(B, T // tb),
        in_specs=[
            pl.BlockSpec((None, tb, D), lambda b, t: (b, t, 0)),
            const((1, D)), const((D, EVEN_COLS)), const((LANE, GLA_KEY)), const((1, GLA_KEY)),
            const((1, GLA_WIDTH)), const((1, SGU_WIDTH)), const((1, SGU_WIDTH)),
            const((SGU_GROUPS, SGU_CHUNK, SGU_CHUNK)), const((SGU_CHUNK, SGU_WIDTH)),
            const((D, D)),
        ],
        out_specs=pl.BlockSpec((None, tb, D), lambda b, t: (b, t, 0)),
        out_shape=jax.ShapeDtypeStruct((B, T, D), F32),
        scratch_shapes=[
            pltpu.VMEM((tb, GLA_KEY), F32),
            pltpu.VMEM((tb, GLA_KEY), F32),
            pltpu.VMEM((tb // GLA_CHUNK, 8, GLA_KEY), F32),
            pltpu.VMEM((tb, GLA_KEY), BF16), pltpu.VMEM((tb, GLA_KEY), BF16),
            pltpu.VMEM((tb, GLA_KEY), BF16),
            pltpu.VMEM((tb, GLA_WIDTH), BF16),
            pltpu.VMEM((tb, GLA_WIDTH), F32),
            pltpu.VMEM((tb, GLA_WIDTH), F32),
            pltpu.VMEM((tb, SGU_WIDTH), F32),
            pltpu.VMEM((GLA_WIDTH, GLA_KEY), F32),
        ],
        compiler_params=pltpu.CompilerParams(
            dimension_semantics=("arbitrary", "arbitrary"), vmem_limit_bytes=VMEM_LIMIT),
        name="even_mixer",
    )(x, row(norm_g), w_in_r, wal, row(b_alpha), row(gla_norm), row(ln_g), row(ln_b),
      sgu_w.astype(F32), sgub, w_out.astype(BF16))


def _odd_kernel(x_ref, gn_ref, win_ref, mu_ref, wlr_ref, w0_ref, a0_ref, kk_ref, ka_ref, rk_ref,
                gng_ref, gnb_ref, cw_ref, cb_ref, clg_ref, clb_ref, wout_ref, o_ref,
                prev_ref, zc_ref, cv_s, cum_s, last_s, y_s, rt_s, pt_s, qt_s, kt_s, qh_s, kh_s, v_s,
                z_s, gl_s, ark_s, uw_s, *, tb):
    t_idx = pl.program_id(1)
    gw = RWKV_GROUP * RWKV_HEAD
    n_groups = RWKV_WIDTH // gw
    n_chunks = tb // RWKV_CHUNK

    @pl.when(t_idx == 0)
    def _():
        prev_ref[...] = jnp.zeros_like(prev_ref)
        zc_ref[0:CONV_HALO, :] = jnp.zeros((CONV_HALO, CONV_WIDTH), F32)
        zc_ref[tb + CONV_HALO:tb + CONV_HALO + 8, :] = jnp.zeros((8, CONV_WIDTH), F32)
        z_s[...] = jnp.zeros_like(z_s)

    @pl.when(t_idx != 0)
    def _():
        zc_ref[0:CONV_HALO, :] = zc_ref[tb:tb + CONV_HALO, :]

    x = x_ref[...]
    h = _rms(x, gn_ref[...], RMS_EPS).astype(BF16)
    c_r, c_k, c_v = slice(0, 512), slice(512, 1024), slice(1024, 1536)
    c_lr = slice(3 * RWKV_WIDTH, RWKV_COLS)
    c_conv = slice(RWKV_COLS, ODD_COLS)

    def project(cols):
        return _dot(h, win_ref[:, cols])

    def shift_mix(pr, cols):
        sh = pltpu.roll(pr, 1, 0)
        head = jnp.where(_iota((8, pr.shape[1]), 0) == 0, prev_ref[7:8, cols], sh[0:8])
        prev_ref[:, cols] = pr[tb - 8:tb]
        sh = jnp.concatenate([head, sh[8:]], axis=0)
        return pr + (sh - pr) * mu_ref[:, cols]

    never = jnp.full((CONV_TILE_ROWS, LANE), t_idx, jnp.int32) < 0

    def conv_tile(r0, n, lanes, after):
        first = CONV_HALO - (CONV_KERNEL - 1)
        wn = n + CONV_HALO + 8
        win = zc_ref[pl.ds(r0, wn), lanes]
        acc = jnp.where(never, jnp.concatenate([after] * (n // 8), axis=0),
                        jnp.zeros((n, LANE), F32) + cb_ref[:, lanes])
        for ph in range(8):
            sh = win if ph == 0 else pltpu.roll(win, wn - ph, 0)
            for j in range(CONV_KERNEL):
                if (j + first) % 8 == ph:
                    a8 = (j + first) // 8 * 8
                    acc = acc + sh[a8:a8 + n] * cw_ref[j:j + 1, lanes]
        return acc

    hm = jnp.where((_iota((256, 256), 0) >> 6) == (_iota((256, 256), 1) >> 6),
                   1.0, 0.0).astype(BF16)
    p_lr = project(c_lr)
    p_k = project(c_k)
    lr = shift_mix(p_lr, c_lr)
    lane = _iota((tb, 256), 1)
    f = jnp.where(lane < RWKV_DECAY_RANK, jnp.tanh(lr),
                  jnp.where(lane < RWKV_DECAY_RANK + RWKV_AAA_RANK, lr, _sigmoid(lr)))
    lo3 = _dot(f.astype(BF16), wlr_ref[...])
    p_r = project(c_r)
    k = shift_mix(p_k, c_k)
    kk = k * kk_ref[...]
    kk = kk * lax.rsqrt(jnp.maximum(_head_sums(kk * kk, hm), 1e-24))
    lw = (-0.6065306597126334) * _sigmoid(w0_ref[...] + lo3[:, 0:512])
    a = _sigmoid(a0_ref[...] + lo3[:, 512:1024])
    gate = lo3[:, 1024:1536]

    tril2 = jnp.concatenate([_tril_ones(RWKV_CHUNK, BF16)] * 2, axis=1)
    for c in range(n_chunks):
        cum_c, last_c = _chunk_cumsum(tril2, lw[c * RWKV_CHUNK:(c + 1) * RWKV_CHUNK])
        cum_s[c * RWKV_CHUNK:(c + 1) * RWKV_CHUNK, :] = cum_c
        last_s[c * RWKV_CHUNK:(c + 1) * RWKV_CHUNK, :] = last_c
    p_v = project(c_v)
    cum = cum_s[...]
    e_last = jnp.exp(last_s[...])
    for c in range(n_chunks):
        gl_s[c] = e_last[c * RWKV_CHUNK:c * RWKV_CHUNK + 8]
    e_neg = jnp.exp(-cum)
    e_hat = e_last * e_neg
    k2 = k * (1.0 + (a - 1.0) * ka_ref[...])
    qa = kk * a
    pt_s[...] = (-kk * jnp.exp(cum - lw)).astype(BF16)
    qt_s[...] = (qa * e_neg).astype(BF16)
    kt_s[...] = (k2 * e_neg).astype(BF16)
    qh_s[...] = (qa * e_hat).astype(BF16)
    kh_s[...] = (k2 * e_hat).astype(BF16)
    p_conv = project(c_conv)
    r = shift_mix(p_r, c_r)
    rt_s[...] = (r * jnp.exp(cum)).astype(BF16)
    v = shift_mix(p_v, c_v)
    v_s[...] = v.astype(BF16)
    bonus = _head_sums(r * k2 * rk_ref[...], hm) * v

    zc_ref[CONV_HALO:CONV_HALO + tb, :] = p_conv[:, 0:CONV_WIDTH] * _sigmoid(
        p_conv[:, CONV_WIDTH:2 * CONV_WIDTH])

    ri = _iota((gw, gw), 0)
    ci = _iota((gw, gw), 1)
    same = (ri >> 6) == (ci >> 6)
    strict = same & (ri > ci)
    ri2 = _iota((gw, 2 * gw), 0)
    ci2 = _iota((gw, 2 * gw), 1) & (gw - 1)
    incl2 = ((ri2 >> 6) == (ci2 >> 6)) & (ri2 >= ci2)
    eye = jnp.where(ri == ci, 1.0, 0.0)

    zero_bf = jnp.zeros((gw, gw), BF16)

    def bd(xs):
        return jnp.where(same, jnp.concatenate([xs] * RWKV_GROUP, axis=0), zero_bf)

    def aligned(i, m):
        return i if isinstance(i, int) else pl.multiple_of(i, m)

    def local_stages(ci_):
        probs = [(cc, j) for cc in range(LOCAL_CHUNKS) for j in range(n_groups)]
        idx = range(len(probs))
        base = aligned(ci_ * (LOCAL_CHUNKS * RWKV_CHUNK), LOCAL_CHUNKS * RWKV_CHUNK)

        def blk(ref, cc, j):
            return bd(ref[pl.ds(base + cc * RWKV_CHUNK, RWKV_CHUNK), j * gw:(j + 1) * gw])

        def slot(cc, j):
            return (ci_ * LOCAL_CHUNKS + cc) * n_groups + j

        P2 = [blk(pt_s, cc, j) for cc, j in probs]
        PR = [jnp.concatenate([P2[i], blk(rt_s, cc, j)], axis=0) for i, (cc, j) in enumerate(probs)]
        QK = [jnp.concatenate([blk(qt_s, cc, j), blk(kt_s, cc, j)], axis=0) for cc, j in probs]
        V2 = [blk(v_s, cc, j) for cc, j in probs]
        sc = [_dot_nt(PR[i], QK[i]) for i in idx]
        yield sc[-1][0:8, 0:LANE]
        A_pq = [jnp.where(strict, sc[i][0:gw, 0:gw], 0.0) for i in idx]
        A_pk = [jnp.where(strict, sc[i][0:gw, gw:2 * gw], 0.0).astype(BF16) for i in idx]
        for i, (cc, j) in enumerate(probs):
            ark_s[slot(cc, j)] = jnp.where(incl2, sc[i][gw:2 * gw, :], 0.0).astype(BF16)
        Xb = [A_pq[i].astype(BF16) for i in idx]
        Xp = [_dot(Xb[i], Xb[i]) for i in idx]
        av = [_dot(A_pk[i], V2[i]) for i in idx]
        yield av[-1][0:8, 0:LANE]
        av = [av[i].astype(BF16) for i in idx]
        Tm = [eye + A_pq[i] for i in idx]
        for step in range(4):
            Xb = [Xp[i].astype(BF16) for i in idx]
            XT = [_dot(Xb[i], jnp.concatenate([Xb[i], Tm[i].astype(BF16)], axis=1)) for i in idx]
            yield XT[-1][0:8, 0:LANE]
            Xp = [XT[i][:, 0:gw] for i in idx]
            Tm = [Tm[i] + XT[i][:, gw:2 * gw] for i in idx]
        XT = [_dot(Xp[i].astype(BF16), Tm[i].astype(BF16)) for i in idx]
        yield XT[-1][0:8, 0:LANE]
        Tm = [(Tm[i] + XT[i]).astype(BF16) for i in idx]
        uw = [_dot(Tm[i], jnp.concatenate([av[i], P2[i]], axis=1)) for i in idx]
        for i, (cc, j) in enumerate(probs):
            uw_s[slot(cc, j)] = uw[i].astype(BF16)
        yield uw[-1][0:8, 0:LANE]

    def state_stages(ci_):
        groups = range(n_groups)
        sls = [slice(j * gw, (j + 1) * gw) for j in groups]
        for cc in range(LOCAL_CHUNKS):
            c = ci_ * LOCAL_CHUNKS + cc
            rows = pl.ds(aligned(c * RWKV_CHUNK, RWKV_CHUNK), RWKV_CHUNK)
            e_last = gl_s[c][0:1, :]
            R2 = [bd(rt_s[rows, s]) for s in sls]
            uw = [uw_s[c * n_groups + j] for j in groups]
            Z = [z_s[j] for j in groups]
            hw = [_dot_nt(jnp.concatenate([uw[j][:, gw:2 * gw], R2[j]], axis=0), Z[j].astype(BF16))
                  for j in groups]
            yield
            QK = [jnp.concatenate([bd(qh_s[rows, s]), bd(kh_s[rows, s])], axis=0) for s in sls]
            V2 = [bd(v_s[rows, s]) for s in sls]
            UV = [jnp.concatenate([(uw[j][:, 0:gw].astype(F32) + hw[j][0:gw]).astype(BF16), V2[j]],
                                  axis=0) for j in groups]
            zn = [_dot_tn(UV[j], QK[j]) for j in groups]
            Y2 = [hw[j][gw:2 * gw] + _dot(ark_s[c * n_groups + j], UV[j]) for j in groups]
            for j in groups:
                z_s[j] = Z[j] * e_last[:, sls[j]] + zn[j]
                y = Y2[j][0:RWKV_CHUNK]
                for e in range(1, RWKV_GROUP):
                    y = y + Y2[j][e * RWKV_CHUNK:(e + 1) * RWKV_CHUNK]
                y_s[rows, sls[j]] = y
            yield

    lane_blocks = CONV_WIDTH // LANE

    conv_tiles = [(r, lb) for r in range(0, LOCAL_CHUNKS * RWKV_CHUNK, CONV_TILE_ROWS)
                  for lb in range(lane_blocks)]
    tiles_per_stage = len(conv_tiles) // LOCAL_STAGES

    def conv_after(ci_, k, after):
        for r, lb in conv_tiles[k * tiles_per_stage:(k + 1) * tiles_per_stage]:
            lanes = slice(lb * LANE, (lb + 1) * LANE)
            r0 = aligned(ci_ * (LOCAL_CHUNKS * RWKV_CHUNK) + r, CONV_TILE_ROWS)
            cv_s[pl.ds(r0, CONV_TILE_ROWS), lanes] = conv_tile(r0, CONV_TILE_ROWS, lanes, after)

    n_groups_t = n_chunks // LOCAL_CHUNKS
    for k, after in enumerate(local_stages(0)):
        conv_after(0, k, after)

    def pipelined(i, carry):
        g_local, g_state = local_stages(i), state_stages(i - 1)
        k = 0
        for which in PIPELINE_ORDER:
            if which == "L":
                conv_after(i, k, next(g_local))
                k += 1
            else:
                next(g_state)
        return carry

    lax.fori_loop(1, n_groups_t, pipelined, 0)
    for _ in state_stages(n_groups_t - 1):
        pass

    o_d = _silu(_layernorm(cv_s[...], clg_ref[...], clb_ref[...], LN_EPS))

    y = y_s[...]
    inv_n = 1.0 / RWKV_HEAD
    mu_y = _head_sums(y, hm) * inv_n
    yc = y - mu_y
    var_y = _head_sums(yc * yc, hm) * inv_n
    y = yc * lax.rsqrt(var_y + RWKV_GN_EPS) * gng_ref[...] + gnb_ref[...] + bonus
    o_c = y * gate

    m = _dot(o_c.astype(BF16), wout_ref[0:512, :]) + _dot(o_d.astype(BF16), wout_ref[512:1024, :])
    o_ref[...] = x + m


def _odd_layer(x, norm_g, w_in, w_out, mu, w0, w2, a0, a2, g2, k_k, k_a, r_k, gn_g, gn_b,
               conv_w, conv_b, cln_g, cln_b, *, tb):
    B, T, D = x.shape
    pad_lr = 256 - (RWKV_DECAY_RANK + RWKV_AAA_RANK + RWKV_GATE_RANK)
    w_in = w_in.astype(BF16)
    w_in_r = jnp.concatenate(
        [w_in[:, :RWKV_IN], jnp.zeros((D, pad_lr), BF16), w_in[:, RWKV_IN:]], axis=1)
    mu_r = jnp.concatenate([mu.astype(F32), jnp.zeros((pad_lr,), F32)]).reshape(1, RWKV_COLS)
    wlr = jnp.zeros((256, 3 * RWKV_WIDTH), F32)
    wlr = wlr.at[0:32, 0:512].set(w2.astype(F32))
    wlr = wlr.at[32:64, 512:1024].set(a2.astype(F32))
    wlr = wlr.at[64:160, 1024:1536].set(g2.astype(F32))
    row = lambda a: a.astype(F32).reshape(1, -1)
    const = lambda shape: pl.BlockSpec(shape, lambda b, t: (0,) * len(shape),
                                       pipeline_mode=pl.Buffered(1))
    gw = RWKV_GROUP * RWKV_HEAD
    kern = functools.partial(_odd_kernel, tb=tb)
    return pl.pallas_call(
        kern,
        grid=(B, T // tb),
        in_specs=[
            pl.BlockSpec((None, tb, D), lambda b, t: (b, t, 0)),
            const((1, D)), const((D, ODD_COLS)), const((1, RWKV_COLS)), const((256, 3 * RWKV_WIDTH)),
            const((1, 512)), const((1, 512)), const((1, 512)), const((1, 512)), const((1, 512)),
            const((1, 512)), const((1, 512)),
            const((CONV_KERNEL, CONV_WIDTH)), const((1, 512)), const((1, 512)), const((1, 512)),
            const((D, D)),
        ],
        out_specs=pl.BlockSpec((None, tb, D), lambda b, t: (b, t, 0)),
        out_shape=jax.ShapeDtypeStruct((B, T, D), F32),
        scratch_shapes=[
            pltpu.VMEM((8, RWKV_COLS), F32),
            pltpu.VMEM((tb + CONV_HALO + 8, CONV_WIDTH), F32),
            pltpu.VMEM((tb, 512), F32),
            pltpu.VMEM((tb, 512), F32), pltpu.VMEM((tb, 512), F32),
            pltpu.VMEM((tb, 512), F32),
            pltpu.VMEM((tb, 512), BF16), pltpu.VMEM((tb, 512), BF16), pltpu.VMEM((tb, 512), BF16),
            pltpu.VMEM((tb, 512), BF16), pltpu.VMEM((tb, 512), BF16), pltpu.VMEM((tb, 512), BF16),
            pltpu.VMEM((tb, 512), BF16),
            pltpu.VMEM((RWKV_WIDTH // gw, gw, gw), F32),
            pltpu.VMEM((tb // RWKV_CHUNK, 8, RWKV_WIDTH), F32),
            pltpu.VMEM((tb // RWKV_CHUNK * (RWKV_WIDTH // gw), gw, 2 * gw), BF16),
            pltpu.VMEM((tb // RWKV_CHUNK * (RWKV_WIDTH // gw), gw, 2 * gw), BF16),
        ],
        compiler_params=pltpu.CompilerParams(
            dimension_semantics=("arbitrary", "arbitrary"), vmem_limit_bytes=VMEM_LIMIT),
        name="odd_mixer",
    )(x, row(norm_g), w_in_r, mu_r, wlr.astype(BF16), row(w0), row(a0), row(k_k), row(k_a), row(r_k),
      row(gn_g), row(gn_b), conv_w.astype(F32), row(conv_b), row(cln_g), row(cln_b), w_out.astype(BF16))


def _mlp_kernel(x_ref, g_ref, wup_ref, wdn_ref, gf_ref, o_ref, *, ff_chunk, final_norm):
    x = x_ref[...]
    xb = _rms(x, g_ref[...], RMS_EPS).astype(BF16)
    acc = x
    d_ff = wup_ref.shape[1]
    for f0 in range(0, d_ff, ff_chunk):
        hcol = _dot(xb, wup_ref[:, f0:f0 + ff_chunk])
        hcol = jnp.square(jnp.maximum(hcol, 0.0))
        acc = acc + _dot(hcol.astype(BF16), wdn_ref[f0:f0 + ff_chunk, :])
    if final_norm:
        acc = _rms(acc, gf_ref[...], RMS_EPS)
    o_ref[...] = acc


def _mlp_layer(x2, norm_g, w_up, w_down, final_g, *, layer, tm, final_norm):
    M, D = x2.shape
    d_ff = w_up.shape[2]
    kern = functools.partial(_mlp_kernel, ff_chunk=1024, final_norm=final_norm)
    return pl.pallas_call(
        kern,
        grid=(M // tm,),
        in_specs=[
            pl.BlockSpec((tm, D), lambda i: (i, 0)),
            pl.BlockSpec((1, D), lambda i: (0, 0)),
            pl.BlockSpec((None, D, d_ff), lambda i: (layer, 0, 0), pipeline_mode=pl.Buffered(1)),
            pl.BlockSpec((None, d_ff, D), lambda i: (layer, 0, 0), pipeline_mode=pl.Buffered(1)),
            pl.BlockSpec((1, D), lambda i: (0, 0)),
        ],
        out_specs=pl.BlockSpec((tm, D), lambda i: (i, 0)),
        out_shape=jax.ShapeDtypeStruct((M, D), F32),
        compiler_params=pltpu.CompilerParams(
            dimension_semantics=("arbitrary",), vmem_limit_bytes=VMEM_LIMIT),
        name="mlp_final" if final_norm else "mlp",
    )(x2, norm_g.astype(F32).reshape(1, D), w_up, w_down, final_g.astype(F32).reshape(1, D))


def _pick_block(n, target):
    b = min(n, target)
    while n % b:
        b //= 2
    return b


def kernel(x, norm_mix, norm_ffn, w_up, w_down, norm_final, even_w_in, even_w_out, gla_w_alpha2, gla_b_alpha, gla_norm, sgu_ln_g, sgu_ln_b, sgu_w, sgu_b, odd_w_in, odd_w_out, rwkv_mu, rwkv_w0, rwkv_w2, rwkv_a0, rwkv_a2, rwkv_g2, rwkv_k_k, rwkv_k_a, rwkv_r_k, rwkv_gn_g, rwkv_gn_b, conv_w, conv_b, conv_ln_g, conv_ln_b):
    B, T, D = x.shape
    depth = norm_mix.shape[0]
    tb_even = _pick_block(T, EVEN_BLOCK)
    tb_odd = _pick_block(T, ODD_BLOCK)
    tm = _pick_block(B * T, MLP_BLOCK)
    assert tb_even % (GLA_LOCAL * GLA_CHUNK) == 0 and tb_even % SGU_CHUNK == 0
    assert tb_odd % (LOCAL_CHUNKS * RWKV_CHUNK) == 0 and D == 1024
    w_up_b, w_down_b = w_up.astype(BF16), w_down.astype(BF16)
    for layer in range(depth):
        j = layer // 2
        if layer % 2 == 0:
            x = _even_layer(x, norm_mix[layer], even_w_in[j], even_w_out[j], gla_w_alpha2[j],
                            gla_b_alpha[j], gla_norm[j], sgu_ln_g[j], sgu_ln_b[j], sgu_w[j], sgu_b[j],
                            tb=tb_even)
        else:
            x = _odd_layer(x, norm_mix[layer], odd_w_in[j], odd_w_out[j], rwkv_mu[j], rwkv_w0[j],
                           rwkv_w2[j], rwkv_a0[j], rwkv_a2[j], rwkv_g2[j], rwkv_k_k[j], rwkv_k_a[j],
                           rwkv_r_k[j].reshape(-1), rwkv_gn_g[j], rwkv_gn_b[j], conv_w[j], conv_b[j],
                           conv_ln_g[j], conv_ln_b[j], tb=tb_odd)
        last = layer == depth - 1
        x = _mlp_layer(x.reshape(B * T, D), norm_ffn[layer], w_up_b, w_down_b, norm_final,
                       layer=layer, tm=tm, final_norm=last).reshape(B, T, D)
    return x
```

```python
import functools
import math

import jax
import jax.numpy as jnp
from jax import lax
from jax.experimental import pallas as pl
from jax.experimental.pallas import tpu as pltpu

F32 = jnp.float32
BF16 = jnp.bfloat16

RMS_EPS = 1e-6
LN_EPS = 1e-5

GLA_HEADS = 4
GLA_DK = 64
GLA_DV = 128
GLA_KEY = GLA_HEADS * GLA_DK
GLA_WIDTH = GLA_HEADS * GLA_DV
GLA_GATE_RANK = 16
GLA_TAU = 16.0
GLA_CHUNK = 64
GLA_LOCAL = 8
SGU_WIDTH = 512
SGU_GROUPS = 4
SGU_CHUNK = 128
EVEN_COLS = 2688
EVEN_SPLIT = {"qk": (0, 512), "v": (512, 1024), "g": (1024, 1536), "u": (1536, 2048),
              "sv": (2048, 2560), "alr": (2560, 2688)}

RWKV_WIDTH = 512
RWKV_HEAD = 64
RWKV_HEADS = 8
RWKV_DECAY_RANK = 32
RWKV_AAA_RANK = 32
RWKV_GATE_RANK = 96
RWKV_GN_EPS = 64e-5
RWKV_IN = 1696
RWKV_LR_COLS = 256
RWKV_COLS = 3 * RWKV_WIDTH + RWKV_LR_COLS
RWKV_CHUNK = 64
RWKV_GROUP = 2
LOCAL_CHUNKS = 2
LOCAL_STAGES = 8
PIPELINE_ORDER = "LSLLSLSLLSLL"
CONV_WIDTH = 512
CONV_KERNEL = 31
CONV_HALO = 32
CONV_TILE_ROWS = 32
ODD_COLS = RWKV_COLS + 2 * CONV_WIDTH

LANE = 128
MXU_TILE = 256
VMEM_LIMIT = 56 * 1024 * 1024
EVEN_BLOCK = 1024
ODD_BLOCK = 512
MLP_BLOCK = 1024
MLP_FF_CHUNK = 1024


def _iota(shape, dim):
    return lax.broadcasted_iota(jnp.int32, shape, dim)


def _dot(a, b):
    return jnp.dot(a, b, preferred_element_type=F32)


def _dot_nt(a, b):
    return lax.dot_general(a, b, (((1,), (1,)), ((), ())), preferred_element_type=F32)


def _dot_tn(a, b):
    return lax.dot_general(a, b, (((0,), (0,)), ((), ())), preferred_element_type=F32)


def _split2(x):
    hi = x.astype(BF16)
    lo = (x - hi.astype(F32)).astype(BF16)
    return hi, lo


def _chunk_cumsum(tril2, x):
    hi, lo = _split2(x)
    cum = _dot(tril2, jnp.concatenate([hi, lo], axis=0))
    n = x.shape[0]
    return cum, jnp.broadcast_to(cum[n - 1:n], cum.shape)


def _head_sums(x, hm_half):
    half = hm_half.shape[0]
    xb = x.astype(BF16)
    return jnp.concatenate([_dot(xb[:, 0:half], hm_half), _dot(xb[:, half:2 * half], hm_half)], axis=1)


def _rms(x, g, eps):
    return x * lax.rsqrt(jnp.mean(x * x, axis=-1, keepdims=True) + eps) * g


def _layernorm(x, g, b, eps):
    mu = jnp.mean(x, axis=-1, keepdims=True)
    xc = x - mu
    var = jnp.mean(xc * xc, axis=-1, keepdims=True)
    return xc * lax.rsqrt(var + eps) * g + b


def _sigmoid(x):
    return 1.0 / (1.0 + jnp.exp(-x))


def _silu(x):
    return x * _sigmoid(x)


def _softplus(x):
    return jnp.maximum(x, 0.0) + jnp.log(1.0 + jnp.exp(-jnp.abs(x)))


def _gelu_tanh(x):
    c = 0.7978845608028654
    return 0.5 * x * (1.0 + jnp.tanh(c * (x + 0.044715 * (x * x * x))))


def _tril_ones(n, dtype):
    return jnp.where(_iota((n, n), 0) >= _iota((n, n), 1), 1.0, 0.0).astype(dtype)


def _even_kernel(x_ref, gn_ref, win_ref, wal_ref, bal_ref, glan_ref, lng_ref, lnb_ref,
                 sguw_ref, sgub_ref, wout_ref, o_ref,
                 la_ref, bl_ref, dl_ref, qd_ref, kd_ref, ks_ref, vb_ref, gt_ref, oa_ref, ob_ref, st_ref,
                 *, tb):
    t_idx = pl.program_id(1)

    @pl.when(t_idx == 0)
    def _():
        st_ref[...] = jnp.zeros_like(st_ref)

    x = x_ref[...]
    h = _rms(x, gn_ref[...], RMS_EPS).astype(BF16)

    def project(name):
        c0, c1 = EVEN_SPLIT[name]
        return _dot(h, win_ref[:, c0:c1])

    p_alr = project("alr")
    p_sv = project("sv")
    a_hi, a_lo = _split2(p_alr)
    w_hi, w_lo = _split2(wal_ref[...])
    z = _dot(a_hi, w_hi) + _dot(a_hi, w_lo) + _dot(a_lo, w_hi) + bal_ref[...]
    la = -_softplus(-z) * (1.0 / GLA_TAU)
    p_u = project("u")
    sv =_layernorm(_gelu_tanh(p_sv), lng_ref[...], lnb_ref[...], LN_EPS).astype(BF16)

    n_chunks = tb // GLA_CHUNK
    tril2 = jnp.concatenate([_tril_ones(GLA_CHUNK, BF16)] * 2, axis=1)
    for c in range(n_chunks):
        cum_c, last_c = _chunk_cumsum(tril2, la[c * GLA_CHUNK:(c + 1) * GLA_CHUNK])
        la_ref[c * GLA_CHUNK:(c + 1) * GLA_CHUNK, :] = cum_c
        bl_ref[c * GLA_CHUNK:(c + 1) * GLA_CHUNK, :] = last_c
    p_qk = project("qk")
    ug = _gelu_tanh(p_u)

    tri128 = _iota((SGU_CHUNK, SGU_CHUNK), 0) >= _iota((SGU_CHUNK, SGU_CHUNK), 1)
    for gg in range(SGU_GROUPS):
        wg = jnp.where(tri128, sguw_ref[gg], 0.0).astype(BF16)
        for c in range(tb // SGU_CHUNK):
            r0 = c * SGU_CHUNK
            s = _dot(wg, sv[r0:r0 + SGU_CHUNK, gg * LANE:(gg + 1) * LANE])
            s = s + sgub_ref[:, gg * LANE:(gg + 1) * LANE]
            ob_ref[r0:r0 + SGU_CHUNK, gg * LANE:(gg + 1) * LANE] = (
                ug[r0:r0 + SGU_CHUNK, gg * LANE:(gg + 1) * LANE] * s)

    p_v = project("v")
    b = la_ref[...]
    b_last = bl_ref[...]
    for c in range(n_chunks):
        dl_ref[c] = jnp.exp(b_last[c * GLA_CHUNK:c * GLA_CHUNK + 8])
    k = p_qk[:, GLA_KEY:2 * GLA_KEY]
    qd_ref[...] = (p_qk[:, 0:GLA_KEY] * (GLA_DK ** -0.5) * jnp.exp(b)).astype(BF16)
    kd_ref[...] = (k * jnp.exp(-b)).astype(BF16)
    ks_ref[...] = (k * jnp.exp(b_last - b)).astype(BF16)
    p_g = project("g")
    vb_ref[...] = p_v.astype(BF16)
    gt_ref[...] = glan_ref[...] * _silu(p_g)

    rk = _iota((GLA_KEY, GLA_KEY), 0) >> 6
    ck = _iota((GLA_KEY, GLA_KEY), 1) >> 6
    k4_mask = rk == ck
    rv = _iota((GLA_KEY, GLA_WIDTH), 0) >> 6
    cv = _iota((GLA_KEY, GLA_WIDTH), 1) >> 7
    vbd_mask = rv == cv
    causal = _iota((GLA_CHUNK, GLA_KEY), 0) >= (_iota((GLA_CHUNK, GLA_KEY), 1) & 63)
    rs = _iota((GLA_WIDTH, GLA_KEY), 0) >> 7
    cs = _iota((GLA_WIDTH, GLA_KEY), 1) >> 6
    st_mask = rs == cs

    zero_k = jnp.zeros((GLA_KEY, GLA_KEY), BF16)
    zero_v = jnp.zeros((GLA_KEY, GLA_WIDTH), BF16)

    def gla_step(ci_, carry):
        base = pl.multiple_of(ci_ * (GLA_LOCAL * GLA_CHUNK), GLA_LOCAL * GLA_CHUNK)
        ccs = range(GLA_LOCAL)
        rows = [pl.ds(base + cc * GLA_CHUNK, GLA_CHUNK) for cc in ccs]
        q_dec = [qd_ref[r, :] for r in rows]
        k4 = [jnp.where(k4_mask, jnp.concatenate([kd_ref[r, :]] * GLA_HEADS, axis=0), zero_k)
              for r in rows]
        v = [vb_ref[r, :] for r in rows]
        v_bd = [jnp.where(vbd_mask, jnp.concatenate([v[cc]] * GLA_HEADS, axis=0), zero_v) for cc in ccs]
        att = [_dot_nt(q_dec[cc], k4[cc]) for cc in ccs]
        kv = [_dot_tn(v[cc], ks_ref[rows[cc], :]) for cc in ccs]
        att = [jnp.where(causal, att[cc], 0.0).astype(BF16) for cc in ccs]
        o_intra = [_dot(att[cc], v_bd[cc]) for cc in ccs]
        st = st_ref[...]
        for cc in ccs:
            o = o_intra[cc] + _dot_nt(q_dec[cc], st.astype(BF16))
            dl = dl_ref[ci_ * GLA_LOCAL + cc][0:1, :]
            st = st * dl + jnp.where(st_mask, kv[cc], 0.0)
            for hh in range(GLA_HEADS):
                oh = o[:, hh * GLA_DV:(hh + 1) * GLA_DV]
                oh = oh * lax.rsqrt(jnp.mean(oh * oh, axis=-1, keepdims=True) + RMS_EPS)
                oa_ref[rows[cc], hh * GLA_DV:(hh + 1) * GLA_DV] = oh
        st_ref[...] = st
        return carry

    lax.fori_loop(0, n_chunks // GLA_LOCAL, gla_step, 0)

    o_a = oa_ref[...] * gt_ref[...]
    m = (_dot(o_a.astype(BF16), wout_ref[0:GLA_WIDTH, :])
         + _dot(ob_ref[...].astype(BF16), wout_ref[GLA_WIDTH:GLA_WIDTH + SGU_WIDTH, :]))
    o_ref[...] = x + m


def _even_layer(x, norm_g, w_in, w_out, w_alpha2, b_alpha, gla_norm, ln_g, ln_b, sgu_w, sgu_b, *, tb):
    B, T, D = x.shape
    w_in = w_in.astype(BF16)
    n_qkvg = 2 * GLA_KEY + 2 * GLA_WIDTH
    n_ref = n_qkvg + GLA_GATE_RANK + 2 * SGU_WIDTH
    w_in_r = jnp.concatenate(
        [w_in[:, :n_qkvg], w_in[:, n_qkvg + GLA_GATE_RANK:n_ref], w_in[:, n_qkvg:n_qkvg + GLA_GATE_RANK],
         jnp.zeros((D, EVEN_COLS - n_ref), BF16)], axis=1)
    wal = jnp.zeros((LANE, GLA_KEY), F32).at[:GLA_GATE_RANK].set(w_alpha2.astype(F32))
    sgub = jnp.repeat(sgu_b.astype(F32).T, LANE, axis=1)
    row = lambda a: a.astype(F32).reshape(1, -1)
    const = lambda shape: pl.BlockSpec(shape, lambda b, t: (0,) * len(shape),
                                       pipeline_mode=pl.Buffered(1))
    kern = functools.partial(_even_kernel, tb=tb)
    return pl.pallas_call(
        kern,
        grid=(B, T // tb),
        in_specs=[
            pl.BlockSpec((None, tb, D), lambda b, t: (b, t, 0)),
            const((1, D)), const((D, EVEN_COLS)), const((LANE, GLA_KEY)), const((1, GLA_KEY)),
            const((1, GLA_WIDTH)), const((1, SGU_WIDTH)), const((1, SGU_WIDTH)),
            const((SGU_GROUPS, SGU_CHUNK, SGU_CHUNK)), const((SGU_CHUNK, SGU_WIDTH)),
            const((D, D)),
        ],
        out_specs=pl.BlockSpec((None, tb, D), lambda b, t: (b, t, 0)),
        out_shape=jax.ShapeDtypeStruct((B, T, D), F32),
        scratch_shapes=[
            pltpu.VMEM((tb, GLA_KEY), F32),
            pltpu.VMEM((tb, GLA_KEY), F32),
            pltpu.VMEM((tb // GLA_CHUNK, 8, GLA_KEY), F32),
            pltpu.VMEM((tb, GLA_KEY), BF16), pltpu.VMEM((tb, GLA_KEY), BF16),
            pltpu.VMEM((tb, GLA_KEY), BF16),
            pltpu.VMEM((tb, GLA_WIDTH), BF16),
            pltpu.VMEM((tb, GLA_WIDTH), F32),
            pltpu.VMEM((tb, GLA_WIDTH), F32),
            pltpu.VMEM((tb, SGU_WIDTH), F32),
            pltpu.VMEM((GLA_WIDTH, GLA_KEY), F32),
        ],
        compiler_params=pltpu.CompilerParams(
            dimension_semantics=("arbitrary", "arbitrary"), vmem_limit_bytes=VMEM_LIMIT),
        name="even_mixer",
    )(x, row(norm_g), w_in_r, wal, row(b_alpha), row(gla_norm), row(ln_g), row(ln_b),
      sgu_w.astype(F32), sgub, w_out.astype(BF16))


def _odd_kernel(x_ref, gn_ref, win_ref, mu_ref, wlr_ref, w0_ref, a0_ref, kk_ref, ka_ref, rk_ref,
                gng_ref, gnb_ref, cw_ref, cb_ref, clg_ref, clb_ref, wout_ref, o_ref,
                prev_ref, zc_ref, cv_s, cum_s, last_s, y_s, rt_s, pt_s, qt_s, kt_s, qh_s, kh_s, v_s,
                z_s, gl_s, ark_s, uw_s, *, tb):
    t_idx = pl.program_id(1)
    gw = RWKV_GROUP * RWKV_HEAD
    n_groups = RWKV_WIDTH // gw
    n_chunks = tb // RWKV_CHUNK

    @pl.when(t_idx == 0)
    def _():
        prev_ref[...] = jnp.zeros_like(prev_ref)
        zc_ref[0:CONV_HALO, :] = jnp.zeros((CONV_HALO, CONV_WIDTH), F32)
        zc_ref[tb + CONV_HALO:tb + CONV_HALO + 8, :] = jnp.zeros((8, CONV_WIDTH), F32)
        z_s[...] = jnp.zeros_like(z_s)

    @pl.when(t_idx != 0)
    def _():
        zc_ref[0:CONV_HALO, :] = zc_ref[tb:tb + CONV_HALO, :]

    x = x_ref[...]
    h = _rms(x, gn_ref[...], RMS_EPS).astype(BF16)
    c_r, c_k, c_v = (slice(i * RWKV_WIDTH, (i + 1) * RWKV_WIDTH) for i in range(3))
    c_lr = slice(3 * RWKV_WIDTH, RWKV_COLS)
    c_conv = slice(RWKV_COLS, ODD_COLS)

    def project(cols):
        return _dot(h, win_ref[:, cols])

    def shift_mix(pr, cols):
        sh = pltpu.roll(pr, 1, 0)
        head = jnp.where(_iota((8, pr.shape[1]), 0) == 0, prev_ref[7:8, cols], sh[0:8])
        prev_ref[:, cols] = pr[tb - 8:tb]
        sh = jnp.concatenate([head, sh[8:]], axis=0)
        return pr + (sh - pr) * mu_ref[:, cols]

    never = jnp.full((CONV_TILE_ROWS, LANE), t_idx, jnp.int32) < 0

    def conv_tile(r0, n, lanes, after):
        first = CONV_HALO - (CONV_KERNEL - 1)
        wn = n + CONV_HALO + 8
        win = zc_ref[pl.ds(r0, wn), lanes]
        acc = jnp.where(never, jnp.concatenate([after] * (n // 8), axis=0),
                        jnp.zeros((n, LANE), F32) + cb_ref[:, lanes])
        for ph in range(8):
            sh = win if ph == 0 else pltpu.roll(win, wn - ph, 0)
            for j in range(CONV_KERNEL):
                if (j + first) % 8 == ph:
                    a8 = (j + first) // 8 * 8
                    acc = acc + sh[a8:a8 + n] * cw_ref[j:j + 1, lanes]
        return acc

    hm = jnp.where((_iota((MXU_TILE, MXU_TILE), 0) >> 6) == (_iota((MXU_TILE, MXU_TILE), 1) >> 6),
                   1.0, 0.0).astype(BF16)
    p_lr = project(c_lr)
    p_k = project(c_k)
    lr = shift_mix(p_lr, c_lr)
    lane = _iota((tb, RWKV_LR_COLS), 1)
    f = jnp.where(lane < RWKV_DECAY_RANK, jnp.tanh(lr),
                  jnp.where(lane < RWKV_DECAY_RANK + RWKV_AAA_RANK, lr, _sigmoid(lr)))
    lo3 = _dot(f.astype(BF16), wlr_ref[...])
    p_r = project(c_r)
    k = shift_mix(p_k, c_k)
    kk = k * kk_ref[...]
    kk = kk * lax.rsqrt(jnp.maximum(_head_sums(kk * kk, hm), 1e-24))
    lo_w, lo_a, lo_g = c_r, c_k, c_v
    lw = -math.exp(-0.5) * _sigmoid(w0_ref[...] + lo3[:, lo_w])
    a = _sigmoid(a0_ref[...] + lo3[:, lo_a])
    gate = lo3[:, lo_g]

    tril2 = jnp.concatenate([_tril_ones(RWKV_CHUNK, BF16)] * 2, axis=1)
    for c in range(n_chunks):
        cum_c, last_c = _chunk_cumsum(tril2, lw[c * RWKV_CHUNK:(c + 1) * RWKV_CHUNK])
        cum_s[c * RWKV_CHUNK:(c + 1) * RWKV_CHUNK, :] = cum_c
        last_s[c * RWKV_CHUNK:(c + 1) * RWKV_CHUNK, :] = last_c
    p_v = project(c_v)
    cum = cum_s[...]
    e_last = jnp.exp(last_s[...])
    for c in range(n_chunks):
        gl_s[c] = e_last[c * RWKV_CHUNK:c * RWKV_CHUNK + 8]
    e_neg = jnp.exp(-cum)
    e_hat = e_last * e_neg
    k2 = k * (1.0 + (a - 1.0) * ka_ref[...])
    qa = kk * a
    pt_s[...] = (-kk * jnp.exp(cum - lw)).astype(BF16)
    qt_s[...] = (qa * e_neg).astype(BF16)
    kt_s[...] = (k2 * e_neg).astype(BF16)
    qh_s[...] = (qa * e_hat).astype(BF16)
    kh_s[...] = (k2 * e_hat).astype(BF16)
    p_conv = project(c_conv)
    r = shift_mix(p_r, c_r)
    rt_s[...] = (r * jnp.exp(cum)).astype(BF16)
    v = shift_mix(p_v, c_v)
    v_s[...] = v.astype(BF16)
    bonus = _head_sums(r * k2 * rk_ref[...], hm) * v

    zc_ref[CONV_HALO:CONV_HALO + tb, :] = p_conv[:, 0:CONV_WIDTH] * _sigmoid(
        p_conv[:, CONV_WIDTH:2 * CONV_WIDTH])

    ri = _iota((gw, gw), 0)
    ci = _iota((gw, gw), 1)
    same = (ri >> 6) == (ci >> 6)
    strict = same & (ri > ci)
    ri2 = _iota((gw, 2 * gw), 0)
    ci2 = _iota((gw, 2 * gw), 1) & (gw - 1)
    incl2 = ((ri2 >> 6) == (ci2 >> 6)) & (ri2 >= ci2)
    eye = jnp.where(ri == ci, 1.0, 0.0)

    zero_bf = jnp.zeros((gw, gw), BF16)

    def bd(xs):
        return jnp.where(same, jnp.concatenate([xs] * RWKV_GROUP, axis=0), zero_bf)

    def aligned(i, m):
        return i if isinstance(i, int) else pl.multiple_of(i, m)

    def local_stages(ci_):
        probs = [(cc, j) for cc in range(LOCAL_CHUNKS) for j in range(n_groups)]
        idx = range(len(probs))
        base = aligned(ci_ * (LOCAL_CHUNKS * RWKV_CHUNK), LOCAL_CHUNKS * RWKV_CHUNK)

        def blk(ref, cc, j):
            return bd(ref[pl.ds(base + cc * RWKV_CHUNK, RWKV_CHUNK), j * gw:(j + 1) * gw])

        def slot(cc, j):
            return (ci_ * LOCAL_CHUNKS + cc) * n_groups + j

        P2 = [blk(pt_s, cc, j) for cc, j in probs]
        PR = [jnp.concatenate([P2[i], blk(rt_s, cc, j)], axis=0) for i, (cc, j) in enumerate(probs)]
        QK = [jnp.concatenate([blk(qt_s, cc, j), blk(kt_s, cc, j)], axis=0) for cc, j in probs]
        V2 = [blk(v_s, cc, j) for cc, j in probs]
        sc = [_dot_nt(PR[i], QK[i]) for i in idx]
        yield sc[-1][0:8, 0:LANE]
        A_pq = [jnp.where(strict, sc[i][0:gw, 0:gw], 0.0) for i in idx]
        A_pk = [jnp.where(strict, sc[i][0:gw, gw:2 * gw], 0.0).astype(BF16) for i in idx]
        for i, (cc, j) in enumerate(probs):
            ark_s[slot(cc, j)] = jnp.where(incl2, sc[i][gw:2 * gw, :], 0.0).astype(BF16)
        Xb = [A_pq[i].astype(BF16) for i in idx]
        Xp = [_dot(Xb[i], Xb[i]) for i in idx]
        av = [_dot(A_pk[i], V2[i]) for i in idx]
        yield av[-1][0:8, 0:LANE]
        av = [av[i].astype(BF16) for i in idx]
        Tm = [eye + A_pq[i] for i in idx]
        for step in range(4):
            Xb = [Xp[i].astype(BF16) for i in idx]
            XT = [_dot(Xb[i], jnp.concatenate([Xb[i], Tm[i].astype(BF16)], axis=1)) for i in idx]
            yield XT[-1][0:8, 0:LANE]
            Xp = [XT[i][:, 0:gw] for i in idx]
            Tm = [Tm[i] + XT[i][:, gw:2 * gw] for i in idx]
        XT = [_dot(Xp[i].astype(BF16), Tm[i].astype(BF16)) for i in idx]
        yield XT[-1][0:8, 0:LANE]
        Tm = [(Tm[i] + XT[i]).astype(BF16) for i in idx]
        uw = [_dot(Tm[i], jnp.concatenate([av[i], P2[i]], axis=1)) for i in idx]
        for i, (cc, j) in enumerate(probs):
            uw_s[slot(cc, j)] = uw[i].astype(BF16)
        yield uw[-1][0:8, 0:LANE]

    def state_stages(ci_):
        groups = range(n_groups)
        sls = [slice(j * gw, (j + 1) * gw) for j in groups]
        for cc in range(LOCAL_CHUNKS):
            c = ci_ * LOCAL_CHUNKS + cc
            rows = pl.ds(aligned(c * RWKV_CHUNK, RWKV_CHUNK), RWKV_CHUNK)
            e_last = gl_s[c][0:1, :]
            R2 = [bd(rt_s[rows, s]) for s in sls]
            uw = [uw_s[c * n_groups + j] for j in groups]
            Z = [z_s[j] for j in groups]
            hw = [_dot_nt(jnp.concatenate([uw[j][:, gw:2 * gw], R2[j]], axis=0), Z[j].astype(BF16))
                  for j in groups]
            yield
            QK = [jnp.concatenate([bd(qh_s[rows, s]), bd(kh_s[rows, s])], axis=0) for s in sls]
            V2 = [bd(v_s[rows, s]) for s in sls]
            UV = [jnp.concatenate([(uw[j][:, 0:gw].astype(F32) + hw[j][0:gw]).astype(BF16), V2[j]],
                                  axis=0) for j in groups]
            zn = [_dot_tn(UV[j], QK[j]) for j in groups]
            Y2 = [hw[j][gw:2 * gw] + _dot(ark_s[c * n_groups + j], UV[j]) for j in groups]
            for j in groups:
                z_s[j] = Z[j] * e_last[:, sls[j]] + zn[j]
                y = Y2[j][0:RWKV_CHUNK]
                for e in range(1, RWKV_GROUP):
                    y = y + Y2[j][e * RWKV_CHUNK:(e + 1) * RWKV_CHUNK]
                y_s[rows, sls[j]] = y
            yield

    lane_blocks = CONV_WIDTH // LANE

    conv_tiles = [(r, lb) for r in range(0, LOCAL_CHUNKS * RWKV_CHUNK, CONV_TILE_ROWS)
                  for lb in range(lane_blocks)]
    tiles_per_stage = len(conv_tiles) // LOCAL_STAGES

    def conv_after(ci_, k, after):
        for r, lb in conv_tiles[k * tiles_per_stage:(k + 1) * tiles_per_stage]:
            lanes = slice(lb * LANE, (lb + 1) * LANE)
            r0 = aligned(ci_ * (LOCAL_CHUNKS * RWKV_CHUNK) + r, CONV_TILE_ROWS)
            cv_s[pl.ds(r0, CONV_TILE_ROWS), lanes] = conv_tile(r0, CONV_TILE_ROWS, lanes, after)

    n_groups_t = n_chunks // LOCAL_CHUNKS
    for k, after in enumerate(local_stages(0)):
        conv_after(0, k, after)

    def pipelined(i, carry):
        g_local, g_state = local_stages(i), state_stages(i - 1)
        k = 0
        for which in PIPELINE_ORDER:
            if which == "L":
                conv_after(i, k, next(g_local))
                k += 1
            else:
                next(g_state)
        return carry

    lax.fori_loop(1, n_groups_t, pipelined, 0)
    for _ in state_stages(n_groups_t - 1):
        pass

    o_d = _silu(_layernorm(cv_s[...], clg_ref[...], clb_ref[...], LN_EPS))

    y = y_s[...]
    inv_n = 1.0 / RWKV_HEAD
    mu_y = _head_sums(y, hm) * inv_n
    yc = y - mu_y
    var_y = _head_sums(yc * yc, hm) * inv_n
    y = yc * lax.rsqrt(var_y + RWKV_GN_EPS) * gng_ref[...] + gnb_ref[...] + bonus
    o_c = y * gate

    m = (_dot(o_c.astype(BF16), wout_ref[0:RWKV_WIDTH, :])
         + _dot(o_d.astype(BF16), wout_ref[RWKV_WIDTH:RWKV_WIDTH + CONV_WIDTH, :]))
    o_ref[...] = x + m


def _odd_layer(x, norm_g, w_in, w_out, mu, w0, w2, a0, a2, g2, k_k, k_a, r_k, gn_g, gn_b,
               conv_w, conv_b, cln_g, cln_b, *, tb):
    B, T, D = x.shape
    pad_lr = RWKV_LR_COLS - (RWKV_DECAY_RANK + RWKV_AAA_RANK + RWKV_GATE_RANK)
    w_in = w_in.astype(BF16)
    w_in_r = jnp.concatenate(
        [w_in[:, :RWKV_IN], jnp.zeros((D, pad_lr), BF16), w_in[:, RWKV_IN:]], axis=1)
    mu_r = jnp.concatenate([mu.astype(F32), jnp.zeros((pad_lr,), F32)]).reshape(1, RWKV_COLS)
    wd = RWKV_WIDTH
    r1, r2 = RWKV_DECAY_RANK, RWKV_DECAY_RANK + RWKV_AAA_RANK
    wlr = jnp.zeros((RWKV_LR_COLS, 3 * wd), F32)
    wlr = wlr.at[0:r1, 0:wd].set(w2.astype(F32))
    wlr = wlr.at[r1:r2, wd:2 * wd].set(a2.astype(F32))
    wlr = wlr.at[r2:r2 + RWKV_GATE_RANK, 2 * wd:3 * wd].set(g2.astype(F32))
    row = lambda a: a.astype(F32).reshape(1, -1)
    const = lambda shape: pl.BlockSpec(shape, lambda b, t: (0,) * len(shape),
                                       pipeline_mode=pl.Buffered(1))
    gw = RWKV_GROUP * RWKV_HEAD
    kern = functools.partial(_odd_kernel, tb=tb)
    return pl.pallas_call(
        kern,
        grid=(B, T // tb),
        in_specs=[
            pl.BlockSpec((None, tb, D), lambda b, t: (b, t, 0)),
            const((1, D)), const((D, ODD_COLS)), const((1, RWKV_COLS)), const((RWKV_LR_COLS, 3 * wd)),
            const((1, wd)), const((1, wd)), const((1, wd)), const((1, wd)), const((1, wd)),
            const((1, wd)), const((1, wd)),
            const((CONV_KERNEL, CONV_WIDTH)), const((1, CONV_WIDTH)), const((1, CONV_WIDTH)),
            const((1, CONV_WIDTH)),
            const((D, D)),
        ],
        out_specs=pl.BlockSpec((None, tb, D), lambda b, t: (b, t, 0)),
        out_shape=jax.ShapeDtypeStruct((B, T, D), F32),
        scratch_shapes=[
            pltpu.VMEM((8, RWKV_COLS), F32),
            pltpu.VMEM((tb + CONV_HALO + 8, CONV_WIDTH), F32),
            pltpu.VMEM((tb, CONV_WIDTH), F32),
            pltpu.VMEM((tb, wd), F32), pltpu.VMEM((tb, wd), F32),
            pltpu.VMEM((tb, wd), F32),
            pltpu.VMEM((tb, wd), BF16), pltpu.VMEM((tb, wd), BF16), pltpu.VMEM((tb, wd), BF16),
            pltpu.VMEM((tb, wd), BF16), pltpu.VMEM((tb, wd), BF16), pltpu.VMEM((tb, wd), BF16),
            pltpu.VMEM((tb, wd), BF16),
            pltpu.VMEM((RWKV_WIDTH // gw, gw, gw), F32),
            pltpu.VMEM((tb // RWKV_CHUNK, 8, RWKV_WIDTH), F32),
            pltpu.VMEM((tb // RWKV_CHUNK * (RWKV_WIDTH // gw), gw, 2 * gw), BF16),
            pltpu.VMEM((tb // RWKV_CHUNK * (RWKV_WIDTH // gw), gw, 2 * gw), BF16),
        ],
        compiler_params=pltpu.CompilerParams(
            dimension_semantics=("arbitrary", "arbitrary"), vmem_limit_bytes=VMEM_LIMIT),
        name="odd_mixer",
    )(x, row(norm_g), w_in_r, mu_r, wlr.astype(BF16), row(w0), row(a0), row(k_k), row(k_a), row(r_k),
      row(gn_g), row(gn_b), conv_w.astype(F32), row(conv_b), row(cln_g), row(cln_b), w_out.astype(BF16))


def _mlp_kernel(x_ref, g_ref, wup_ref, wdn_ref, gf_ref, o_ref, *, ff_chunk, final_norm):
    x = x_ref[...]
    xb = _rms(x, g_ref[...], RMS_EPS).astype(BF16)
    acc = x
    d_ff = wup_ref.shape[1]
    for f0 in range(0, d_ff, ff_chunk):
        hcol = _dot(xb, wup_ref[:, f0:f0 + ff_chunk])
        hcol = jnp.square(jnp.maximum(hcol, 0.0))
        acc = acc + _dot(hcol.astype(BF16), wdn_ref[f0:f0 + ff_chunk, :])
    if final_norm:
        acc = _rms(acc, gf_ref[...], RMS_EPS)
    o_ref[...] = acc


def _mlp_layer(x2, norm_g, w_up, w_down, final_g, *, layer, tm, final_norm):
    M, D = x2.shape
    d_ff = w_up.shape[2]
    kern = functools.partial(_mlp_kernel, ff_chunk=MLP_FF_CHUNK, final_norm=final_norm)
    return pl.pallas_call(
        kern,
        grid=(M // tm,),
        in_specs=[
            pl.BlockSpec((tm, D), lambda i: (i, 0)),
            pl.BlockSpec((1, D), lambda i: (0, 0)),
            pl.BlockSpec((None, D, d_ff), lambda i: (layer, 0, 0), pipeline_mode=pl.Buffered(1)),
            pl.BlockSpec((None, d_ff, D), lambda i: (layer, 0, 0), pipeline_mode=pl.Buffered(1)),
            pl.BlockSpec((1, D), lambda i: (0, 0)),
        ],
        out_specs=pl.BlockSpec((tm, D), lambda i: (i, 0)),
        out_shape=jax.ShapeDtypeStruct((M, D), F32),
        compiler_params=pltpu.CompilerParams(
            dimension_semantics=("arbitrary",), vmem_limit_bytes=VMEM_LIMIT),
        name="mlp_final" if final_norm else "mlp",
    )(x2, norm_g.astype(F32).reshape(1, D), w_up, w_down, final_g.astype(F32).reshape(1, D))


def _pick_block(n, target):
    b = min(n, target)
    while n % b:
        b //= 2
    return b


def kernel(x, norm_mix, norm_ffn, w_up, w_down, norm_final, even_w_in, even_w_out, gla_w_alpha2, gla_b_alpha, gla_norm, sgu_ln_g, sgu_ln_b, sgu_w, sgu_b, odd_w_in, odd_w_out, rwkv_mu, rwkv_w0, rwkv_w2, rwkv_a0, rwkv_a2, rwkv_g2, rwkv_k_k, rwkv_k_a, rwkv_r_k, rwkv_gn_g, rwkv_gn_b, conv_w, conv_b, conv_ln_g, conv_ln_b):
    B, T, D = x.shape
    depth = norm_mix.shape[0]
    tb_even = _pick_block(T, EVEN_BLOCK)
    tb_odd = _pick_block(T, ODD_BLOCK)
    tm = _pick_block(B * T, MLP_BLOCK)
    assert tb_even % (GLA_LOCAL * GLA_CHUNK) == 0 and tb_even % SGU_CHUNK == 0
    assert tb_odd % (LOCAL_CHUNKS * RWKV_CHUNK) == 0
    assert D == GLA_WIDTH + SGU_WIDTH == RWKV_WIDTH + CONV_WIDTH
    w_up_b, w_down_b = w_up.astype(BF16), w_down.astype(BF16)
    for layer in range(depth):
        j = layer // 2
        if layer % 2 == 0:
            x = _even_layer(x, norm_mix[layer], even_w_in[j], even_w_out[j], gla_w_alpha2[j],
                            gla_b_alpha[j], gla_norm[j], sgu_ln_g[j], sgu_ln_b[j], sgu_w[j], sgu_b[j],
                            tb=tb_even)
        else:
            x = _odd_layer(x, norm_mix[layer], odd_w_in[j], odd_w_out[j], rwkv_mu[j], rwkv_w0[j],
                           rwkv_w2[j], rwkv_a0[j], rwkv_a2[j], rwkv_g2[j], rwkv_k_k[j], rwkv_k_a[j],
                           rwkv_r_k[j].reshape(-1), rwkv_gn_g[j], rwkv_gn_b[j], conv_w[j], conv_b[j],
                           conv_ln_g[j], conv_ln_b[j], tb=tb_odd)
        last = layer == depth - 1
        x = _mlp_layer(x.reshape(B * T, D), norm_ffn[layer], w_up_b, w_down_b, norm_final,
                       layer=layer, tm=tm, final_norm=last).reshape(B, T, D)
    return x
```

```python
import functools
import math

import jax
import jax.numpy as jnp
from jax import lax
from jax.experimental import pallas as pl
from jax.experimental.pallas import tpu as pltpu

F32 = jnp.float32
BF16 = jnp.bfloat16

RMS_EPS = 1e-6
LN_EPS = 1e-5

GLA_HEADS = 4
GLA_DK = 64
GLA_DV = 128
GLA_KEY = GLA_HEADS * GLA_DK
GLA_WIDTH = GLA_HEADS * GLA_DV
GLA_GATE_RANK = 16
GLA_TAU = 16.0
GLA_CHUNK = 64
GLA_LOCAL = 16
SGU_WIDTH = 512
SGU_GROUPS = 4
SGU_CHUNK = 128
EVEN_COLS = 2688
EVEN_SPLIT = {"qk": (0, 512), "v": (512, 1024), "g": (1024, 1536), "u": (1536, 2048),
              "sv": (2048, 2560), "alr": (2560, 2688)}

RWKV_WIDTH = 512
RWKV_HEAD = 64
RWKV_HEADS = 8
RWKV_DECAY_RANK = 32
RWKV_AAA_RANK = 32
RWKV_GATE_RANK = 96
RWKV_GN_EPS = 64e-5
RWKV_IN = 1696
RWKV_LR_COLS = 256
RWKV_COLS = 3 * RWKV_WIDTH + RWKV_LR_COLS
RWKV_CHUNK = 64
RWKV_GROUP = 2
LOCAL_CHUNKS = 2
LOCAL_STAGES = 8
PIPELINE_ORDER = "LSLLSLSLLSLL"
CONV_WIDTH = 512
CONV_KERNEL = 31
CONV_HALO = 32
CONV_TILE_ROWS = 32
ODD_COLS = RWKV_COLS + 2 * CONV_WIDTH

LANE = 128
SUBLANE = 8
MXU_TILE = 256
VMEM_LIMIT = 56 * 1024 * 1024
EVEN_BLOCK = 1024
ODD_BLOCK = 512
MLP_BLOCK = 1024
MLP_FF_CHUNK = 1024


def _iota(shape, dim):
    return lax.broadcasted_iota(jnp.int32, shape, dim)


def _dot(a, b):
    return jnp.dot(a, b, preferred_element_type=F32)


def _dot_nt(a, b):
    return lax.dot_general(a, b, (((1,), (1,)), ((), ())), preferred_element_type=F32)


def _dot_tn(a, b):
    return lax.dot_general(a, b, (((0,), (0,)), ((), ())), preferred_element_type=F32)


def _split2(x):
    hi = x.astype(BF16)
    lo = (x - hi.astype(F32)).astype(BF16)
    return hi, lo


def _chunk_cumsum(tril2, x):
    hi, lo = _split2(x)
    cum = _dot(tril2, jnp.concatenate([hi, lo], axis=0))
    n = x.shape[0]
    return cum, jnp.broadcast_to(cum[n - 1:n], cum.shape)


def _head_sums(x, hm_half):
    half = hm_half.shape[0]
    xb = x.astype(BF16)
    return jnp.concatenate([_dot(xb[:, 0:half], hm_half), _dot(xb[:, half:2 * half], hm_half)], axis=1)


def _rms(x, g, eps):
    return x * lax.rsqrt(jnp.mean(x * x, axis=-1, keepdims=True) + eps) * g


def _layernorm(x, g, b, eps):
    mu = jnp.mean(x, axis=-1, keepdims=True)
    xc = x - mu
    var = jnp.mean(xc * xc, axis=-1, keepdims=True)
    return xc * lax.rsqrt(var + eps) * g + b


def _sigmoid(x):
    return 1.0 / (1.0 + jnp.exp(-x))


def _silu(x):
    return x * _sigmoid(x)


def _softplus(x):
    return jnp.maximum(x, 0.0) + jnp.log(1.0 + jnp.exp(-jnp.abs(x)))


def _gelu_tanh(x):
    c = 0.7978845608028654
    return 0.5 * x * (1.0 + jnp.tanh(c * (x + 0.044715 * (x * x * x))))


def _tril_ones(n, dtype):
    return jnp.where(_iota((n, n), 0) >= _iota((n, n), 1), 1.0, 0.0).astype(dtype)


def _even_kernel(x_ref, gn_ref, win_ref, wal_ref, bal_ref, glan_ref, lng_ref, lnb_ref,
                 sguw_ref, sgub_ref, wout_ref, o_ref,
                 la_ref, bl_ref, dl_ref, qd_ref, kd_ref, ks_ref, vb_ref, gt_ref, oa_ref, ob_ref, st_ref,
                 *, tb):
    t_idx = pl.program_id(1)

    @pl.when(t_idx == 0)
    def _():
        st_ref[...] = jnp.zeros_like(st_ref)

    x = x_ref[...]
    h = _rms(x, gn_ref[...], RMS_EPS).astype(BF16)

    def project(name):
        c0, c1 = EVEN_SPLIT[name]
        return _dot(h, win_ref[:, c0:c1])

    p_alr = project("alr")
    p_sv = project("sv")
    a_hi, a_lo = _split2(p_alr)
    w_hi, w_lo = _split2(wal_ref[...])
    z = _dot(a_hi, w_hi) + _dot(a_hi, w_lo) + _dot(a_lo, w_hi) + bal_ref[...]
    la = -_softplus(-z) * (1.0 / GLA_TAU)
    p_u = project("u")
    sv =_layernorm(_gelu_tanh(p_sv), lng_ref[...], lnb_ref[...], LN_EPS).astype(BF16)

    n_chunks = tb // GLA_CHUNK
    tril2 = jnp.concatenate([_tril_ones(GLA_CHUNK, BF16)] * 2, axis=1)
    for c in range(n_chunks):
        cum_c, last_c = _chunk_cumsum(tril2, la[c * GLA_CHUNK:(c + 1) * GLA_CHUNK])
        la_ref[c * GLA_CHUNK:(c + 1) * GLA_CHUNK, :] = cum_c
        bl_ref[c * GLA_CHUNK:(c + 1) * GLA_CHUNK, :] = last_c
    p_qk = project("qk")
    ug = _gelu_tanh(p_u)

    tri128 = _iota((SGU_CHUNK, SGU_CHUNK), 0) >= _iota((SGU_CHUNK, SGU_CHUNK), 1)
    for gg in range(SGU_GROUPS):
        wg = jnp.where(tri128, sguw_ref[gg], 0.0).astype(BF16)
        for c in range(tb // SGU_CHUNK):
            r0 = c * SGU_CHUNK
            s = _dot(wg, sv[r0:r0 + SGU_CHUNK, gg * LANE:(gg + 1) * LANE])
            s = s + sgub_ref[:, gg * LANE:(gg + 1) * LANE]
            ob_ref[r0:r0 + SGU_CHUNK, gg * LANE:(gg + 1) * LANE] = (
                ug[r0:r0 + SGU_CHUNK, gg * LANE:(gg + 1) * LANE] * s)

    p_v = project("v")
    b = la_ref[...]
    b_last = bl_ref[...]
    for c in range(n_chunks):
        dl_ref[c] = jnp.exp(b_last[c * GLA_CHUNK:c * GLA_CHUNK + SUBLANE])
    k = p_qk[:, GLA_KEY:2 * GLA_KEY]
    qd_ref[...] = (p_qk[:, 0:GLA_KEY] * (GLA_DK ** -0.5) * jnp.exp(b)).astype(BF16)
    kd_ref[...] = (k * jnp.exp(-b)).astype(BF16)
    ks_ref[...] = (k * jnp.exp(b_last - b)).astype(BF16)
    p_g = project("g")
    vb_ref[...] = p_v.astype(BF16)
    gt_ref[...] = glan_ref[...] * _silu(p_g)

    rk = _iota((GLA_KEY, GLA_KEY), 0) >> 6
    ck = _iota((GLA_KEY, GLA_KEY), 1) >> 6
    k4_mask = rk == ck
    rv = _iota((GLA_KEY, GLA_WIDTH), 0) >> 6
    cv = _iota((GLA_KEY, GLA_WIDTH), 1) >> 7
    vbd_mask = rv == cv
    causal = _iota((GLA_CHUNK, GLA_KEY), 0) >= (_iota((GLA_CHUNK, GLA_KEY), 1) & 63)
    rs = _iota((GLA_WIDTH, GLA_KEY), 0) >> 7
    cs = _iota((GLA_WIDTH, GLA_KEY), 1) >> 6
    st_mask = rs == cs

    gla_local = math.gcd(GLA_LOCAL, n_chunks)
    zero_k = jnp.zeros((GLA_KEY, GLA_KEY), BF16)
    zero_v = jnp.zeros((GLA_KEY, GLA_WIDTH), BF16)

    def gla_step(ci_, carry):
        base = pl.multiple_of(ci_ * (gla_local * GLA_CHUNK), gla_local * GLA_CHUNK)
        ccs = range(gla_local)
        rows = [pl.ds(base + cc * GLA_CHUNK, GLA_CHUNK) for cc in ccs]
        q_dec = [qd_ref[r, :] for r in rows]
        k4 = [jnp.where(k4_mask, jnp.concatenate([kd_ref[r, :]] * GLA_HEADS, axis=0), zero_k)
              for r in rows]
        v = [vb_ref[r, :] for r in rows]
        v_bd = [jnp.where(vbd_mask, jnp.concatenate([v[cc]] * GLA_HEADS, axis=0), zero_v) for cc in ccs]
        att = [_dot_nt(q_dec[cc], k4[cc]) for cc in ccs]
        kv = [_dot_tn(v[cc], ks_ref[rows[cc], :]) for cc in ccs]
        att = [jnp.where(causal, att[cc], 0.0).astype(BF16) for cc in ccs]
        o_intra = [_dot(att[cc], v_bd[cc]) for cc in ccs]
        st = st_ref[...]
        for cc in ccs:
            o = o_intra[cc] + _dot_nt(q_dec[cc], st.astype(BF16))
            dl = dl_ref[ci_ * gla_local + cc][0:1, :]
            st = st * dl + jnp.where(st_mask, kv[cc], 0.0)
            for hh in range(GLA_HEADS):
                oh = o[:, hh * GLA_DV:(hh + 1) * GLA_DV]
                oh = oh * lax.rsqrt(jnp.mean(oh * oh, axis=-1, keepdims=True) + RMS_EPS)
                oa_ref[rows[cc], hh * GLA_DV:(hh + 1) * GLA_DV] = oh
        st_ref[...] = st
        return carry

    lax.fori_loop(0, n_chunks // gla_local, gla_step, 0)

    o_a = oa_ref[...] * gt_ref[...]
    m = (_dot(o_a.astype(BF16), wout_ref[0:GLA_WIDTH, :])
         + _dot(ob_ref[...].astype(BF16), wout_ref[GLA_WIDTH:GLA_WIDTH + SGU_WIDTH, :]))
    o_ref[...] = x + m


def _even_layer(x, norm_g, w_in, w_out, w_alpha2, b_alpha, gla_norm, ln_g, ln_b, sgu_w, sgu_b, *, tb):
    B, T, D = x.shape
    w_in = w_in.astype(BF16)
    n_qkvg = 2 * GLA_KEY + 2 * GLA_WIDTH
    n_ref = n_qkvg + GLA_GATE_RANK + 2 * SGU_WIDTH
    w_in_r = jnp.concatenate(
        [w_in[:, :n_qkvg], w_in[:, n_qkvg + GLA_GATE_RANK:n_ref], w_in[:, n_qkvg:n_qkvg + GLA_GATE_RANK],
         jnp.zeros((D, EVEN_COLS - n_ref), BF16)], axis=1)
    wal = jnp.zeros((LANE, GLA_KEY), F32).at[:GLA_GATE_RANK].set(w_alpha2.astype(F32))
    sgub = jnp.repeat(sgu_b.astype(F32).T, LANE, axis=1)
    row = lambda a: a.astype(F32).reshape(1, -1)
    const = lambda shape: pl.BlockSpec(shape, lambda b, t: (0,) * len(shape),
                                       pipeline_mode=pl.Buffered(1))
    kern = functools.partial(_even_kernel, tb=tb)
    return pl.pallas_call(
        kern,
        grid=(B, T // tb),
        in_specs=[
            pl.BlockSpec((None, tb, D), lambda b, t: (b, t, 0)),
            const((1, D)), const((D, EVEN_COLS)), const((LANE, GLA_KEY)), const((1, GLA_KEY)),
            const((1, GLA_WIDTH)), const((1, SGU_WIDTH)), const((1, SGU_WIDTH)),
            const((SGU_GROUPS, SGU_CHUNK, SGU_CHUNK)), const((SGU_CHUNK, SGU_WIDTH)),
            const((D, D)),
        ],
        out_specs=pl.BlockSpec((None, tb, D), lambda b, t: (b, t, 0)),
        out_shape=jax.ShapeDtypeStruct((B, T, D), F32),
        scratch_shapes=[
            pltpu.VMEM((tb, GLA_KEY), F32),
            pltpu.VMEM((tb, GLA_KEY), F32),
            pltpu.VMEM((tb // GLA_CHUNK, SUBLANE, GLA_KEY), F32),
            pltpu.VMEM((tb, GLA_KEY), BF16), pltpu.VMEM((tb, GLA_KEY), BF16),
            pltpu.VMEM((tb, GLA_KEY), BF16),
            pltpu.VMEM((tb, GLA_WIDTH), BF16),
            pltpu.VMEM((tb, GLA_WIDTH), F32),
            pltpu.VMEM((tb, GLA_WIDTH), F32),
            pltpu.VMEM((tb, SGU_WIDTH), F32),
            pltpu.VMEM((GLA_WIDTH, GLA_KEY), F32),
        ],
        compiler_params=pltpu.CompilerParams(
            dimension_semantics=("arbitrary", "arbitrary"), vmem_limit_bytes=VMEM_LIMIT),
        name="even_mixer",
    )(x, row(norm_g), w_in_r, wal, row(b_alpha), row(gla_norm), row(ln_g), row(ln_b),
      sgu_w.astype(F32), sgub, w_out.astype(BF16))


def _odd_kernel(x_ref, gn_ref, win_ref, mu_ref, wlr_ref, w0_ref, a0_ref, kk_ref, ka_ref, rk_ref,
                gng_ref, gnb_ref, cw_ref, cb_ref, clg_ref, clb_ref, wout_ref, o_ref,
                prev_ref, zc_ref, cv_s, cum_s, last_s, y_s, rt_s, pt_s, qt_s, kt_s, qh_s, kh_s, v_s,
                z_s, gl_s, ark_s, uw_s, *, tb):
    t_idx = pl.program_id(1)
    gw = RWKV_GROUP * RWKV_HEAD
    n_groups = RWKV_WIDTH // gw
    n_chunks = tb // RWKV_CHUNK

    @pl.when(t_idx == 0)
    def _():
        prev_ref[...] = jnp.zeros_like(prev_ref)
        zc_ref[0:CONV_HALO, :] = jnp.zeros((CONV_HALO, CONV_WIDTH), F32)
        zc_ref[tb + CONV_HALO:tb + CONV_HALO + SUBLANE, :] = jnp.zeros((SUBLANE, CONV_WIDTH), F32)
        z_s[...] = jnp.zeros_like(z_s)

    @pl.when(t_idx != 0)
    def _():
        zc_ref[0:CONV_HALO, :] = zc_ref[tb:tb + CONV_HALO, :]

    x = x_ref[...]
    h = _rms(x, gn_ref[...], RMS_EPS).astype(BF16)
    c_r, c_k, c_v = (slice(i * RWKV_WIDTH, (i + 1) * RWKV_WIDTH) for i in range(3))
    c_lr = slice(3 * RWKV_WIDTH, RWKV_COLS)
    c_conv = slice(RWKV_COLS, ODD_COLS)

    def project(cols):
        return _dot(h, win_ref[:, cols])

    def shift_mix(pr, cols):
        sh = pltpu.roll(pr, 1, 0)
        last_row = prev_ref[SUBLANE - 1:SUBLANE, cols]
        head = jnp.where(_iota((SUBLANE, pr.shape[1]), 0) == 0, last_row, sh[0:SUBLANE])
        prev_ref[:, cols] = pr[tb - SUBLANE:tb]
        sh = jnp.concatenate([head, sh[SUBLANE:]], axis=0)
        return pr + (sh - pr) * mu_ref[:, cols]

    never = jnp.full((CONV_TILE_ROWS, LANE), t_idx, jnp.int32) < 0

    def conv_tile(r0, n, lanes, after):
        first = CONV_HALO - (CONV_KERNEL - 1)
        wn = n + CONV_HALO + SUBLANE
        win = zc_ref[pl.ds(r0, wn), lanes]
        acc = jnp.where(never, jnp.concatenate([after] * (n // SUBLANE), axis=0),
                        jnp.zeros((n, LANE), F32) + cb_ref[:, lanes])
        for ph in range(SUBLANE):
            sh = win if ph == 0 else pltpu.roll(win, wn - ph, 0)
            for j in range(CONV_KERNEL):
                if (j + first) % SUBLANE == ph:
                    aligned_row = (j + first) // SUBLANE * SUBLANE
                    acc = acc + sh[aligned_row:aligned_row + n] * cw_ref[j:j + 1, lanes]
        return acc

    hm = jnp.where((_iota((MXU_TILE, MXU_TILE), 0) >> 6) == (_iota((MXU_TILE, MXU_TILE), 1) >> 6),
                   1.0, 0.0).astype(BF16)
    p_lr = project(c_lr)
    p_k = project(c_k)
    lr = shift_mix(p_lr, c_lr)
    lane = _iota((tb, RWKV_LR_COLS), 1)
    f = jnp.where(lane < RWKV_DECAY_RANK, jnp.tanh(lr),
                  jnp.where(lane < RWKV_DECAY_RANK + RWKV_AAA_RANK, lr, _sigmoid(lr)))
    lo3 = _dot(f.astype(BF16), wlr_ref[...])
    p_r = project(c_r)
    k = shift_mix(p_k, c_k)
    kk = k * kk_ref[...]
    kk = kk * lax.rsqrt(jnp.maximum(_head_sums(kk * kk, hm), 1e-24))
    lo_w, lo_a, lo_g = c_r, c_k, c_v
    lw = -math.exp(-0.5) * _sigmoid(w0_ref[...] + lo3[:, lo_w])
    a = _sigmoid(a0_ref[...] + lo3[:, lo_a])
    gate = lo3[:, lo_g]

    tril2 = jnp.concatenate([_tril_ones(RWKV_CHUNK, BF16)] * 2, axis=1)
    for c in range(n_chunks):
        cum_c, last_c = _chunk_cumsum(tril2, lw[c * RWKV_CHUNK:(c + 1) * RWKV_CHUNK])
        cum_s[c * RWKV_CHUNK:(c + 1) * RWKV_CHUNK, :] = cum_c
        last_s[c * RWKV_CHUNK:(c + 1) * RWKV_CHUNK, :] = last_c
    p_v = project(c_v)
    cum = cum_s[...]
    e_last = jnp.exp(last_s[...])
    for c in range(n_chunks):
        gl_s[c] = e_last[c * RWKV_CHUNK:c * RWKV_CHUNK + SUBLANE]
    e_neg = jnp.exp(-cum)
    e_hat = e_last * e_neg
    k2 = k * (1.0 + (a - 1.0) * ka_ref[...])
    qa = kk * a
    pt_s[...] = (-kk * jnp.exp(cum - lw)).astype(BF16)
    qt_s[...] = (qa * e_neg).astype(BF16)
    kt_s[...] = (k2 * e_neg).astype(BF16)
    qh_s[...] = (qa * e_hat).astype(BF16)
    kh_s[...] = (k2 * e_hat).astype(BF16)
    p_conv = project(c_conv)
    r = shift_mix(p_r, c_r)
    rt_s[...] = (r * jnp.exp(cum)).astype(BF16)
    v = shift_mix(p_v, c_v)
    v_s[...] = v.astype(BF16)
    bonus = _head_sums(r * k2 * rk_ref[...], hm) * v

    zc_ref[CONV_HALO:CONV_HALO + tb, :] = p_conv[:, 0:CONV_WIDTH] * _sigmoid(
        p_conv[:, CONV_WIDTH:2 * CONV_WIDTH])

    ri = _iota((gw, gw), 0)
    ci = _iota((gw, gw), 1)
    same = (ri >> 6) == (ci >> 6)
    strict = same & (ri > ci)
    ri2 = _iota((gw, 2 * gw), 0)
    ci2 = _iota((gw, 2 * gw), 1) & (gw - 1)
    incl2 = ((ri2 >> 6) == (ci2 >> 6)) & (ri2 >= ci2)
    eye = jnp.where(ri == ci, 1.0, 0.0)

    zero_bf = jnp.zeros((gw, gw), BF16)

    def bd(xs):
        return jnp.where(same, jnp.concatenate([xs] * RWKV_GROUP, axis=0), zero_bf)

    def aligned(i, m):
        return i if isinstance(i, int) else pl.multiple_of(i, m)

    def local_stages(ci_):
        probs = [(cc, j) for cc in range(LOCAL_CHUNKS) for j in range(n_groups)]
        idx = range(len(probs))
        base = aligned(ci_ * (LOCAL_CHUNKS * RWKV_CHUNK), LOCAL_CHUNKS * RWKV_CHUNK)

        def blk(ref, cc, j):
            return bd(ref[pl.ds(base + cc * RWKV_CHUNK, RWKV_CHUNK), j * gw:(j + 1) * gw])

        def slot(cc, j):
            return (ci_ * LOCAL_CHUNKS + cc) * n_groups + j

        P2 = [blk(pt_s, cc, j) for cc, j in probs]
        PR = [jnp.concatenate([P2[i], blk(rt_s, cc, j)], axis=0) for i, (cc, j) in enumerate(probs)]
        QK = [jnp.concatenate([blk(qt_s, cc, j), blk(kt_s, cc, j)], axis=0) for cc, j in probs]
        V2 = [blk(v_s, cc, j) for cc, j in probs]
        sc = [_dot_nt(PR[i], QK[i]) for i in idx]
        yield sc[-1][0:SUBLANE, 0:LANE]
        A_pq = [jnp.where(strict, sc[i][0:gw, 0:gw], 0.0) for i in idx]
        A_pk = [jnp.where(strict, sc[i][0:gw, gw:2 * gw], 0.0).astype(BF16) for i in idx]
        for i, (cc, j) in enumerate(probs):
            ark_s[slot(cc, j)] = jnp.where(incl2, sc[i][gw:2 * gw, :], 0.0).astype(BF16)
        Xb = [A_pq[i].astype(BF16) for i in idx]
        Xp = [_dot(Xb[i], Xb[i]) for i in idx]
        av = [_dot(A_pk[i], V2[i]) for i in idx]
        yield av[-1][0:SUBLANE, 0:LANE]
        av = [av[i].astype(BF16) for i in idx]
        Tm = [eye + A_pq[i] for i in idx]
        for step in range(4):
            Xb = [Xp[i].astype(BF16) for i in idx]
            XT = [_dot(Xb[i], jnp.concatenate([Xb[i], Tm[i].astype(BF16)], axis=1)) for i in idx]
            yield XT[-1][0:SUBLANE, 0:LANE]
            Xp = [XT[i][:, 0:gw] for i in idx]
            Tm = [Tm[i] + XT[i][:, gw:2 * gw] for i in idx]
        XT = [_dot(Xp[i].astype(BF16), Tm[i].astype(BF16)) for i in idx]
        yield XT[-1][0:SUBLANE, 0:LANE]
        Tm = [(Tm[i] + XT[i]).astype(BF16) for i in idx]
        uw = [_dot(Tm[i], jnp.concatenate([av[i], P2[i]], axis=1)) for i in idx]
        for i, (cc, j) in enumerate(probs):
            uw_s[slot(cc, j)] = uw[i].astype(BF16)
        yield uw[-1][0:SUBLANE, 0:LANE]

    def state_stages(ci_):
        groups = range(n_groups)
        sls = [slice(j * gw, (j + 1) * gw) for j in groups]
        for cc in range(LOCAL_CHUNKS):
            c = ci_ * LOCAL_CHUNKS + cc
            rows = pl.ds(aligned(c * RWKV_CHUNK, RWKV_CHUNK), RWKV_CHUNK)
            e_last = gl_s[c][0:1, :]
            R2 = [bd(rt_s[rows, s]) for s in sls]
            uw = [uw_s[c * n_groups + j] for j in groups]
            Z = [z_s[j] for j in groups]
            hw = [_dot_nt(jnp.concatenate([uw[j][:, gw:2 * gw], R2[j]], axis=0), Z[j].astype(BF16))
                  for j in groups]
            yield
            QK = [jnp.concatenate([bd(qh_s[rows, s]), bd(kh_s[rows, s])], axis=0) for s in sls]
            V2 = [bd(v_s[rows, s]) for s in sls]
            UV = [jnp.concatenate([(uw[j][:, 0:gw].astype(F32) + hw[j][0:gw]).astype(BF16), V2[j]],
                                  axis=0) for j in groups]
            zn = [_dot_tn(UV[j], QK[j]) for j in groups]
            Y2 = [hw[j][gw:2 * gw] + _dot(ark_s[c * n_groups + j], UV[j]) for j in groups]
            for j in groups:
                z_s[j] = Z[j] * e_last[:, sls[j]] + zn[j]
                y = Y2[j][0:RWKV_CHUNK]
                for e in range(1, RWKV_GROUP):
                    y = y + Y2[j][e * RWKV_CHUNK:(e + 1) * RWKV_CHUNK]
                y_s[rows, sls[j]] = y
            yield

    lane_blocks = CONV_WIDTH // LANE

    conv_tiles = [(r, lb) for r in range(0, LOCAL_CHUNKS * RWKV_CHUNK, CONV_TILE_ROWS)
                  for lb in range(lane_blocks)]
    tiles_per_stage = len(conv_tiles) // LOCAL_STAGES

    def conv_after(ci_, k, after):
        for r, lb in conv_tiles[k * tiles_per_stage:(k + 1) * tiles_per_stage]:
            lanes = slice(lb * LANE, (lb + 1) * LANE)
            r0 = aligned(ci_ * (LOCAL_CHUNKS * RWKV_CHUNK) + r, CONV_TILE_ROWS)
            cv_s[pl.ds(r0, CONV_TILE_ROWS), lanes] = conv_tile(r0, CONV_TILE_ROWS, lanes, after)

    n_groups_t = n_chunks // LOCAL_CHUNKS
    for k, after in enumerate(local_stages(0)):
        conv_after(0, k, after)

    def pipelined(i, carry):
        g_local, g_state = local_stages(i), state_stages(i - 1)
        k = 0
        for which in PIPELINE_ORDER:
            if which == "L":
                conv_after(i, k, next(g_local))
                k += 1
            else:
                next(g_state)
        return carry

    lax.fori_loop(1, n_groups_t, pipelined, 0)
    for _ in state_stages(n_groups_t - 1):
        pass

    o_d = _silu(_layernorm(cv_s[...], clg_ref[...], clb_ref[...], LN_EPS))

    y = y_s[...]
    inv_n = 1.0 / RWKV_HEAD
    mu_y = _head_sums(y, hm) * inv_n
    yc = y - mu_y
    var_y = _head_sums(yc * yc, hm) * inv_n
    y = yc * lax.rsqrt(var_y + RWKV_GN_EPS) * gng_ref[...] + gnb_ref[...] + bonus
    o_c = y * gate

    m = (_dot(o_c.astype(BF16), wout_ref[0:RWKV_WIDTH, :])
         + _dot(o_d.astype(BF16), wout_ref[RWKV_WIDTH:RWKV_WIDTH + CONV_WIDTH, :]))
    o_ref[...] = x + m


def _odd_layer(x, norm_g, w_in, w_out, mu, w0, w2, a0, a2, g2, k_k, k_a, r_k, gn_g, gn_b,
               conv_w, conv_b, cln_g, cln_b, *, tb):
    B, T, D = x.shape
    pad_lr = RWKV_LR_COLS - (RWKV_DECAY_RANK + RWKV_AAA_RANK + RWKV_GATE_RANK)
    w_in = w_in.astype(BF16)
    w_in_r = jnp.concatenate(
        [w_in[:, :RWKV_IN], jnp.zeros((D, pad_lr), BF16), w_in[:, RWKV_IN:]], axis=1)
    mu_r = jnp.concatenate([mu.astype(F32), jnp.zeros((pad_lr,), F32)]).reshape(1, RWKV_COLS)
    wd = RWKV_WIDTH
    r1, r2 = RWKV_DECAY_RANK, RWKV_DECAY_RANK + RWKV_AAA_RANK
    wlr = jnp.zeros((RWKV_LR_COLS, 3 * wd), F32)
    wlr = wlr.at[0:r1, 0:wd].set(w2.astype(F32))
    wlr = wlr.at[r1:r2, wd:2 * wd].set(a2.astype(F32))
    wlr = wlr.at[r2:r2 + RWKV_GATE_RANK, 2 * wd:3 * wd].set(g2.astype(F32))
    row = lambda a: a.astype(F32).reshape(1, -1)
    const = lambda shape: pl.BlockSpec(shape, lambda b, t: (0,) * len(shape),
                                       pipeline_mode=pl.Buffered(1))
    gw = RWKV_GROUP * RWKV_HEAD
    kern = functools.partial(_odd_kernel, tb=tb)
    return pl.pallas_call(
        kern,
        grid=(B, T // tb),
        in_specs=[
            pl.BlockSpec((None, tb, D), lambda b, t: (b, t, 0)),
            const((1, D)), const((D, ODD_COLS)), const((1, RWKV_COLS)), const((RWKV_LR_COLS, 3 * wd)),
            const((1, wd)), const((1, wd)), const((1, wd)), const((1, wd)), const((1, wd)),
            const((1, wd)), const((1, wd)),
            const((CONV_KERNEL, CONV_WIDTH)), const((1, CONV_WIDTH)), const((1, CONV_WIDTH)),
            const((1, CONV_WIDTH)),
            const((D, D)),
        ],
        out_specs=pl.BlockSpec((None, tb, D), lambda b, t: (b, t, 0)),
        out_shape=jax.ShapeDtypeStruct((B, T, D), F32),
        scratch_shapes=[
            pltpu.VMEM((SUBLANE, RWKV_COLS), F32),
            pltpu.VMEM((tb + CONV_HALO + SUBLANE, CONV_WIDTH), F32),
            pltpu.VMEM((tb, CONV_WIDTH), F32),
            pltpu.VMEM((tb, wd), F32), pltpu.VMEM((tb, wd), F32),
            pltpu.VMEM((tb, wd), F32),
            pltpu.VMEM((tb, wd), BF16), pltpu.VMEM((tb, wd), BF16), pltpu.VMEM((tb, wd), BF16),
            pltpu.VMEM((tb, wd), BF16), pltpu.VMEM((tb, wd), BF16), pltpu.VMEM((tb, wd), BF16),
            pltpu.VMEM((tb, wd), BF16),
            pltpu.VMEM((RWKV_WIDTH // gw, gw, gw), F32),
            pltpu.VMEM((tb // RWKV_CHUNK, SUBLANE, RWKV_WIDTH), F32),
            pltpu.VMEM((tb // RWKV_CHUNK * (RWKV_WIDTH // gw), gw, 2 * gw), BF16),
            pltpu.VMEM((tb // RWKV_CHUNK * (RWKV_WIDTH // gw), gw, 2 * gw), BF16),
        ],
        compiler_params=pltpu.CompilerParams(
            dimension_semantics=("arbitrary", "arbitrary"), vmem_limit_bytes=VMEM_LIMIT),
        name="odd_mixer",
    )(x, row(norm_g), w_in_r, mu_r, wlr.astype(BF16), row(w0), row(a0), row(k_k), row(k_a), row(r_k),
      row(gn_g), row(gn_b), conv_w.astype(F32), row(conv_b), row(cln_g), row(cln_b), w_out.astype(BF16))


def _mlp_kernel(x_ref, g_ref, wup_ref, wdn_ref, gf_ref, o_ref, *, ff_chunk, final_norm):
    x = x_ref[...]
    xb = _rms(x, g_ref[...], RMS_EPS).astype(BF16)
    acc = x
    d_ff = wup_ref.shape[1]
    for f0 in range(0, d_ff, ff_chunk):
        hcol = _dot(xb, wup_ref[:, f0:f0 + ff_chunk])
        hcol = jnp.square(jnp.maximum(hcol, 0.0))
        acc = acc + _dot(hcol.astype(BF16), wdn_ref[f0:f0 + ff_chunk, :])
    if final_norm:
        acc = _rms(acc, gf_ref[...], RMS_EPS)
    o_ref[...] = acc


def _mlp_layer(x2, norm_g, w_up, w_down, final_g, *, layer, tm, final_norm):
    M, D = x2.shape
    d_ff = w_up.shape[2]
    kern = functools.partial(_mlp_kernel, ff_chunk=MLP_FF_CHUNK, final_norm=final_norm)
    return pl.pallas_call(
        kern,
        grid=(M // tm,),
        in_specs=[
            pl.BlockSpec((tm, D), lambda i: (i, 0)),
            pl.BlockSpec((1, D), lambda i: (0, 0)),
            pl.BlockSpec((None, D, d_ff), lambda i: (layer, 0, 0), pipeline_mode=pl.Buffered(1)),
            pl.BlockSpec((None, d_ff, D), lambda i: (layer, 0, 0), pipeline_mode=pl.Buffered(1)),
            pl.BlockSpec((1, D), lambda i: (0, 0)),
        ],
        out_specs=pl.BlockSpec((tm, D), lambda i: (i, 0)),
        out_shape=jax.ShapeDtypeStruct((M, D), F32),
        compiler_params=pltpu.CompilerParams(
            dimension_semantics=("arbitrary",), vmem_limit_bytes=VMEM_LIMIT),
        name="mlp_final" if final_norm else "mlp",
    )(x2, norm_g.astype(F32).reshape(1, D), w_up, w_down, final_g.astype(F32).reshape(1, D))


def _pick_block(n, target):
    b = min(n, target)
    while n % b:
        b //= 2
    return b


def kernel(x, norm_mix, norm_ffn, w_up, w_down, norm_final, even_w_in, even_w_out, gla_w_alpha2, gla_b_alpha, gla_norm, sgu_ln_g, sgu_ln_b, sgu_w, sgu_b, odd_w_in, odd_w_out, rwkv_mu, rwkv_w0, rwkv_w2, rwkv_a0, rwkv_a2, rwkv_g2, rwkv_k_k, rwkv_k_a, rwkv_r_k, rwkv_gn_g, rwkv_gn_b, conv_w, conv_b, conv_ln_g, conv_ln_b):
    B, T, D = x.shape
    depth = norm_mix.shape[0]
    tb_even = _pick_block(T, EVEN_BLOCK)
    tb_odd = _pick_block(T, ODD_BLOCK)
    tm = _pick_block(B * T, MLP_BLOCK)
    assert tb_even % GLA_CHUNK == 0 and tb_even % SGU_CHUNK == 0
    assert tb_odd % (LOCAL_CHUNKS * RWKV_CHUNK) == 0
    assert D == GLA_WIDTH + SGU_WIDTH == RWKV_WIDTH + CONV_WIDTH
    w_up_b, w_down_b = w_up.astype(BF16), w_down.astype(BF16)
    for layer in range(depth):
        j = layer // 2
        if layer % 2 == 0:
            x = _even_layer(x, norm_mix[layer], even_w_in[j], even_w_out[j], gla_w_alpha2[j],
                            gla_b_alpha[j], gla_norm[j], sgu_ln_g[j], sgu_ln_b[j], sgu_w[j], sgu_b[j],
                            tb=tb_even)
        else:
            x = _odd_layer(x, norm_mix[layer], odd_w_in[j], odd_w_out[j], rwkv_mu[j], rwkv_w0[j],
                           rwkv_w2[j], rwkv_a0[j], rwkv_a2[j], rwkv_g2[j], rwkv_k_k[j], rwkv_k_a[j],
                           rwkv_r_k[j].reshape(-1), rwkv_gn_g[j], rwkv_gn_b[j], conv_w[j], conv_b[j],
                           conv_ln_g[j], conv_ln_b[j], tb=tb_odd)
        last = layer == depth - 1
        x = _mlp_layer(x.reshape(B * T, D), norm_ffn[layer], w_up_b, w_down_b, norm_final,
                       layer=layer, tm=tm, final_norm=last).reshape(B, T, D)
    return x
```

```python
import functools
import math

import jax
import jax.numpy as jnp
from jax import lax
from jax.experimental import pallas as pl
from jax.experimental.pallas import tpu as pltpu

F32 = jnp.float32
BF16 = jnp.bfloat16

RMS_EPS = 1e-6
LN_EPS = 1e-5

GLA_HEADS = 4
GLA_DK = 64
GLA_DV = 128
GLA_KEY = GLA_HEADS * GLA_DK
GLA_WIDTH = GLA_HEADS * GLA_DV
GLA_GATE_RANK = 16
GLA_TAU = 16.0
GLA_CHUNK = 64
GLA_LOCAL = 16
SGU_WIDTH = 512
SGU_GROUPS = 4
SGU_CHUNK = 128
EVEN_COLS = 2688
EVEN_SPLIT = {"qk": (0, 512), "v": (512, 1024), "g": (1024, 1536), "u": (1536, 2048),
              "sv": (2048, 2560), "alr": (2560, 2688)}

RWKV_WIDTH = 512
RWKV_HEAD = 64
RWKV_HEADS = 8
RWKV_DECAY_RANK = 32
RWKV_AAA_RANK = 32
RWKV_GATE_RANK = 96
RWKV_GN_EPS = 64e-5
RWKV_IN = 1696
RWKV_LR_COLS = 256
RWKV_COLS = 3 * RWKV_WIDTH + RWKV_LR_COLS
RWKV_CHUNK = 64
RWKV_GROUP = 2
LOCAL_CHUNKS = 2
LOCAL_STAGES = 8
PIPELINE_ORDER = "LSLLSLSLLSLL"
CONV_WIDTH = 512
CONV_KERNEL = 31
CONV_HALO = 32
CONV_TILE_ROWS = 32
ODD_COLS = RWKV_COLS + 2 * CONV_WIDTH

LANE = 128
SUBLANE = 8
MXU_TILE = 256
VMEM_LIMIT = 56 * 1024 * 1024
EVEN_BLOCK = 1024
ODD_BLOCK = 512
MLP_BLOCK = 1024
MLP_FF_CHUNK = 1024


def _iota(shape, dim):
    return lax.broadcasted_iota(jnp.int32, shape, dim)


def _dot(a, b):
    return jnp.dot(a, b, preferred_element_type=F32)


def _dot_nt(a, b):
    return lax.dot_general(a, b, (((1,), (1,)), ((), ())), preferred_element_type=F32)


def _dot_tn(a, b):
    return lax.dot_general(a, b, (((0,), (0,)), ((), ())), preferred_element_type=F32)


def _split2(x):
    hi = x.astype(BF16)
    lo = (x - hi.astype(F32)).astype(BF16)
    return hi, lo


def _chunk_cumsum(tril2, x):
    hi, lo = _split2(x)
    cum = _dot(tril2, jnp.concatenate([hi, lo], axis=0))
    n = x.shape[0]
    return cum, jnp.broadcast_to(cum[n - 1:n], cum.shape)


def _head_sums(x, hm_half):
    half = hm_half.shape[0]
    xb = x.astype(BF16)
    return jnp.concatenate([_dot(xb[:, 0:half], hm_half), _dot(xb[:, half:2 * half], hm_half)], axis=1)


def _rms(x, g, eps):
    return x * lax.rsqrt(jnp.mean(x * x, axis=-1, keepdims=True) + eps) * g


def _layernorm(x, g, b, eps):
    mu = jnp.mean(x, axis=-1, keepdims=True)
    xc = x - mu
    var = jnp.mean(xc * xc, axis=-1, keepdims=True)
    return xc * lax.rsqrt(var + eps) * g + b


def _sigmoid(x):
    return 1.0 / (1.0 + jnp.exp(-x))


def _silu(x):
    return x * _sigmoid(x)


def _softplus(x):
    return jnp.maximum(x, 0.0) + jnp.log(1.0 + jnp.exp(-jnp.abs(x)))


def _gelu_tanh(x):
    c = 0.7978845608028654
    return 0.5 * x * (1.0 + jnp.tanh(c * (x + 0.044715 * (x * x * x))))


def _tril_ones(n, dtype):
    return jnp.where(_iota((n, n), 0) >= _iota((n, n), 1), 1.0, 0.0).astype(dtype)


def _even_kernel(x_ref, gn_ref, win_ref, wal_ref, bal_ref, glan_ref, lng_ref, lnb_ref,
                 sguw_ref, sgub_ref, wout_ref, o_ref,
                 la_ref, bl_ref, dl_ref, qd_ref, kd_ref, ks_ref, vb_ref, gt_ref, oa_ref, ob_ref, st_ref,
                 *, tb):
    t_idx = pl.program_id(1)

    @pl.when(t_idx == 0)
    def _():
        st_ref[...] = jnp.zeros_like(st_ref)

    x = x_ref[...]
    h = _rms(x, gn_ref[...], RMS_EPS).astype(BF16)

    def project(name):
        c0, c1 = EVEN_SPLIT[name]
        return _dot(h, win_ref[:, c0:c1])

    p_alr = project("alr")
    p_sv = project("sv")
    a_hi, a_lo = _split2(p_alr)
    w_hi, w_lo = _split2(wal_ref[...])
    z = _dot(a_hi, w_hi) + _dot(a_hi, w_lo) + _dot(a_lo, w_hi) + bal_ref[...]
    la = -_softplus(-z) * (1.0 / GLA_TAU)
    p_u = project("u")
    sv =_layernorm(_gelu_tanh(p_sv), lng_ref[...], lnb_ref[...], LN_EPS).astype(BF16)

    n_chunks = tb // GLA_CHUNK
    tril2 = jnp.concatenate([_tril_ones(GLA_CHUNK, BF16)] * 2, axis=1)
    for c in range(n_chunks):
        cum_c, last_c = _chunk_cumsum(tril2, la[c * GLA_CHUNK:(c + 1) * GLA_CHUNK])
        la_ref[c * GLA_CHUNK:(c + 1) * GLA_CHUNK, :] = cum_c
        bl_ref[c * GLA_CHUNK:(c + 1) * GLA_CHUNK, :] = last_c
    p_qk = project("qk")
    ug = _gelu_tanh(p_u)

    tri128 = _iota((SGU_CHUNK, SGU_CHUNK), 0) >= _iota((SGU_CHUNK, SGU_CHUNK), 1)
    for gg in range(SGU_GROUPS):
        wg = jnp.where(tri128, sguw_ref[gg], 0.0).astype(BF16)
        for c in range(tb // SGU_CHUNK):
            r0 = c * SGU_CHUNK
            s = _dot(wg, sv[r0:r0 + SGU_CHUNK, gg * LANE:(gg + 1) * LANE])
            s = s + sgub_ref[:, gg * LANE:(gg + 1) * LANE]
            ob_ref[r0:r0 + SGU_CHUNK, gg * LANE:(gg + 1) * LANE] = (
                ug[r0:r0 + SGU_CHUNK, gg * LANE:(gg + 1) * LANE] * s)

    p_v = project("v")
    b = la_ref[...]
    b_last = bl_ref[...]
    for c in range(n_chunks):
        dl_ref[c] = jnp.exp(b_last[c * GLA_CHUNK:c * GLA_CHUNK + SUBLANE])
    k = p_qk[:, GLA_KEY:2 * GLA_KEY]
    qd_ref[...] = (p_qk[:, 0:GLA_KEY] * (GLA_DK ** -0.5) * jnp.exp(b)).astype(BF16)
    kd_ref[...] = (k * jnp.exp(-b)).astype(BF16)
    ks_ref[...] = (k * jnp.exp(b_last - b)).astype(BF16)
    p_g = project("g")
    vb_ref[...] = p_v.astype(BF16)
    gt_ref[...] = glan_ref[...] * _silu(p_g)

    rk = _iota((GLA_KEY, GLA_KEY), 0) >> 6
    ck = _iota((GLA_KEY, GLA_KEY), 1) >> 6
    k4_mask = rk == ck
    rv = _iota((GLA_KEY, GLA_WIDTH), 0) >> 6
    cv = _iota((GLA_KEY, GLA_WIDTH), 1) >> 7
    vbd_mask = rv == cv
    causal = _iota((GLA_CHUNK, GLA_KEY), 0) >= (_iota((GLA_CHUNK, GLA_KEY), 1) & 63)
    rs = _iota((GLA_WIDTH, GLA_KEY), 0) >> 7
    cs = _iota((GLA_WIDTH, GLA_KEY), 1) >> 6
    st_mask = rs == cs

    gla_local = math.gcd(GLA_LOCAL, n_chunks)
    zero_k = jnp.zeros((GLA_KEY, GLA_KEY), BF16)
    zero_v = jnp.zeros((GLA_KEY, GLA_WIDTH), BF16)

    def gla_step(ci_, carry):
        base = pl.multiple_of(ci_ * (gla_local * GLA_CHUNK), gla_local * GLA_CHUNK)
        ccs = range(gla_local)
        rows = [pl.ds(base + cc * GLA_CHUNK, GLA_CHUNK) for cc in ccs]
        q_dec = [qd_ref[r, :] for r in rows]
        k4 = [jnp.where(k4_mask, jnp.concatenate([kd_ref[r, :]] * GLA_HEADS, axis=0), zero_k)
              for r in rows]
        v = [vb_ref[r, :] for r in rows]
        v_bd = [jnp.where(vbd_mask, jnp.concatenate([v[cc]] * GLA_HEADS, axis=0), zero_v) for cc in ccs]
        att = [_dot_nt(q_dec[cc], k4[cc]) for cc in ccs]
        kv = [_dot_tn(v[cc], ks_ref[rows[cc], :]) for cc in ccs]
        att = [jnp.where(causal, att[cc], 0.0).astype(BF16) for cc in ccs]
        o_intra = [_dot(att[cc], v_bd[cc]) for cc in ccs]
        st = st_ref[...]
        for cc in ccs:
            o = o_intra[cc] + _dot_nt(q_dec[cc], st.astype(BF16))
            dl = dl_ref[ci_ * gla_local + cc][0:1, :]
            st = st * dl + jnp.where(st_mask, kv[cc], 0.0)
            for hh in range(GLA_HEADS):
                oh = o[:, hh * GLA_DV:(hh + 1) * GLA_DV]
                oh = oh * lax.rsqrt(jnp.mean(oh * oh, axis=-1, keepdims=True) + RMS_EPS)
                oa_ref[rows[cc], hh * GLA_DV:(hh + 1) * GLA_DV] = oh
        st_ref[...] = st
        return carry

    lax.fori_loop(0, n_chunks // gla_local, gla_step, 0)

    o_a = oa_ref[...] * gt_ref[...]
    m = (_dot(o_a.astype(BF16), wout_ref[0:GLA_WIDTH, :])
         + _dot(ob_ref[...].astype(BF16), wout_ref[GLA_WIDTH:GLA_WIDTH + SGU_WIDTH, :]))
    o_ref[...] = x + m


def _even_layer(x, norm_g, w_in, w_out, w_alpha2, b_alpha, gla_norm, ln_g, ln_b, sgu_w, sgu_b, *, tb):
    B, T, D = x.shape
    w_in = w_in.astype(BF16)
    n_qkvg = 2 * GLA_KEY + 2 * GLA_WIDTH
    n_ref = n_qkvg + GLA_GATE_RANK + 2 * SGU_WIDTH
    w_in_r = jnp.concatenate(
        [w_in[:, :n_qkvg], w_in[:, n_qkvg + GLA_GATE_RANK:n_ref], w_in[:, n_qkvg:n_qkvg + GLA_GATE_RANK],
         jnp.zeros((D, EVEN_COLS - n_ref), BF16)], axis=1)
    wal = jnp.zeros((LANE, GLA_KEY), F32).at[:GLA_GATE_RANK].set(w_alpha2.astype(F32))
    sgub = jnp.repeat(sgu_b.astype(F32).T, LANE, axis=1)
    row = lambda a: a.astype(F32).reshape(1, -1)
    const = lambda shape: pl.BlockSpec(shape, lambda b, t: (0,) * len(shape),
                                       pipeline_mode=pl.Buffered(1))
    kern = functools.partial(_even_kernel, tb=tb)
    return pl.pallas_call(
        kern,
        grid=(B, T // tb),
        in_specs=[
            pl.BlockSpec((None, tb, D), lambda b, t: (b, t, 0)),
            const((1, D)), const((D, EVEN_COLS)), const((LANE, GLA_KEY)), const((1, GLA_KEY)),
            const((1, GLA_WIDTH)), const((1, SGU_WIDTH)), const((1, SGU_WIDTH)),
            const((SGU_GROUPS, SGU_CHUNK, SGU_CHUNK)), const((SGU_CHUNK, SGU_WIDTH)),
            const((D, D)),
        ],
        out_specs=pl.BlockSpec((None, tb, D), lambda b, t: (b, t, 0)),
        out_shape=jax.ShapeDtypeStruct((B, T, D), F32),
        scratch_shapes=[
            pltpu.VMEM((tb, GLA_KEY), F32),
            pltpu.VMEM((tb, GLA_KEY), F32),
            pltpu.VMEM((tb // GLA_CHUNK, SUBLANE, GLA_KEY), F32),
            pltpu.VMEM((tb, GLA_KEY), BF16), pltpu.VMEM((tb, GLA_KEY), BF16),
            pltpu.VMEM((tb, GLA_KEY), BF16),
            pltpu.VMEM((tb, GLA_WIDTH), BF16),
            pltpu.VMEM((tb, GLA_WIDTH), F32),
            pltpu.VMEM((tb, GLA_WIDTH), F32),
            pltpu.VMEM((tb, SGU_WIDTH), F32),
            pltpu.VMEM((GLA_WIDTH, GLA_KEY), F32),
        ],
        compiler_params=pltpu.CompilerParams(
            dimension_semantics=("arbitrary", "arbitrary"), vmem_limit_bytes=VMEM_LIMIT),
        name="even_mixer",
    )(x, row(norm_g), w_in_r, wal, row(b_alpha), row(gla_norm), row(ln_g), row(ln_b),
      sgu_w.astype(F32), sgub, w_out.astype(BF16))


def _odd_kernel(x_ref, gn_ref, win_ref, mu_ref, wlr_ref, w0_ref, a0_ref, kk_ref, ka_ref, rk_ref,
                gng_ref, gnb_ref, cw_ref, cb_ref, clg_ref, clb_ref, wout_ref, o_ref,
                prev_ref, zc_ref, cv_s, cum_s, last_s, y_s, rt_s, pt_s, qt_s, kt_s, qh_s, kh_s, v_s,
                z_s, gl_s, ark_s, uw_s, *, tb):
    t_idx = pl.program_id(1)
    gw = RWKV_GROUP * RWKV_HEAD
    n_groups = RWKV_WIDTH // gw
    n_chunks = tb // RWKV_CHUNK

    @pl.when(t_idx == 0)
    def _():
        prev_ref[...] = jnp.zeros_like(prev_ref)
        zc_ref[0:CONV_HALO, :] = jnp.zeros((CONV_HALO, CONV_WIDTH), F32)
        zc_ref[tb + CONV_HALO:tb + CONV_HALO + SUBLANE, :] = jnp.zeros((SUBLANE, CONV_WIDTH), F32)
        z_s[...] = jnp.zeros_like(z_s)

    @pl.when(t_idx != 0)
    def _():
        zc_ref[0:CONV_HALO, :] = zc_ref[tb:tb + CONV_HALO, :]

    x = x_ref[...]
    h = _rms(x, gn_ref[...], RMS_EPS).astype(BF16)
    c_r, c_k, c_v = (slice(i * RWKV_WIDTH, (i + 1) * RWKV_WIDTH) for i in range(3))
    c_lr = slice(3 * RWKV_WIDTH, RWKV_COLS)
    c_conv = slice(RWKV_COLS, ODD_COLS)

    def project(cols):
        return _dot(h, win_ref[:, cols])

    def shift_mix(pr, cols):
        sh = pltpu.roll(pr, 1, 0)
        last_row = prev_ref[SUBLANE - 1:SUBLANE, cols]
        head = jnp.where(_iota((SUBLANE, pr.shape[1]), 0) == 0, last_row, sh[0:SUBLANE])
        prev_ref[:, cols] = pr[tb - SUBLANE:tb]
        sh = jnp.concatenate([head, sh[SUBLANE:]], axis=0)
        return pr + (sh - pr) * mu_ref[:, cols]

    never = jnp.full((CONV_TILE_ROWS, LANE), t_idx, jnp.int32) < 0

    def conv_tile(r0, n, lanes, after):
        first = CONV_HALO - (CONV_KERNEL - 1)
        wn = n + CONV_HALO + SUBLANE
        win = zc_ref[pl.ds(r0, wn), lanes]
        acc = jnp.where(never, jnp.concatenate([after] * (n // SUBLANE), axis=0),
                        jnp.zeros((n, LANE), F32) + cb_ref[:, lanes])
        for ph in range(SUBLANE):
            sh = win if ph == 0 else pltpu.roll(win, wn - ph, 0)
            for j in range(CONV_KERNEL):
                if (j + first) % SUBLANE == ph:
                    aligned_row = (j + first) // SUBLANE * SUBLANE
                    acc = acc + sh[aligned_row:aligned_row + n] * cw_ref[j:j + 1, lanes]
        return acc

    hm = jnp.where((_iota((MXU_TILE, MXU_TILE), 0) >> 6) == (_iota((MXU_TILE, MXU_TILE), 1) >> 6),
                   1.0, 0.0).astype(BF16)
    p_lr = project(c_lr)
    p_k = project(c_k)
    lr = shift_mix(p_lr, c_lr)
    lane = _iota((tb, RWKV_LR_COLS), 1)
    f = jnp.where(lane < RWKV_DECAY_RANK, jnp.tanh(lr),
                  jnp.where(lane < RWKV_DECAY_RANK + RWKV_AAA_RANK, lr, _sigmoid(lr)))
    lo3 = _dot(f.astype(BF16), wlr_ref[...])
    p_r = project(c_r)
    k = shift_mix(p_k, c_k)
    kk = k * kk_ref[...]
    kk = kk * lax.rsqrt(jnp.maximum(_head_sums(kk * kk, hm), 1e-24))
    lo_w, lo_a, lo_g = c_r, c_k, c_v
    lw = -math.exp(-0.5) * _sigmoid(w0_ref[...] + lo3[:, lo_w])
    a = _sigmoid(a0_ref[...] + lo3[:, lo_a])
    gate = lo3[:, lo_g]

    tril2 = jnp.concatenate([_tril_ones(RWKV_CHUNK, BF16)] * 2, axis=1)
    for c in range(n_chunks):
        cum_c, last_c = _chunk_cumsum(tril2, lw[c * RWKV_CHUNK:(c + 1) * RWKV_CHUNK])
        cum_s[c * RWKV_CHUNK:(c + 1) * RWKV_CHUNK, :] = cum_c
        last_s[c * RWKV_CHUNK:(c + 1) * RWKV_CHUNK, :] = last_c
    p_v = project(c_v)
    cum = cum_s[...]
    e_last = jnp.exp(last_s[...])
    for c in range(n_chunks):
        gl_s[c] = e_last[c * RWKV_CHUNK:c * RWKV_CHUNK + SUBLANE]
    e_neg = jnp.exp(-cum)
    e_hat = e_last * e_neg
    k2 = k * (1.0 + (a - 1.0) * ka_ref[...])
    qa = kk * a
    pt_s[...] = (-kk * jnp.exp(cum - lw)).astype(BF16)
    qt_s[...] = (qa * e_neg).astype(BF16)
    kt_s[...] = (k2 * e_neg).astype(BF16)
    qh_s[...] = (qa * e_hat).astype(BF16)
    kh_s[...] = (k2 * e_hat).astype(BF16)
    p_conv = project(c_conv)
    r = shift_mix(p_r, c_r)
    rt_s[...] = (r * jnp.exp(cum)).astype(BF16)
    v = shift_mix(p_v, c_v)
    v_s[...] = v.astype(BF16)
    bonus = _head_sums(r * k2 * rk_ref[...], hm) * v

    zc_ref[CONV_HALO:CONV_HALO + tb, :] = p_conv[:, 0:CONV_WIDTH] * _sigmoid(
        p_conv[:, CONV_WIDTH:2 * CONV_WIDTH])

    ri = _iota((gw, gw), 0)
    ci = _iota((gw, gw), 1)
    same = (ri >> 6) == (ci >> 6)
    strict = same & (ri > ci)
    ri2 = _iota((gw, 2 * gw), 0)
    ci2 = _iota((gw, 2 * gw), 1) & (gw - 1)
    incl2 = ((ri2 >> 6) == (ci2 >> 6)) & (ri2 >= ci2)
    eye = jnp.where(ri == ci, 1.0, 0.0)

    zero_bf = jnp.zeros((gw, gw), BF16)

    def bd(xs):
        return jnp.where(same, jnp.concatenate([xs] * RWKV_GROUP, axis=0), zero_bf)

    def aligned(i, m):
        return i if isinstance(i, int) else pl.multiple_of(i, m)

    def local_stages(ci_):
        probs = [(cc, j) for cc in range(LOCAL_CHUNKS) for j in range(n_groups)]
        idx = range(len(probs))
        base = aligned(ci_ * (LOCAL_CHUNKS * RWKV_CHUNK), LOCAL_CHUNKS * RWKV_CHUNK)

        def blk(ref, cc, j):
            return bd(ref[pl.ds(base + cc * RWKV_CHUNK, RWKV_CHUNK), j * gw:(j + 1) * gw])

        def slot(cc, j):
            return (ci_ * LOCAL_CHUNKS + cc) * n_groups + j

        P2 = [blk(pt_s, cc, j) for cc, j in probs]
        PR = [jnp.concatenate([P2[i], blk(rt_s, cc, j)], axis=0) for i, (cc, j) in enumerate(probs)]
        QK = [jnp.concatenate([blk(qt_s, cc, j), blk(kt_s, cc, j)], axis=0) for cc, j in probs]
        V2 = [blk(v_s, cc, j) for cc, j in probs]
        sc = [_dot_nt(PR[i], QK[i]) for i in idx]
        yield sc[-1][0:SUBLANE, 0:LANE]
        A_pq = [jnp.where(strict, sc[i][0:gw, 0:gw], 0.0) for i in idx]
        A_pk = [jnp.where(strict, sc[i][0:gw, gw:2 * gw], 0.0).astype(BF16) for i in idx]
        for i, (cc, j) in enumerate(probs):
            ark_s[slot(cc, j)] = jnp.where(incl2, sc[i][gw:2 * gw, :], 0.0).astype(BF16)
        Xb = [A_pq[i].astype(BF16) for i in idx]
        Xp = [_dot(Xb[i], Xb[i]) for i in idx]
        av = [_dot(A_pk[i], V2[i]) for i in idx]
        yield av[-1][0:SUBLANE, 0:LANE]
        av = [av[i].astype(BF16) for i in idx]
        Tm = [eye + A_pq[i] for i in idx]
        for step in range(4):
            Xb = [Xp[i].astype(BF16) for i in idx]
            XT = [_dot(Xb[i], jnp.concatenate([Xb[i], Tm[i].astype(BF16)], axis=1)) for i in idx]
            yield XT[-1][0:SUBLANE, 0:LANE]
            Xp = [XT[i][:, 0:gw] for i in idx]
            Tm = [Tm[i] + XT[i][:, gw:2 * gw] for i in idx]
        XT = [_dot(Xp[i].astype(BF16), Tm[i].astype(BF16)) for i in idx]
        yield XT[-1][0:SUBLANE, 0:LANE]
        Tm = [(Tm[i] + XT[i]).astype(BF16) for i in idx]
        uw = [_dot(Tm[i], jnp.concatenate([av[i], P2[i]], axis=1)) for i in idx]
        for i, (cc, j) in enumerate(probs):
            uw_s[slot(cc, j)] = uw[i].astype(BF16)
        yield uw[-1][0:SUBLANE, 0:LANE]

    def state_stages(ci_):
        groups = range(n_groups)
        sls = [slice(j * gw, (j + 1) * gw) for j in groups]
        for cc in range(LOCAL_CHUNKS):
            c = ci_ * LOCAL_CHUNKS + cc
            rows = pl.ds(aligned(c * RWKV_CHUNK, RWKV_CHUNK), RWKV_CHUNK)
            e_last = gl_s[c][0:1, :]
            R2 = [bd(rt_s[rows, s]) for s in sls]
            uw = [uw_s[c * n_groups + j] for j in groups]
            Z = [z_s[j] for j in groups]
            hw = [_dot_nt(jnp.concatenate([uw[j][:, gw:2 * gw], R2[j]], axis=0), Z[j].astype(BF16))
                  for j in groups]
            yield
            QK = [jnp.concatenate([bd(qh_s[rows, s]), bd(kh_s[rows, s])], axis=0) for s in sls]
            V2 = [bd(v_s[rows, s]) for s in sls]
            UV = [jnp.concatenate([(uw[j][:, 0:gw].astype(F32) + hw[j][0:gw]).astype(BF16), V2[j]],
                                  axis=0) for j in groups]
            zn = [_dot_tn(UV[j], QK[j]) for j in groups]
            Y2 = [hw[j][gw:2 * gw] + _dot(ark_s[c * n_groups + j], UV[j]) for j in groups]
            for j in groups:
                z_s[j] = Z[j] * e_last[:, sls[j]] + zn[j]
                y = Y2[j][0:RWKV_CHUNK]
                for e in range(1, RWKV_GROUP):
                    y = y + Y2[j][e * RWKV_CHUNK:(e + 1) * RWKV_CHUNK]
                y_s[rows, sls[j]] = y
            yield

    lane_blocks = CONV_WIDTH // LANE

    conv_tiles = [(r, lb) for r in range(0, LOCAL_CHUNKS * RWKV_CHUNK, CONV_TILE_ROWS)
                  for lb in range(lane_blocks)]
    tiles_per_stage = len(conv_tiles) // LOCAL_STAGES

    def conv_after(ci_, k, after):
        for r, lb in conv_tiles[k * tiles_per_stage:(k + 1) * tiles_per_stage]:
            lanes = slice(lb * LANE, (lb + 1) * LANE)
            r0 = aligned(ci_ * (LOCAL_CHUNKS * RWKV_CHUNK) + r, CONV_TILE_ROWS)
            cv_s[pl.ds(r0, CONV_TILE_ROWS), lanes] = conv_tile(r0, CONV_TILE_ROWS, lanes, after)

    n_groups_t = n_chunks // LOCAL_CHUNKS
    for k, after in enumerate(local_stages(0)):
        conv_after(0, k, after)

    def pipelined(i, carry):
        g_local, g_state = local_stages(i), state_stages(i - 1)
        k = 0
        for which in PIPELINE_ORDER:
            if which == "L":
                conv_after(i, k, next(g_local))
                k += 1
            else:
                next(g_state)
        return carry

    for i in range(1, n_groups_t):
        pipelined(i, 0)
    for _ in state_stages(n_groups_t - 1):
        pass

    o_d = _silu(_layernorm(cv_s[...], clg_ref[...], clb_ref[...], LN_EPS))

    y = y_s[...]
    inv_n = 1.0 / RWKV_HEAD
    mu_y = _head_sums(y, hm) * inv_n
    yc = y - mu_y
    var_y = _head_sums(yc * yc, hm) * inv_n
    y = yc * lax.rsqrt(var_y + RWKV_GN_EPS) * gng_ref[...] + gnb_ref[...] + bonus
    o_c = y * gate

    m = (_dot(o_c.astype(BF16), wout_ref[0:RWKV_WIDTH, :])
         + _dot(o_d.astype(BF16), wout_ref[RWKV_WIDTH:RWKV_WIDTH + CONV_WIDTH, :]))
    o_ref[...] = x + m


def _odd_layer(x, norm_g, w_in, w_out, mu, w0, w2, a0, a2, g2, k_k, k_a, r_k, gn_g, gn_b,
               conv_w, conv_b, cln_g, cln_b, *, tb):
    B, T, D = x.shape
    pad_lr = RWKV_LR_COLS - (RWKV_DECAY_RANK + RWKV_AAA_RANK + RWKV_GATE_RANK)
    w_in = w_in.astype(BF16)
    w_in_r = jnp.concatenate(
        [w_in[:, :RWKV_IN], jnp.zeros((D, pad_lr), BF16), w_in[:, RWKV_IN:]], axis=1)
    mu_r = jnp.concatenate([mu.astype(F32), jnp.zeros((pad_lr,), F32)]).reshape(1, RWKV_COLS)
    wd = RWKV_WIDTH
    r1, r2 = RWKV_DECAY_RANK, RWKV_DECAY_RANK + RWKV_AAA_RANK
    wlr = jnp.zeros((RWKV_LR_COLS, 3 * wd), F32)
    wlr = wlr.at[0:r1, 0:wd].set(w2.astype(F32))
    wlr = wlr.at[r1:r2, wd:2 * wd].set(a2.astype(F32))
    wlr = wlr.at[r2:r2 + RWKV_GATE_RANK, 2 * wd:3 * wd].set(g2.astype(F32))
    row = lambda a: a.astype(F32).reshape(1, -1)
    const = lambda shape: pl.BlockSpec(shape, lambda b, t: (0,) * len(shape),
                                       pipeline_mode=pl.Buffered(1))
    gw = RWKV_GROUP * RWKV_HEAD
    kern = functools.partial(_odd_kernel, tb=tb)
    return pl.pallas_call(
        kern,
        grid=(B, T // tb),
        in_specs=[
            pl.BlockSpec((None, tb, D), lambda b, t: (b, t, 0)),
            const((1, D)), const((D, ODD_COLS)), const((1, RWKV_COLS)), const((RWKV_LR_COLS, 3 * wd)),
            const((1, wd)), const((1, wd)), const((1, wd)), const((1, wd)), const((1, wd)),
            const((1, wd)), const((1, wd)),
            const((CONV_KERNEL, CONV_WIDTH)), const((1, CONV_WIDTH)), const((1, CONV_WIDTH)),
            const((1, CONV_WIDTH)),
            const((D, D)),
        ],
        out_specs=pl.BlockSpec((None, tb, D), lambda b, t: (b, t, 0)),
        out_shape=jax.ShapeDtypeStruct((B, T, D), F32),
        scratch_shapes=[
            pltpu.VMEM((SUBLANE, RWKV_COLS), F32),
            pltpu.VMEM((tb + CONV_HALO + SUBLANE, CONV_WIDTH), F32),
            pltpu.VMEM((tb, CONV_WIDTH), F32),
            pltpu.VMEM((tb, wd), F32), pltpu.VMEM((tb, wd), F32),
            pltpu.VMEM((tb, wd), F32),
            pltpu.VMEM((tb, wd), BF16), pltpu.VMEM((tb, wd), BF16), pltpu.VMEM((tb, wd), BF16),
            pltpu.VMEM((tb, wd), BF16), pltpu.VMEM((tb, wd), BF16), pltpu.VMEM((tb, wd), BF16),
            pltpu.VMEM((tb, wd), BF16),
            pltpu.VMEM((RWKV_WIDTH // gw, gw, gw), F32),
            pltpu.VMEM((tb // RWKV_CHUNK, SUBLANE, RWKV_WIDTH), F32),
            pltpu.VMEM((tb // RWKV_CHUNK * (RWKV_WIDTH // gw), gw, 2 * gw), BF16),
            pltpu.VMEM((tb // RWKV_CHUNK * (RWKV_WIDTH // gw), gw, 2 * gw), BF16),
        ],
        compiler_params=pltpu.CompilerParams(
            dimension_semantics=("arbitrary", "arbitrary"), vmem_limit_bytes=VMEM_LIMIT),
        name="odd_mixer",
    )(x, row(norm_g), w_in_r, mu_r, wlr.astype(BF16), row(w0), row(a0), row(k_k), row(k_a), row(r_k),
      row(gn_g), row(gn_b), conv_w.astype(F32), row(conv_b), row(cln_g), row(cln_b), w_out.astype(BF16))


def _mlp_kernel(x_ref, g_ref, wup_ref, wdn_ref, gf_ref, o_ref, *, ff_chunk, final_norm):
    x = x_ref[...]
    xb = _rms(x, g_ref[...], RMS_EPS).astype(BF16)
    acc = x
    d_ff = wup_ref.shape[1]
    for f0 in range(0, d_ff, ff_chunk):
        hcol = _dot(xb, wup_ref[:, f0:f0 + ff_chunk])
        hcol = jnp.square(jnp.maximum(hcol, 0.0))
        acc = acc + _dot(hcol.astype(BF16), wdn_ref[f0:f0 + ff_chunk, :])
    if final_norm:
        acc = _rms(acc, gf_ref[...], RMS_EPS)
    o_ref[...] = acc


def _mlp_layer(x2, norm_g, w_up, w_down, final_g, *, layer, tm, final_norm):
    M, D = x2.shape
    d_ff = w_up.shape[2]
    kern = functools.partial(_mlp_kernel, ff_chunk=MLP_FF_CHUNK, final_norm=final_norm)
    return pl.pallas_call(
        kern,
        grid=(M // tm,),
        in_specs=[
            pl.BlockSpec((tm, D), lambda i: (i, 0)),
            pl.BlockSpec((1, D), lambda i: (0, 0)),
            pl.BlockSpec((None, D, d_ff), lambda i: (layer, 0, 0), pipeline_mode=pl.Buffered(1)),
            pl.BlockSpec((None, d_ff, D), lambda i: (layer, 0, 0), pipeline_mode=pl.Buffered(1)),
            pl.BlockSpec((1, D), lambda i: (0, 0)),
        ],
        out_specs=pl.BlockSpec((tm, D), lambda i: (i, 0)),
        out_shape=jax.ShapeDtypeStruct((M, D), F32),
        compiler_params=pltpu.CompilerParams(
            dimension_semantics=("arbitrary",), vmem_limit_bytes=VMEM_LIMIT),
        name="mlp_final" if final_norm else "mlp",
    )(x2, norm_g.astype(F32).reshape(1, D), w_up, w_down, final_g.astype(F32).reshape(1, D))


def _pick_block(n, target):
    b = min(n, target)
    while n % b:
        b //= 2
    return b


def kernel(x, norm_mix, norm_ffn, w_up, w_down, norm_final, even_w_in, even_w_out, gla_w_alpha2, gla_b_alpha, gla_norm, sgu_ln_g, sgu_ln_b, sgu_w, sgu_b, odd_w_in, odd_w_out, rwkv_mu, rwkv_w0, rwkv_w2, rwkv_a0, rwkv_a2, rwkv_g2, rwkv_k_k, rwkv_k_a, rwkv_r_k, rwkv_gn_g, rwkv_gn_b, conv_w, conv_b, conv_ln_g, conv_ln_b):
    B, T, D = x.shape
    depth = norm_mix.shape[0]
    tb_even = _pick_block(T, EVEN_BLOCK)
    tb_odd = _pick_block(T, ODD_BLOCK)
    tm = _pick_block(B * T, MLP_BLOCK)
    assert tb_even % GLA_CHUNK == 0 and tb_even % SGU_CHUNK == 0
    assert tb_odd % (LOCAL_CHUNKS * RWKV_CHUNK) == 0
    assert D == GLA_WIDTH + SGU_WIDTH == RWKV_WIDTH + CONV_WIDTH
    w_up_b, w_down_b = w_up.astype(BF16), w_down.astype(BF16)
    for layer in range(depth):
        j = layer // 2
        if layer % 2 == 0:
            x = _even_layer(x, norm_mix[layer], even_w_in[j], even_w_out[j], gla_w_alpha2[j],
                            gla_b_alpha[j], gla_norm[j], sgu_ln_g[j], sgu_ln_b[j], sgu_w[j], sgu_b[j],
                            tb=tb_even)
        else:
            x = _odd_layer(x, norm_mix[layer], odd_w_in[j], odd_w_out[j], rwkv_mu[j], rwkv_w0[j],
                           rwkv_w2[j], rwkv_a0[j], rwkv_a2[j], rwkv_g2[j], rwkv_k_k[j], rwkv_k_a[j],
                           rwkv_r_k[j].reshape(-1), rwkv_gn_g[j], rwkv_gn_b[j], conv_w[j], conv_b[j],
                           conv_ln_g[j], conv_ln_b[j], tb=tb_odd)
        last = layer == depth - 1
        x = _mlp_layer(x.reshape(B * T, D), norm_ffn[layer], w_up_b, w_down_b, norm_final,
                       layer=layer, tm=tm, final_norm=last).reshape(B, T, D)
    return x
```

```python
import functools
import math

import jax
import jax.numpy as jnp
from jax import lax
from jax.experimental import pallas as pl
from jax.experimental.pallas import tpu as pltpu

F32 = jnp.float32
BF16 = jnp.bfloat16

RMS_EPS = 1e-6
LN_EPS = 1e-5

GLA_HEADS = 4
GLA_DK = 64
GLA_DV = 128
GLA_KEY = GLA_HEADS * GLA_DK
GLA_WIDTH = GLA_HEADS * GLA_DV
GLA_GATE_RANK = 16
GLA_TAU = 16.0
GLA_CHUNK = 64
GLA_LOCAL = 16
SGU_WIDTH = 512
SGU_GROUPS = 4
SGU_CHUNK = 128
EVEN_COLS = 2688
EVEN_SPLIT = {"qk": (0, 512), "v": (512, 1024), "g": (1024, 1536), "u": (1536, 2048),
              "sv": (2048, 2560), "alr": (2560, 2688)}

RWKV_WIDTH = 512
RWKV_HEAD = 64
RWKV_HEADS = 8
RWKV_DECAY_RANK = 32
RWKV_AAA_RANK = 32
RWKV_GATE_RANK = 96
RWKV_GN_EPS = 64e-5
RWKV_IN = 1696
RWKV_LR_COLS = 256
RWKV_COLS = 3 * RWKV_WIDTH + RWKV_LR_COLS
RWKV_CHUNK = 64
RWKV_GROUP = 2
LOCAL_CHUNKS = 2
LOCAL_STAGES = 8
PIPELINE_ORDER = "LSLLSLSLLSLL"
CONV_WIDTH = 512
CONV_KERNEL = 31
CONV_HALO = 32
CONV_TILE_ROWS = 32
ODD_COLS = RWKV_COLS + 2 * CONV_WIDTH

LANE = 128
SUBLANE = 8
MXU_TILE = 256
VMEM_LIMIT = 56 * 1024 * 1024
EVEN_BLOCK = 1024
ODD_BLOCK = 512
MLP_BLOCK = 1024
MLP_FF_CHUNK = 1024


def _iota(shape, dim):
    return lax.broadcasted_iota(jnp.int32, shape, dim)


def _dot(a, b):
    return jnp.dot(a, b, preferred_element_type=F32)


def _dot_nt(a, b):
    return lax.dot_general(a, b, (((1,), (1,)), ((), ())), preferred_element_type=F32)


def _dot_tn(a, b):
    return lax.dot_general(a, b, (((0,), (0,)), ((), ())), preferred_element_type=F32)


def _split2(x):
    hi = x.astype(BF16)
    lo = (x - hi.astype(F32)).astype(BF16)
    return hi, lo


def _chunk_cumsum(tril2, x):
    hi, lo = _split2(x)
    cum = _dot(tril2, jnp.concatenate([hi, lo], axis=0))
    n = x.shape[0]
    return cum, jnp.broadcast_to(cum[n - 1:n], cum.shape)


def _head_sums(x, hm_half):
    half = hm_half.shape[0]
    xb = x.astype(BF16)
    return jnp.concatenate([_dot(xb[:, 0:half], hm_half), _dot(xb[:, half:2 * half], hm_half)], axis=1)


def _rms(x, g, eps):
    return x * lax.rsqrt(jnp.mean(x * x, axis=-1, keepdims=True) + eps) * g


def _layernorm(x, g, b, eps):
    mu = jnp.mean(x, axis=-1, keepdims=True)
    xc = x - mu
    var = jnp.mean(xc * xc, axis=-1, keepdims=True)
    return xc * lax.rsqrt(var + eps) * g + b


def _sigmoid(x):
    return 1.0 / (1.0 + jnp.exp(-x))


def _silu(x):
    return x * _sigmoid(x)


def _softplus(x):
    return jnp.maximum(x, 0.0) + jnp.log(1.0 + jnp.exp(-jnp.abs(x)))


def _gelu_tanh(x):
    c = 0.7978845608028654
    return 0.5 * x * (1.0 + jnp.tanh(c * (x + 0.044715 * (x * x * x))))


def _tril_ones(n, dtype):
    return jnp.where(_iota((n, n), 0) >= _iota((n, n), 1), 1.0, 0.0).astype(dtype)


def _even_kernel(x_ref, gn_ref, win_ref, wal_ref, bal_ref, glan_ref, lng_ref, lnb_ref,
                 sguw_ref, sgub_ref, wout_ref, o_ref,
                 la_ref, bl_ref, dl_ref, qd_ref, kd_ref, ks_ref, vb_ref, gt_ref, oa_ref, ob_ref, st_ref,
                 *, tb):
    t_idx = pl.program_id(1)

    @pl.when(t_idx == 0)
    def _():
        st_ref[...] = jnp.zeros_like(st_ref)

    x = x_ref[...]
    h = _rms(x, gn_ref[...], RMS_EPS).astype(BF16)

    def project(name):
        c0, c1 = EVEN_SPLIT[name]
        return _dot(h, win_ref[:, c0:c1])

    p_alr = project("alr")
    p_sv = project("sv")
    a_hi, a_lo = _split2(p_alr)
    w_hi, w_lo = _split2(wal_ref[...])
    z = _dot(a_hi, w_hi) + _dot(a_hi, w_lo) + _dot(a_lo, w_hi) + bal_ref[...]
    la = -_softplus(-z) * (1.0 / GLA_TAU)
    p_u = project("u")
    sv =_layernorm(_gelu_tanh(p_sv), lng_ref[...], lnb_ref[...], LN_EPS).astype(BF16)

    n_chunks = tb // GLA_CHUNK
    tril2 = jnp.concatenate([_tril_ones(GLA_CHUNK, BF16)] * 2, axis=1)
    for c in range(n_chunks):
        cum_c, last_c = _chunk_cumsum(tril2, la[c * GLA_CHUNK:(c + 1) * GLA_CHUNK])
        la_ref[c * GLA_CHUNK:(c + 1) * GLA_CHUNK, :] = cum_c
        bl_ref[c * GLA_CHUNK:(c + 1) * GLA_CHUNK, :] = last_c
    p_qk = project("qk")
    ug = _gelu_tanh(p_u)

    tri128 = _iota((SGU_CHUNK, SGU_CHUNK), 0) >= _iota((SGU_CHUNK, SGU_CHUNK), 1)
    for gg in range(SGU_GROUPS):
        wg = jnp.where(tri128, sguw_ref[gg], 0.0).astype(BF16)
        for c in range(tb // SGU_CHUNK):
            r0 = c * SGU_CHUNK
            s = _dot(wg, sv[r0:r0 + SGU_CHUNK, gg * LANE:(gg + 1) * LANE])
            s = s + sgub_ref[:, gg * LANE:(gg + 1) * LANE]
            ob_ref[r0:r0 + SGU_CHUNK, gg * LANE:(gg + 1) * LANE] = (
                ug[r0:r0 + SGU_CHUNK, gg * LANE:(gg + 1) * LANE] * s)

    p_v = project("v")
    b = la_ref[...]
    b_last = bl_ref[...]
    for c in range(n_chunks):
        dl_ref[c] = jnp.exp(b_last[c * GLA_CHUNK:c * GLA_CHUNK + SUBLANE])
    k = p_qk[:, GLA_KEY:2 * GLA_KEY]
    qd_ref[...] = (p_qk[:, 0:GLA_KEY] * (GLA_DK ** -0.5) * jnp.exp(b)).astype(BF16)
    kd_ref[...] = (k * jnp.exp(-b)).astype(BF16)
    ks_ref[...] = (k * jnp.exp(b_last - b)).astype(BF16)
    p_g = project("g")
    vb_ref[...] = p_v.astype(BF16)
    gt_ref[...] = glan_ref[...] * _silu(p_g)

    rk = _iota((GLA_KEY, GLA_KEY), 0) >> 6
    ck = _iota((GLA_KEY, GLA_KEY), 1) >> 6
    k4_mask = rk == ck
    rv = _iota((GLA_KEY, GLA_WIDTH), 0) >> 6
    cv = _iota((GLA_KEY, GLA_WIDTH), 1) >> 7
    vbd_mask = rv == cv
    causal = _iota((GLA_CHUNK, GLA_KEY), 0) >= (_iota((GLA_CHUNK, GLA_KEY), 1) & 63)
    rs = _iota((GLA_WIDTH, GLA_KEY), 0) >> 7
    cs = _iota((GLA_WIDTH, GLA_KEY), 1) >> 6
    st_mask = rs == cs

    gla_local = math.gcd(GLA_LOCAL, n_chunks)
    zero_k = jnp.zeros((GLA_KEY, GLA_KEY), BF16)
    zero_v = jnp.zeros((GLA_KEY, GLA_WIDTH), BF16)

    def gla_step(ci_, carry):
        base = pl.multiple_of(ci_ * (gla_local * GLA_CHUNK), gla_local * GLA_CHUNK)
        ccs = range(gla_local)
        rows = [pl.ds(base + cc * GLA_CHUNK, GLA_CHUNK) for cc in ccs]
        q_dec = [qd_ref[r, :] for r in rows]
        k4 = [jnp.where(k4_mask, jnp.concatenate([kd_ref[r, :]] * GLA_HEADS, axis=0), zero_k)
              for r in rows]
        v = [vb_ref[r, :] for r in rows]
        v_bd = [jnp.where(vbd_mask, jnp.concatenate([v[cc]] * GLA_HEADS, axis=0), zero_v) for cc in ccs]
        att = [_dot_nt(q_dec[cc], k4[cc]) for cc in ccs]
        kv = [_dot_tn(v[cc], ks_ref[rows[cc], :]) for cc in ccs]
        att = [jnp.where(causal, att[cc], 0.0).astype(BF16) for cc in ccs]
        o_intra = [_dot(att[cc], v_bd[cc]) for cc in ccs]
        st = st_ref[...]
        for cc in ccs:
            o = o_intra[cc] + _dot_nt(q_dec[cc], st.astype(BF16))
            dl = dl_ref[ci_ * gla_local + cc][0:1, :]
            st = st * dl + jnp.where(st_mask, kv[cc], 0.0)
            for hh in range(GLA_HEADS):
                oh = o[:, hh * GLA_DV:(hh + 1) * GLA_DV]
                oh = oh * lax.rsqrt(jnp.mean(oh * oh, axis=-1, keepdims=True) + RMS_EPS)
                oa_ref[rows[cc], hh * GLA_DV:(hh + 1) * GLA_DV] = oh
        st_ref[...] = st
        return carry

    lax.fori_loop(0, n_chunks // gla_local, gla_step, 0)

    o_a = oa_ref[...] * gt_ref[...]
    m = (_dot(o_a.astype(BF16), wout_ref[0:GLA_WIDTH, :])
         + _dot(ob_ref[...].astype(BF16), wout_ref[GLA_WIDTH:GLA_WIDTH + SGU_WIDTH, :]))
    o_ref[...] = x + m


def _even_layer(x, norm_g, w_in, w_out, w_alpha2, b_alpha, gla_norm, ln_g, ln_b, sgu_w, sgu_b, *, tb):
    B, T, D = x.shape
    w_in = w_in.astype(BF16)
    n_qkvg = 2 * GLA_KEY + 2 * GLA_WIDTH
    n_ref = n_qkvg + GLA_GATE_RANK + 2 * SGU_WIDTH
    w_in_r = jnp.concatenate(
        [w_in[:, :n_qkvg], w_in[:, n_qkvg + GLA_GATE_RANK:n_ref], w_in[:, n_qkvg:n_qkvg + GLA_GATE_RANK],
         jnp.zeros((D, EVEN_COLS - n_ref), BF16)], axis=1)
    wal = jnp.zeros((LANE, GLA_KEY), F32).at[:GLA_GATE_RANK].set(w_alpha2.astype(F32))
    sgub = jnp.repeat(sgu_b.astype(F32).T, LANE, axis=1)
    row = lambda a: a.astype(F32).reshape(1, -1)
    const = lambda shape: pl.BlockSpec(shape, lambda b, t: (0,) * len(shape),
                                       pipeline_mode=pl.Buffered(1))
    kern = functools.partial(_even_kernel, tb=tb)
    return pl.pallas_call(
        kern,
        grid=(B, T // tb),
        in_specs=[
            pl.BlockSpec((None, tb, D), lambda b, t: (b, t, 0)),
            const((1, D)), const((D, EVEN_COLS)), const((LANE, GLA_KEY)), const((1, GLA_KEY)),
            const((1, GLA_WIDTH)), const((1, SGU_WIDTH)), const((1, SGU_WIDTH)),
            const((SGU_GROUPS, SGU_CHUNK, SGU_CHUNK)), const((SGU_CHUNK, SGU_WIDTH)),
            const((D, D)),
        ],
        out_specs=pl.BlockSpec((None, tb, D), lambda b, t: (b, t, 0)),
        out_shape=jax.ShapeDtypeStruct((B, T, D), F32),
        scratch_shapes=[
            pltpu.VMEM((tb, GLA_KEY), F32),
            pltpu.VMEM((tb, GLA_KEY), F32),
            pltpu.VMEM((tb // GLA_CHUNK, SUBLANE, GLA_KEY), F32),
            pltpu.VMEM((tb, GLA_KEY), BF16), pltpu.VMEM((tb, GLA_KEY), BF16),
            pltpu.VMEM((tb, GLA_KEY), BF16),
            pltpu.VMEM((tb, GLA_WIDTH), BF16),
            pltpu.VMEM((tb, GLA_WIDTH), F32),
            pltpu.VMEM((tb, GLA_WIDTH), F32),
            pltpu.VMEM((tb, SGU_WIDTH), F32),
            pltpu.VMEM((GLA_WIDTH, GLA_KEY), F32),
        ],
        compiler_params=pltpu.CompilerParams(
            dimension_semantics=("arbitrary", "arbitrary"), vmem_limit_bytes=VMEM_LIMIT),
        name="even_mixer",
    )(x, row(norm_g), w_in_r, wal, row(b_alpha), row(gla_norm), row(ln_g), row(ln_b),
      sgu_w.astype(F32), sgub, w_out.astype(BF16))


def _odd_kernel(x_ref, gn_ref, win_ref, mu_ref, wlr_ref, w0_ref, a0_ref, kk_ref, ka_ref, rk_ref,
                gng_ref, gnb_ref, cw_ref, cb_ref, clg_ref, clb_ref, wout_ref, o_ref,
                prev_ref, zc_ref, cv_s, cum_s, last_s, y_s, rt_s, pt_s, qt_s, kt_s, qh_s, kh_s, v_s,
                z_s, gl_s, ark_s, uw_s, *, tb):
    t_idx = pl.program_id(1)
    gw = RWKV_GROUP * RWKV_HEAD
    n_groups = RWKV_WIDTH // gw
    n_chunks = tb // RWKV_CHUNK

    @pl.when(t_idx == 0)
    def _():
        prev_ref[...] = jnp.zeros_like(prev_ref)
        zc_ref[0:CONV_HALO, :] = jnp.zeros((CONV_HALO, CONV_WIDTH), F32)
        zc_ref[tb + CONV_HALO:tb + CONV_HALO + SUBLANE, :] = jnp.zeros((SUBLANE, CONV_WIDTH), F32)
        z_s[...] = jnp.zeros_like(z_s)

    @pl.when(t_idx != 0)
    def _():
        zc_ref[0:CONV_HALO, :] = zc_ref[tb:tb + CONV_HALO, :]

    x = x_ref[...]
    h = _rms(x, gn_ref[...], RMS_EPS).astype(BF16)
    c_r, c_k, c_v = (slice(i * RWKV_WIDTH, (i + 1) * RWKV_WIDTH) for i in range(3))
    c_lr = slice(3 * RWKV_WIDTH, RWKV_COLS)
    c_conv = slice(RWKV_COLS, ODD_COLS)

    def project(cols):
        return _dot(h, win_ref[:, cols])

    def shift_mix(pr, cols):
        sh = pltpu.roll(pr, 1, 0)
        last_row = prev_ref[SUBLANE - 1:SUBLANE, cols]
        head = jnp.where(_iota((SUBLANE, pr.shape[1]), 0) == 0, last_row, sh[0:SUBLANE])
        prev_ref[:, cols] = pr[tb - SUBLANE:tb]
        sh = jnp.concatenate([head, sh[SUBLANE:]], axis=0)
        return pr + (sh - pr) * mu_ref[:, cols]

    never = jnp.full((CONV_TILE_ROWS, LANE), t_idx, jnp.int32) < 0

    def conv_tile(r0, n, lanes, after):
        first = CONV_HALO - (CONV_KERNEL - 1)
        wn = n + CONV_HALO + SUBLANE
        win = zc_ref[pl.ds(r0, wn), lanes]
        acc = jnp.where(never, jnp.concatenate([after] * (n // SUBLANE), axis=0),
                        jnp.zeros((n, LANE), F32) + cb_ref[:, lanes])
        for ph in range(SUBLANE):
            sh = win if ph == 0 else pltpu.roll(win, wn - ph, 0)
            for j in range(CONV_KERNEL):
                if (j + first) % SUBLANE == ph:
                    aligned_row = (j + first) // SUBLANE * SUBLANE
                    acc = acc + sh[aligned_row:aligned_row + n] * cw_ref[j:j + 1, lanes]
        return acc

    hm = jnp.where((_iota((MXU_TILE, MXU_TILE), 0) >> 6) == (_iota((MXU_TILE, MXU_TILE), 1) >> 6),
                   1.0, 0.0).astype(BF16)
    p_lr = project(c_lr)
    p_k = project(c_k)
    lr = shift_mix(p_lr, c_lr)
    lane = _iota((tb, RWKV_LR_COLS), 1)
    f = jnp.where(lane < RWKV_DECAY_RANK, jnp.tanh(lr),
                  jnp.where(lane < RWKV_DECAY_RANK + RWKV_AAA_RANK, lr, _sigmoid(lr)))
    lo3 = _dot(f.astype(BF16), wlr_ref[...])
    p_r = project(c_r)
    k = shift_mix(p_k, c_k)
    kk = k * kk_ref[...]
    kk = kk * lax.rsqrt(jnp.maximum(_head_sums(kk * kk, hm), 1e-24))
    lo_w, lo_a, lo_g = c_r, c_k, c_v
    lw = -math.exp(-0.5) * _sigmoid(w0_ref[...] + lo3[:, lo_w])
    a = _sigmoid(a0_ref[...] + lo3[:, lo_a])
    gate = lo3[:, lo_g]

    tril2 = jnp.concatenate([_tril_ones(RWKV_CHUNK, BF16)] * 2, axis=1)
    for c in range(n_chunks):
        cum_c, last_c = _chunk_cumsum(tril2, lw[c * RWKV_CHUNK:(c + 1) * RWKV_CHUNK])
        cum_s[c * RWKV_CHUNK:(c + 1) * RWKV_CHUNK, :] = cum_c
        last_s[c * RWKV_CHUNK:(c + 1) * RWKV_CHUNK, :] = last_c
    p_v = project(c_v)
    cum = cum_s[...]
    e_last = jnp.exp(last_s[...])
    for c in range(n_chunks):
        gl_s[c] = e_last[c * RWKV_CHUNK:c * RWKV_CHUNK + SUBLANE]
    e_neg = jnp.exp(-cum)
    e_hat = e_last * e_neg
    k2 = k * (1.0 + (a - 1.0) * ka_ref[...])
    qa = kk * a
    pt_s[...] = (-kk * jnp.exp(cum - lw)).astype(BF16)
    qt_s[...] = (qa * e_neg).astype(BF16)
    kt_s[...] = (k2 * e_neg).astype(BF16)
    qh_s[...] = (qa * e_hat).astype(BF16)
    kh_s[...] = (k2 * e_hat).astype(BF16)
    p_conv = project(c_conv)
    r = shift_mix(p_r, c_r)
    rt_s[...] = (r * jnp.exp(cum)).astype(BF16)
    v = shift_mix(p_v, c_v)
    v_s[...] = v.astype(BF16)
    bonus = _head_sums(r * k2 * rk_ref[...], hm) * v

    zc_ref[CONV_HALO:CONV_HALO + tb, :] = p_conv[:, 0:CONV_WIDTH] * _sigmoid(
        p_conv[:, CONV_WIDTH:2 * CONV_WIDTH])

    ri = _iota((gw, gw), 0)
    ci = _iota((gw, gw), 1)
    same = (ri >> 6) == (ci >> 6)
    strict = same & (ri > ci)
    ri2 = _iota((gw, 2 * gw), 0)
    ci2 = _iota((gw, 2 * gw), 1) & (gw - 1)
    incl2 = ((ri2 >> 6) == (ci2 >> 6)) & (ri2 >= ci2)
    eye = jnp.where(ri == ci, 1.0, 0.0)

    zero_bf = jnp.zeros((gw, gw), BF16)

    def bd(xs):
        return jnp.where(same, jnp.concatenate([xs] * RWKV_GROUP, axis=0), zero_bf)

    def aligned(i, m):
        return i if isinstance(i, int) else pl.multiple_of(i, m)

    def local_stages(ci_):
        probs = [(cc, j) for cc in range(LOCAL_CHUNKS) for j in range(n_groups)]
        idx = range(len(probs))
        base = aligned(ci_ * (LOCAL_CHUNKS * RWKV_CHUNK), LOCAL_CHUNKS * RWKV_CHUNK)

        def blk(ref, cc, j):
            return bd(ref[pl.ds(base + cc * RWKV_CHUNK, RWKV_CHUNK), j * gw:(j + 1) * gw])

        def slot(cc, j):
            return (ci_ * LOCAL_CHUNKS + cc) * n_groups + j

        P2 = [blk(pt_s, cc, j) for cc, j in probs]
        PR = [jnp.concatenate([P2[i], blk(rt_s, cc, j)], axis=0) for i, (cc, j) in enumerate(probs)]
        QK = [jnp.concatenate([blk(qt_s, cc, j), blk(kt_s, cc, j)], axis=0) for cc, j in probs]
        V2 = [blk(v_s, cc, j) for cc, j in probs]
        sc = [_dot_nt(PR[i], QK[i]) for i in idx]
        yield sc[-1][0:SUBLANE, 0:LANE]
        A_pq = [jnp.where(strict, sc[i][0:gw, 0:gw], 0.0) for i in idx]
        A_pk = [jnp.where(strict, sc[i][0:gw, gw:2 * gw], 0.0).astype(BF16) for i in idx]
        for i, (cc, j) in enumerate(probs):
            ark_s[slot(cc, j)] = jnp.where(incl2, sc[i][gw:2 * gw, :], 0.0).astype(BF16)
        Xb = [A_pq[i].astype(BF16) for i in idx]
        Xp = [_dot(Xb[i], Xb[i]) for i in idx]
        av = [_dot(A_pk[i], V2[i]) for i in idx]
        yield av[-1][0:SUBLANE, 0:LANE]
        av = [av[i].astype(BF16) for i in idx]
        Tm = [eye + A_pq[i] for i in idx]
        for step in range(4):
            Xb = [Xp[i].astype(BF16) for i in idx]
            XT = [_dot(Xb[i], jnp.concatenate([Xb[i], Tm[i].astype(BF16)], axis=1)) for i in idx]
            yield XT[-1][0:SUBLANE, 0:LANE]
            Xp = [XT[i][:, 0:gw] for i in idx]
            Tm = [Tm[i] + XT[i][:, gw:2 * gw] for i in idx]
        XT = [_dot(Xp[i].astype(BF16), Tm[i].astype(BF16)) for i in idx]
        yield XT[-1][0:SUBLANE, 0:LANE]
        Tm = [(Tm[i] + XT[i]).astype(BF16) for i in idx]
        uw = [_dot(Tm[i], jnp.concatenate([av[i], P2[i]], axis=1)) for i in idx]
        for i, (cc, j) in enumerate(probs):
            uw_s[slot(cc, j)] = uw[i].astype(BF16)
        yield uw[-1][0:SUBLANE, 0:LANE]

    def state_stages(ci_):
        groups = range(n_groups)
        sls = [slice(j * gw, (j + 1) * gw) for j in groups]
        for cc in range(LOCAL_CHUNKS):
            c = ci_ * LOCAL_CHUNKS + cc
            rows = pl.ds(aligned(c * RWKV_CHUNK, RWKV_CHUNK), RWKV_CHUNK)
            e_last = gl_s[c][0:1, :]
            R2 = [bd(rt_s[rows, s]) for s in sls]
            uw = [uw_s[c * n_groups + j] for j in groups]
            Z = [z_s[j] for j in groups]
            hw = [_dot_nt(jnp.concatenate([uw[j][:, gw:2 * gw], R2[j]], axis=0), Z[j].astype(BF16))
                  for j in groups]
            yield
            QK = [jnp.concatenate([bd(qh_s[rows, s]), bd(kh_s[rows, s])], axis=0) for s in sls]
            V2 = [bd(v_s[rows, s]) for s in sls]
            UV = [jnp.concatenate([(uw[j][:, 0:gw].astype(F32) + hw[j][0:gw]).astype(BF16), V2[j]],
                                  axis=0) for j in groups]
            zn = [_dot_tn(UV[j], QK[j]) for j in groups]
            Y2 = [hw[j][gw:2 * gw] + _dot(ark_s[c * n_groups + j], UV[j]) for j in groups]
            for j in groups:
                z_s[j] = Z[j] * e_last[:, sls[j]] + zn[j]
                y = Y2[j][0:RWKV_CHUNK]
                for e in range(1, RWKV_GROUP):
                    y = y + Y2[j][e * RWKV_CHUNK:(e + 1) * RWKV_CHUNK]
                y_s[rows, sls[j]] = y
            yield

    lane_blocks = CONV_WIDTH // LANE

    conv_tiles = [(r, lb) for r in range(0, LOCAL_CHUNKS * RWKV_CHUNK, CONV_TILE_ROWS)
                  for lb in range(lane_blocks)]
    tiles_per_stage = len(conv_tiles) // LOCAL_STAGES

    def conv_after(ci_, k, after):
        for r, lb in conv_tiles[k * tiles_per_stage:(k + 1) * tiles_per_stage]:
            lanes = slice(lb * LANE, (lb + 1) * LANE)
            r0 = aligned(ci_ * (LOCAL_CHUNKS * RWKV_CHUNK) + r, CONV_TILE_ROWS)
            cv_s[pl.ds(r0, CONV_TILE_ROWS), lanes] = conv_tile(r0, CONV_TILE_ROWS, lanes, after)

    n_groups_t = n_chunks // LOCAL_CHUNKS
    for k, after in enumerate(local_stages(0)):
        conv_after(0, k, after)

    def pipelined(i, carry):
        g_local, g_state = local_stages(i), state_stages(i - 1)
        k = 0
        for which in PIPELINE_ORDER:
            if which == "L":
                conv_after(i, k, next(g_local))
                k += 1
            else:
                next(g_state)
        return carry

    for i in range(1, n_groups_t):
        pipelined(i, 0)
    for _ in state_stages(n_groups_t - 1):
        pass

    o_d = _silu(_layernorm(cv_s[...], clg_ref[...], clb_ref[...], LN_EPS))

    y = y_s[...]
    inv_n = 1.0 / RWKV_HEAD
    mu_y = _head_sums(y, hm) * inv_n
    yc = y - mu_y
    var_y = _head_sums(yc * yc, hm) * inv_n
    y = yc * lax.rsqrt(var_y + RWKV_GN_EPS) * gng_ref[...] + gnb_ref[...] + bonus
    o_c = y * gate

    m = (_dot(o_c.astype(BF16), wout_ref[0:RWKV_WIDTH, :])
         + _dot(o_d.astype(BF16), wout_ref[RWKV_WIDTH:RWKV_WIDTH + CONV_WIDTH, :]))
    o_ref[...] = x + m


def _odd_layer(x, norm_g, w_in, w_out, mu, w0, w2, a0, a2, g2, k_k, k_a, r_k, gn_g, gn_b,
               conv_w, conv_b, cln_g, cln_b, *, tb):
    B, T, D = x.shape
    pad_lr = RWKV_LR_COLS - (RWKV_DECAY_RANK + RWKV_AAA_RANK + RWKV_GATE_RANK)
    w_in = w_in.astype(BF16)
    w_in_r = jnp.concatenate(
        [w_in[:, :RWKV_IN], jnp.zeros((D, pad_lr), BF16), w_in[:, RWKV_IN:]], axis=1)
    mu_r = jnp.concatenate([mu.astype(F32), jnp.zeros((pad_lr,), F32)]).reshape(1, RWKV_COLS)
    wd = RWKV_WIDTH
    r1, r2 = RWKV_DECAY_RANK, RWKV_DECAY_RANK + RWKV_AAA_RANK
    wlr = jnp.zeros((RWKV_LR_COLS, 3 * wd), F32)
    wlr = wlr.at[0:r1, 0:wd].set(w2.astype(F32))
    wlr = wlr.at[r1:r2, wd:2 * wd].set(a2.astype(F32))
    wlr = wlr.at[r2:r2 + RWKV_GATE_RANK, 2 * wd:3 * wd].set(g2.astype(F32))
    row = lambda a: a.astype(F32).reshape(1, -1)
    const = lambda shape: pl.BlockSpec(shape, lambda b, t: (0,) * len(shape),
                                       pipeline_mode=pl.Buffered(1))
    gw = RWKV_GROUP * RWKV_HEAD
    kern = functools.partial(_odd_kernel, tb=tb)
    return pl.pallas_call(
        kern,
        grid=(B, T // tb),
        in_specs=[
            pl.BlockSpec((None, tb, D), lambda b, t: (b, t, 0)),
            const((1, D)), const((D, ODD_COLS)), const((1, RWKV_COLS)), const((RWKV_LR_COLS, 3 * wd)),
            const((1, wd)), const((1, wd)), const((1, wd)), const((1, wd)), const((1, wd)),
            const((1, wd)), const((1, wd)),
            const((CONV_KERNEL, CONV_WIDTH)), const((1, CONV_WIDTH)), const((1, CONV_WIDTH)),
            const((1, CONV_WIDTH)),
            const((D, D)),
        ],
        out_specs=pl.BlockSpec((None, tb, D), lambda b, t: (b, t, 0)),
        out_shape=jax.ShapeDtypeStruct((B, T, D), F32),
        scratch_shapes=[
            pltpu.VMEM((SUBLANE, RWKV_COLS), F32),
            pltpu.VMEM((tb + CONV_HALO + SUBLANE, CONV_WIDTH), F32),
            pltpu.VMEM((tb, CONV_WIDTH), F32),
            pltpu.VMEM((tb, wd), F32), pltpu.VMEM((tb, wd), F32),
            pltpu.VMEM((tb, wd), F32),
            pltpu.VMEM((tb, wd), BF16), pltpu.VMEM((tb, wd), BF16), pltpu.VMEM((tb, wd), BF16),
            pltpu.VMEM((tb, wd), BF16), pltpu.VMEM((tb, wd), BF16), pltpu.VMEM((tb, wd), BF16),
            pltpu.VMEM((tb, wd), BF16),
            pltpu.VMEM((RWKV_WIDTH // gw, gw, gw), F32),
            pltpu.VMEM((tb // RWKV_CHUNK, SUBLANE, RWKV_WIDTH), F32),
            pltpu.VMEM((tb // RWKV_CHUNK * (RWKV_WIDTH // gw), gw, 2 * gw), BF16),
            pltpu.VMEM((tb // RWKV_CHUNK * (RWKV_WIDTH // gw), gw, 2 * gw), BF16),
        ],
        compiler_params=pltpu.CompilerParams(
            dimension_semantics=("arbitrary", "arbitrary"), vmem_limit_bytes=VMEM_LIMIT),
        name="odd_mixer",
    )(x, row(norm_g), w_in_r, mu_r, wlr.astype(BF16), row(w0), row(a0), row(k_k), row(k_a), row(r_k),
      row(gn_g), row(gn_b), conv_w.astype(F32), row(conv_b), row(cln_g), row(cln_b), w_out.astype(BF16))


def _mlp_kernel(x_ref, g_ref, wup_ref, wdn_ref, gf_ref, o_ref, xb_ref, *, final_norm):
    f = pl.program_id(1)

    @pl.when(f == 0)
    def _():
        x = x_ref[...]
        xb_ref[...] = _rms(x, g_ref[...], RMS_EPS).astype(BF16)
        o_ref[...] = x

    hcol = _dot(xb_ref[...], wup_ref[...])
    hcol = jnp.square(jnp.maximum(hcol, 0.0))
    o_ref[...] += _dot(hcol.astype(BF16), wdn_ref[...])

    if final_norm:
        @pl.when(f == pl.num_programs(1) - 1)
        def _():
            o_ref[...] = _rms(o_ref[...], gf_ref[...], RMS_EPS)


def _mlp_layer(x2, norm_g, w_up, w_down, final_g, *, layer, tm, final_norm):
    M, D = x2.shape
    d_ff = w_up.shape[2]
    kern = functools.partial(_mlp_kernel, final_norm=final_norm)
    fc = MLP_FF_CHUNK
    return pl.pallas_call(
        kern,
        grid=(M // tm, d_ff // fc),
        in_specs=[
            pl.BlockSpec((tm, D), lambda i, f: (i, 0)),
            pl.BlockSpec((1, D), lambda i, f: (0, 0)),
            pl.BlockSpec((None, D, fc), lambda i, f: (layer, 0, f)),
            pl.BlockSpec((None, fc, D), lambda i, f: (layer, f, 0)),
            pl.BlockSpec((1, D), lambda i, f: (0, 0)),
        ],
        out_specs=pl.BlockSpec((tm, D), lambda i, f: (i, 0)),
        out_shape=jax.ShapeDtypeStruct((M, D), F32),
        scratch_shapes=[pltpu.VMEM((tm, D), BF16)],
        compiler_params=pltpu.CompilerParams(
            dimension_semantics=("arbitrary", "arbitrary"), vmem_limit_bytes=VMEM_LIMIT),
        name="mlp_final" if final_norm else "mlp",
    )(x2, norm_g.astype(F32).reshape(1, D), w_up, w_down, final_g.astype(F32).reshape(1, D))


def _pick_block(n, target):
    b = min(n, target)
    while n % b:
        b //= 2
    return b


def kernel(x, norm_mix, norm_ffn, w_up, w_down, norm_final, even_w_in, even_w_out, gla_w_alpha2, gla_b_alpha, gla_norm, sgu_ln_g, sgu_ln_b, sgu_w, sgu_b, odd_w_in, odd_w_out, rwkv_mu, rwkv_w0, rwkv_w2, rwkv_a0, rwkv_a2, rwkv_g2, rwkv_k_k, rwkv_k_a, rwkv_r_k, rwkv_gn_g, rwkv_gn_b, conv_w, conv_b, conv_ln_g, conv_ln_b):
    B, T, D = x.shape
    depth = norm_mix.shape[0]
    tb_even = _pick_block(T, EVEN_BLOCK)
    tb_odd = _pick_block(T, ODD_BLOCK)
    tm = _pick_block(B * T, MLP_BLOCK)
    assert tb_even % GLA_CHUNK == 0 and tb_even % SGU_CHUNK == 0
    assert tb_odd % (LOCAL_CHUNKS * RWKV_CHUNK) == 0
    assert D == GLA_WIDTH + SGU_WIDTH == RWKV_WIDTH + CONV_WIDTH
    w_up_b, w_down_b = w_up.astype(BF16), w_down.astype(BF16)
    for layer in range(depth):
        j = layer // 2
        if layer % 2 == 0:
            x = _even_layer(x, norm_mix[layer], even_w_in[j], even_w_out[j], gla_w_alpha2[j],
                            gla_b_alpha[j], gla_norm[j], sgu_ln_g[j], sgu_ln_b[j], sgu_w[j], sgu_b[j],
                            tb=tb_even)
        else:
            x = _odd_layer(x, norm_mix[layer], odd_w_in[j], odd_w_out[j], rwkv_mu[j], rwkv_w0[j],
                           rwkv_w2[j], rwkv_a0[j], rwkv_a2[j], rwkv_g2[j], rwkv_k_k[j], rwkv_k_a[j],
                           rwkv_r_k[j].reshape(-1), rwkv_gn_g[j], rwkv_gn_b[j], conv_w[j], conv_b[j],
                           conv_ln_g[j], conv_ln_b[j], tb=tb_odd)
        last = layer == depth - 1
        x = _mlp_layer(x.reshape(B * T, D), norm_ffn[layer], w_up_b, w_down_b, norm_final,
                       layer=layer, tm=tm, final_norm=last).reshape(B, T, D)
    return x
```
